```python
import math
import jax, jax.numpy as jnp
from jax import lax
import numpy as np

D_MODEL = 2048
BATCH = 4
SEQ = 2048
DEPTH = 1
DEC_BATCH = 128
DEC_SEQ = 1
PAST_LEN = 2048
PAGE_SIZE = 128

D_MIX = D_MODEL
D_ATTN = D_MIX // 2
D_SSM = D_MIX - D_ATTN
N_HEADS = 16
HEAD_DIM = D_ATTN // N_HEADS
N_KV = 4
GQ = N_HEADS // N_KV
D_KV = N_KV * HEAD_DIM
CMP_BLOCK = 32
CMP_STRIDE = 16
SEL_BLOCK = 64
N_SEL = 8
WINDOW = 512
Q_BLOCK = 128
FORCE = 1e4
SSM_GROUP = 16
N_SSM_GROUPS = D_SSM // SSM_GROUP
SSM_STATE = 64
MEM_LEN = 256
X_HEADS = 4
X_HEAD_DIM = D_MODEL // X_HEADS
PEER_KEYS = 128
N_EXPERTS = PEER_KEYS * PEER_KEYS
PEER_HEADS = 8
PEER_QDIM = 256
PEER_TOPK = 16
PEER_BLOCK = 128
D_IN = D_ATTN + 6 * D_KV + 3 * N_HEADS + D_SSM
EPS = 1e-6
NEG = -1e30
TINY = 1e-30

kernel_name = "hybrid_nsa_s5_peer_step"

F32 = jnp.float32


def rms_norm(x, g):
    x32 = x.astype(F32)
    y = x32 * lax.rsqrt(jnp.mean(x32 * x32, axis=-1, keepdims=True) + EPS) * g.astype(F32)
    return y.astype(x.dtype)


def masked_softmax(s, mask):
    s = jnp.where(mask, s, NEG)
    p = jnp.where(mask, jnp.exp(s - jnp.max(s, axis=-1, keepdims=True)), 0.0)
    return p / jnp.maximum(jnp.sum(p, axis=-1, keepdims=True), TINY)


def alibi_slopes():
    h = np.arange(1, N_HEADS + 1, dtype=np.float32)
    return jnp.asarray(2.0 ** (-8.0 * h / N_HEADS), dtype=F32).reshape(N_KV, GQ)


def sel_overlap(n_cmp, n_sel):
    i = np.arange(n_cmp)[:, None]
    j = np.arange(n_sel)[None, :]
    lo = np.maximum(i * CMP_STRIDE, j * SEL_BLOCK)
    hi = np.minimum(i * CMP_STRIDE + CMP_BLOCK, (j + 1) * SEL_BLOCK)
    return jnp.asarray(np.maximum(hi - lo, 0).astype(np.float32) / CMP_BLOCK)


def compress(k, w):
    bsz, L = k.shape[:2]
    lp = -(-L // CMP_STRIDE) * CMP_STRIDE
    k = jnp.pad(k, ((0, 0), (0, lp - L), (0, 0), (0, 0)))
    ch = k.reshape(bsz, lp // CMP_STRIDE, CMP_STRIDE, N_KV, HEAD_DIM)
    blocks = jnp.concatenate([ch[:, :-1], ch[:, 1:]], axis=2)
    return jnp.einsum('bnjhd,hj->bnhd', blocks, w)


def sel_blocks(k, n_sel):
    bsz, L = k.shape[:2]
    k = jnp.pad(k, ((0, 0), (0, n_sel * SEL_BLOCK - L), (0, 0), (0, 0)))
    return k.reshape(bsz, n_sel, SEL_BLOCK, N_KV, HEAD_DIM).transpose(0, 3, 1, 2, 4)


def nsa_attend(q, qpos, gates, ck, cv, kb, vb, kw, vw, kwpos):
    bsz, T = q.shape[:2]
    qg = q.reshape(bsz, T, N_KV, GQ, HEAD_DIM)
    scale = HEAD_DIM ** -0.5
    slope = alibi_slopes()
    nc = ck.shape[1]
    cend = jnp.arange(nc, dtype=jnp.int32) * CMP_STRIDE + (CMP_BLOCK - 1)
    dist_c = qpos[:, None] - cend[None, :]
    s_c = jnp.einsum('bthgd,bnhd->bhgtn', qg, ck).astype(F32) * scale \
        - slope[:, :, None, None] * dist_c.astype(F32)
    p_c = masked_softmax(s_c, dist_c >= 0)
    o_c = jnp.einsum('bhgtn,bnhd->bthgd', p_c.astype(cv.dtype), cv)
    n_sel = kb.shape[2]
    imp = jnp.einsum('bhgtn,nj->bhtj', p_c, sel_overlap(nc, n_sel))
    blk = jnp.arange(n_sel, dtype=jnp.int32)
    valid_blk = blk[None, :] * SEL_BLOCK <= qpos[:, None]
    forced = (blk[None, :] == (qpos // SEL_BLOCK)[:, None]) | (blk[None, :] == 0)
    imp = jnp.where(valid_blk, imp + jnp.where(forced, FORCE, 0.0), NEG)
    top_s, idx = lax.top_k(imp, N_SEL)
    ok = top_s > 0.5 * NEG
    bi = jnp.arange(bsz)[:, None, None, None]
    hi = jnp.arange(N_KV)[None, :, None, None]
    ks = kb[bi, hi, idx]
    vs = vb[bi, hi, idx]
    kpos = idx[..., None] * SEL_BLOCK + jnp.arange(SEL_BLOCK, dtype=jnp.int32)
    dist_s = qpos[None, None, :, None, None] - kpos
    mask_s = ok[..., None] & (dist_s >= 0)
    s_s = jnp.einsum('bthgd,bhtnkd->bhgtnk', qg, ks).astype(F32) * scale \
        - slope[None, :, :, None, None, None] * dist_s[:, :, None].astype(F32)
    flat = s_s.shape[:4] + (N_SEL * SEL_BLOCK,)
    p_s = masked_softmax(s_s.reshape(flat), jnp.broadcast_to(mask_s[:, :, None], s_s.shape).reshape(flat)).reshape(s_s.shape)
    o_s = jnp.einsum('bhgtnk,bhtnkd->bthgd', p_s.astype(vs.dtype), vs)
    dist_w = qpos[:, None] - kwpos[None, :]
    mask_w = (dist_w >= 0) & (dist_w < WINDOW) & (kwpos[None, :] >= 0)
    s_w = jnp.einsum('bthgd,bnhd->bhgtn', qg, kw).astype(F32) * scale \
        - slope[:, :, None, None] * dist_w.astype(F32)
    p_w = masked_softmax(s_w, mask_w)
    o_w = jnp.einsum('bhgtn,bnhd->bthgd', p_w.astype(vw.dtype), vw)
    shp = (bsz, T, N_HEADS, HEAD_DIM)
    o = gates[:, :, 0, :, None] * o_c.reshape(shp) + gates[:, :, 1, :, None] * o_s.reshape(shp) \
        + gates[:, :, 2, :, None] * o_w.reshape(shp)
    return o.astype(q.dtype).reshape(bsz, T, D_ATTN)


def nsa_prompt(q, gates, kc, vc, ks, vs, kw, vw, w_ck, w_cv):
    bsz, S = q.shape[:2]
    ck = compress(kc, w_ck)
    cv = compress(vc, w_cv)
    n_sel = max(-(-S // SEL_BLOCK), N_SEL)
    kb = sel_blocks(ks, n_sel)
    vb = sel_blocks(vs, n_sel)
    pad = ((0, 0), (WINDOW, 0), (0, 0), (0, 0))
    kw_pad = jnp.pad(kw, pad)
    vw_pad = jnp.pad(vw, pad)
    nb = S // Q_BLOCK
    qb = q.reshape(bsz, nb, Q_BLOCK, N_HEADS, HEAD_DIM).swapaxes(0, 1)
    gb = gates.reshape(bsz, nb, Q_BLOCK, 3, N_HEADS).swapaxes(0, 1)
    q0s = jnp.arange(nb, dtype=jnp.int32) * Q_BLOCK

    def one_block(args):
        qx, gx, q0 = args
        qpos = q0 + jnp.arange(Q_BLOCK, dtype=jnp.int32)
        kwin = lax.dynamic_slice_in_dim(kw_pad, q0, WINDOW + Q_BLOCK, axis=1)
        vwin = lax.dynamic_slice_in_dim(vw_pad, q0, WINDOW + Q_BLOCK, axis=1)
        kwpos = q0 - WINDOW + jnp.arange(WINDOW + Q_BLOCK, dtype=jnp.int32)
        return nsa_attend(qx, qpos, gx, ck, cv, kb, vb, kwin, vwin, kwpos)

    out = lax.map(one_block, (qb, gb, q0s))
    return out.swapaxes(0, 1).reshape(bsz, S, D_ATTN)


def _ssm_combine(e1, e2):
    a1r, a1i, b1r, b1i = e1
    a2r, a2i, b2r, b2i = e2
    return (a2r * a1r - a2i * a1i, a2r * a1i + a2i * a1r,
            a2r * b1r - a2i * b1i + b2r, a2r * b1i + a2i * b1r + b2i)


def s5_mixer(u, lp, h0_re, h0_im):
    bsz, T = u.shape[:2]
    lr = lp['lam_re'].astype(F32)
    li = lp['lam_im'].astype(F32)
    dt = jnp.exp(lp['log_dt'].astype(F32))[:, None]
    mag = jnp.exp(lr * dt)
    ar = mag * jnp.cos(li * dt)
    ai = mag * jnp.sin(li * dt)
    den = lr * lr + li * li
    fr = ((ar - 1.0) * lr + ai * li) / den
    fi = (ai * lr - (ar - 1.0) * li) / den
    br = lp['b_re'].astype(F32)
    bim = lp['b_im'].astype(F32)
    bbr = fr[..., None] * br - fi[..., None] * bim
    bbi = fr[..., None] * bim + fi[..., None] * br
    ug = u.astype(F32).reshape(bsz, T, N_SSM_GROUPS, SSM_GROUP)
    bur = jnp.einsum('btgp,gnp->btgn', ug, bbr)
    bui = jnp.einsum('btgp,gnp->btgn', ug, bbi)
    a_r = jnp.broadcast_to(ar, bur.shape)
    a_i = jnp.broadcast_to(ai, bur.shape)
    _, _, hr, hi = lax.associative_scan(_ssm_combine, (a_r, a_i, bur, bui), axis=1)
    if h0_re is not None:
        k = jnp.arange(1, T + 1, dtype=F32)[:, None, None]
        pm = jnp.exp(lr * dt * k)
        pr = pm * jnp.cos(li * dt * k)
        pi = pm * jnp.sin(li * dt * k)
        h0r = h0_re.astype(F32)[:, None]
        h0i = h0_im.astype(F32)[:, None]
        hr = hr + pr * h0r - pi * h0i
        hi = hi + pr * h0i + pi * h0r
    y = jnp.einsum('btgn,gpn->btgp', hr, lp['c_re'].astype(F32)) \
        - jnp.einsum('btgn,gpn->btgp', hi, lp['c_im'].astype(F32)) \
        + lp['d_skip'].astype(F32).reshape(N_SSM_GROUPS, SSM_GROUP) * ug
    y = jax.nn.gelu(y.reshape(bsz, T, D_SSM))
    y = y * jax.nn.sigmoid(y @ lp['w_glu'].astype(F32))
    return y.astype(u.dtype), hr[:, -1], hi[:, -1]


def split_in(z):
    bsz, T = z.shape[:2]
    q = z[..., :D_ATTN].reshape(bsz, T, N_HEADS, HEAD_DIM)
    o = D_ATTN
    kv = z[..., o:o + 6 * D_KV].reshape(bsz, T, 6, N_KV, HEAD_DIM)
    o += 6 * D_KV
    gates = jax.nn.sigmoid(z[..., o:o + 3 * N_HEADS].astype(F32)).reshape(bsz, T, 3, N_HEADS)
    o += 3 * N_HEADS
    u = z[..., o:]
    return (q, kv[:, :, 0], kv[:, :, 1], kv[:, :, 2], kv[:, :, 3], kv[:, :, 4], kv[:, :, 5], gates, u)


def merge_heads(o_a, o_s, lp):
    o = jnp.concatenate([rms_norm(o_a, lp['g_attn_out']), rms_norm(o_s, lp['g_ssm_out'])], axis=-1)
    return o @ lp['w_out']


def mem_kv(mem, lp):
    bsz, m = mem.shape[:2]
    mn = rms_norm(mem, lp['g_mem'])
    mk = (mn @ lp['w_xk']).reshape(bsz, m, X_HEADS, X_HEAD_DIM)
    mv = (mn @ lp['w_xv']).reshape(bsz, m, X_HEADS, X_HEAD_DIM)
    return mk, mv


def cross_attn(hn, mk, mv, lp):
    bsz, T = hn.shape[:2]
    q = (hn @ lp['w_xq']).reshape(bsz, T, X_HEADS, X_HEAD_DIM)
    s = jnp.einsum('bthd,bmhd->bhtm', q, mk).astype(F32) * (X_HEAD_DIM ** -0.5)
    p = jax.nn.softmax(s, axis=-1)
    o = jnp.einsum('bhtm,bmhd->bthd', p.astype(mv.dtype), mv).reshape(bsz, T, D_MODEL)
    return o @ lp['w_xo']


def peer_ffn(xn, lp):
    shp = xn.shape
    xf = xn.reshape(-1, D_MODEL)
    n = xf.shape[0]
    nb = -(-n // PEER_BLOCK)
    xf = jnp.pad(xf, ((0, nb * PEER_BLOCK - n), (0, 0))).reshape(nb, PEER_BLOCK, D_MODEL)
    half = PEER_QDIM // 2
    w_pq, k1, k2, u_tab, v_tab = lp['w_pq'], lp['peer_k1'], lp['peer_k2'], lp['peer_u'], lp['peer_v']

    def one(xb):
        q = (xb @ w_pq).reshape(PEER_BLOCK, PEER_HEADS, 2, half)
        s1 = jnp.einsum('thd,kd->thk', q[:, :, 0], k1).astype(F32)
        s2 = jnp.einsum('thd,kd->thk', q[:, :, 1], k2).astype(F32)
        t1, i1 = lax.top_k(s1, PEER_TOPK)
        t2, i2 = lax.top_k(s2, PEER_TOPK)
        cand = (t1[..., :, None] + t2[..., None, :]).reshape(PEER_BLOCK, PEER_HEADS, PEER_TOPK * PEER_TOPK)
        cidx = (i1[..., :, None] * PEER_KEYS + i2[..., None, :]).reshape(PEER_BLOCK, PEER_HEADS, PEER_TOPK * PEER_TOPK)
        top, pos = lax.top_k(cand, PEER_TOPK)
        eidx = jnp.take_along_axis(cidx, pos, axis=-1)
        g = jax.nn.softmax(top, axis=-1)
        u = u_tab[eidx]
        v = v_tab[eidx]
        a = jax.nn.gelu(jnp.einsum('thkd,td->thk', u, xb).astype(F32))
        return jnp.einsum('thk,thkd->td', (g * a).astype(v.dtype), v)

    out = lax.map(one, xf).reshape(nb * PEER_BLOCK, D_MODEL)[:n]
    return out.reshape(shp)


def gather_pages(pool, page_table):
    g = pool[page_table]
    return g.reshape(g.shape[0], g.shape[1] * g.shape[2], N_KV, HEAD_DIM)


def layer_prompt(h, mem, lp):
    q, kc, vc, ks, vs, kw, vw, gates, u = split_in(rms_norm(h, lp['g_mix']) @ lp['w_in'])
    o_a = nsa_prompt(q, gates, kc, vc, ks, vs, kw, vw, lp['w_cmp_k'], lp['w_cmp_v'])
    o_s, sr, si = s5_mixer(u, lp, None, None)
    h = h + merge_heads(o_a, o_s, lp)
    mk, mv = mem_kv(mem, lp)
    h = h + cross_attn(rms_norm(h, lp['g_x']), mk, mv, lp)
    h = h + peer_ffn(rms_norm(h, lp['g_ffn']), lp)
    wl = min(WINDOW, h.shape[1])
    return h, (kc, vc, ks, vs, kw[:, -wl:], vw[:, -wl:], sr, si, mk, mv)


def layer_sample(h, lp, pool_kc, pool_vc, pool_ks, pool_vs, buf_kw, buf_vw, s_re, s_im, mk, mv, page_table):
    T = h.shape[1]
    past = page_table.shape[1] * PAGE_SIZE
    q, kc, vc, ks, vs, kw, vw, gates, u = split_in(rms_norm(h, lp['g_mix']) @ lp['w_in'])
    kc_all = jnp.concatenate([gather_pages(pool_kc, page_table), kc], axis=1)
    vc_all = jnp.concatenate([gather_pages(pool_vc, page_table), vc], axis=1)
    ks_all = jnp.concatenate([gather_pages(pool_ks, page_table), ks], axis=1)
    vs_all = jnp.concatenate([gather_pages(pool_vs, page_table), vs], axis=1)
    ck = compress(kc_all, lp['w_cmp_k'])
    cv = compress(vc_all, lp['w_cmp_v'])
    n_sel = max(-(-(past + T) // SEL_BLOCK), N_SEL)
    kb = sel_blocks(ks_all, n_sel)
    vb = sel_blocks(vs_all, n_sel)
    wbuf = buf_kw.shape[1]
    kw_all = jnp.concatenate([buf_kw, kw], axis=1)
    vw_all = jnp.concatenate([buf_vw, vw], axis=1)
    kwpos = past - wbuf + jnp.arange(wbuf + T, dtype=jnp.int32)
    qpos = past + jnp.arange(T, dtype=jnp.int32)
    o_a = nsa_attend(q, qpos, gates, ck, cv, kb, vb, kw_all, vw_all, kwpos)
    o_s, sr, si = s5_mixer(u, lp, s_re, s_im)
    h = h + merge_heads(o_a, o_s, lp)
    h = h + cross_attn(rms_norm(h, lp['g_x']), mk, mv, lp)
    h = h + peer_ffn(rms_norm(h, lp['g_ffn']), lp)
    wl = min(WINDOW, past + T)
    return h, (kc, vc, ks, vs, kw_all[:, -wl:], vw_all[:, -wl:], sr, si)


def setup_inputs(seed: int = 0) -> dict:
    key = jax.random.key(seed)
    keys = iter(jax.random.split(key, 64))

    def nrm(shape, scale):
        return jax.random.normal(next(keys), shape, F32) * scale

    def gain(shape):
        return 1.0 + nrm(shape, 0.01)

    n_pages = PAST_LEN // PAGE_SIZE
    n_used = DEC_BATCH * n_pages
    n_phys = n_used + -(-n_used // 4)
    wbuf = min(WINDOW, PAST_LEN)
    L = DEPTH
    G, N, P = N_SSM_GROUPS, SSM_STATE, SSM_GROUP
    inp = {}
    inp['x_prompt'] = nrm((BATCH, SEQ, D_MODEL), 1.0)
    inp['x_sample'] = nrm((DEC_BATCH, DEC_SEQ, D_MODEL), 1.0)
    inp['mem_prompt'] = nrm((BATCH, MEM_LEN, D_MODEL), 1.0)
    inp['cache_k_cmp'] = nrm((L, n_phys, PAGE_SIZE, N_KV, HEAD_DIM), 1.0)
    inp['cache_v_cmp'] = nrm((L, n_phys, PAGE_SIZE, N_KV, HEAD_DIM), 1.0)
    inp['cache_k_sel'] = nrm((L, n_phys, PAGE_SIZE, N_KV, HEAD_DIM), 1.0)
    inp['cache_v_sel'] = nrm((L, n_phys, PAGE_SIZE, N_KV, HEAD_DIM), 1.0)
    inp['cache_k_win'] = nrm((L, DEC_BATCH, wbuf, N_KV, HEAD_DIM), 1.0)
    inp['cache_v_win'] = nrm((L, DEC_BATCH, wbuf, N_KV, HEAD_DIM), 1.0)
    inp['state_s5_re'] = nrm((L, DEC_BATCH, G, N), 1.0)
    inp['state_s5_im'] = nrm((L, DEC_BATCH, G, N), 1.0)
    inp['cache_mem_k'] = nrm((L, DEC_BATCH, MEM_LEN, X_HEADS, X_HEAD_DIM), 1.0)
    inp['cache_mem_v'] = nrm((L, DEC_BATCH, MEM_LEN, X_HEADS, X_HEAD_DIM), 1.0)
    inp['page_table'] = jax.random.permutation(next(keys), n_phys)[:n_used].reshape(DEC_BATCH, n_pages).astype(jnp.int32)
    inp['g_mix'] = gain((L, D_MODEL))
    inp['w_in'] = nrm((L, D_MODEL, D_IN), D_MODEL ** -0.5)
    inp['w_cmp_k'] = (1.0 + nrm((L, N_KV, CMP_BLOCK), 0.1)) * CMP_BLOCK ** -0.5
    inp['w_cmp_v'] = (1.0 + nrm((L, N_KV, CMP_BLOCK), 0.1)) * CMP_BLOCK ** -0.5
    inp['lam_re'] = -0.5 + nrm((L, G, N), 0.01)
    inp['lam_im'] = math.pi * jnp.arange(N, dtype=F32) + nrm((L, G, N), 0.01)
    inp['log_dt'] = jax.random.uniform(next(keys), (L, G), F32, math.log(1e-3), math.log(1e-1))
    inp['b_re'] = nrm((L, G, N, P), (2 * P) ** -0.5)
    inp['b_im'] = nrm((L, G, N, P), (2 * P) ** -0.5)
    inp['c_re'] = nrm((L, G, P, N), N ** -0.5)
    inp['c_im'] = nrm((L, G, P, N), N ** -0.5)
    inp['d_skip'] = nrm((L, D_SSM), 1.0)
    inp['w_glu'] = nrm((L, D_SSM, D_SSM), D_SSM ** -0.5)
    inp['g_attn_out'] = gain((L, D_ATTN))
    inp['g_ssm_out'] = gain((L, D_SSM))
    inp['w_out'] = nrm((L, D_MIX, D_MODEL), D_MIX ** -0.5)
    inp['g_x'] = gain((L, D_MODEL))
    inp['g_mem'] = gain((L, D_MODEL))
    inp['w_xq'] = nrm((L, D_MODEL, D_MODEL), D_MODEL ** -0.5)
    inp['w_xk'] = nrm((L, D_MODEL, D_MODEL), D_MODEL ** -0.5)
    inp['w_xv'] = nrm((L, D_MODEL, D_MODEL), D_MODEL ** -0.5)
    inp['w_xo'] = nrm((L, D_MODEL, D_MODEL), D_MODEL ** -0.5)
    inp['g_ffn'] = gain((L, D_MODEL))
    inp['w_pq'] = nrm((L, D_MODEL, PEER_HEADS * PEER_QDIM), D_MODEL ** -0.5)
    inp['peer_k1'] = nrm((L, PEER_KEYS, PEER_QDIM // 2), (PEER_QDIM // 2) ** -0.5)
    inp['peer_k2'] = nrm((L, PEER_KEYS, PEER_QDIM // 2), (PEER_QDIM // 2) ** -0.5)
    inp['peer_u'] = nrm((L, N_EXPERTS, D_MODEL), D_MODEL ** -0.5)
    inp['peer_v'] = nrm((L, N_EXPERTS, D_MODEL), PEER_HEADS ** -0.5)
    inp['g_final'] = gain((D_MODEL,))
    return inp


def reference(x_prompt, x_sample, mem_prompt, cache_k_cmp, cache_v_cmp, cache_k_sel, cache_v_sel,
              cache_k_win, cache_v_win, state_s5_re, state_s5_im, cache_mem_k, cache_mem_v, page_table,
              g_mix, w_in, w_cmp_k, w_cmp_v, lam_re, lam_im, log_dt, b_re, b_im, c_re, c_im, d_skip, w_glu,
              g_attn_out, g_ssm_out, w_out, g_x, g_mem, w_xq, w_xk, w_xv, w_xo, g_ffn, w_pq,
              peer_k1, peer_k2, peer_u, peer_v, g_final):
    h_p = x_prompt
    h_s = x_sample
    p_list = []
    s_list = []
    for l in range(DEPTH):
        lp = dict(g_mix=g_mix[l], w_in=w_in[l], w_cmp_k=w_cmp_k[l], w_cmp_v=w_cmp_v[l],
                  lam_re=lam_re[l], lam_im=lam_im[l], log_dt=log_dt[l], b_re=b_re[l], b_im=b_im[l],
                  c_re=c_re[l], c_im=c_im[l], d_skip=d_skip[l], w_glu=w_glu[l],
                  g_attn_out=g_attn_out[l], g_ssm_out=g_ssm_out[l], w_out=w_out[l],
                  g_x=g_x[l], g_mem=g_mem[l], w_xq=w_xq[l], w_xk=w_xk[l], w_xv=w_xv[l], w_xo=w_xo[l],
                  g_ffn=g_ffn[l], w_pq=w_pq[l], peer_k1=peer_k1[l], peer_k2=peer_k2[l],
                  peer_u=peer_u[l], peer_v=peer_v[l])
        h_p, st_p = layer_prompt(h_p, mem_prompt, lp)
        h_s, st_s = layer_sample(h_s, lp, cache_k_cmp[l], cache_v_cmp[l], cache_k_sel[l], cache_v_sel[l],
                                 cache_k_win[l], cache_v_win[l], state_s5_re[l], state_s5_im[l],
                                 cache_mem_k[l], cache_mem_v[l], page_table)
        p_list.append(st_p)
        s_list.append(st_s)
    y_prompt = rms_norm(h_p, g_final)
    y_sample = rms_norm(h_s, g_final)
    (p_k_cmp, p_v_cmp, p_k_sel, p_v_sel, p_k_win, p_v_win, p_s5_re, p_s5_im, p_mem_k, p_mem_v) = \
        [jnp.stack(s) for s in zip(*p_list)]
    (s_k_cmp, s_v_cmp, s_k_sel, s_v_sel, s_k_win, s_v_win, s_s5_re, s_s5_im) = \
        [jnp.stack(s) for s in zip(*s_list)]
    return (y_prompt, y_sample,
            p_k_cmp, p_v_cmp, p_k_sel, p_v_sel, p_k_win, p_v_win, p_s5_re, p_s5_im, p_mem_k, p_mem_v,
            s_k_cmp, s_v_cmp, s_k_sel, s_v_sel, s_k_win, s_v_win, s_s5_re, s_s5_im)
```

```python
import functools
import math

import jax
import jax.numpy as jnp
import numpy as np
from jax import lax
from jax.experimental import pallas as pl
from jax.experimental.pallas import tpu as pltpu

F32 = jnp.float32
BF16 = jnp.bfloat16

D_MODEL = 2048
N_HEADS = 16
HEAD_DIM = 64
N_KV = 4
GQ = 4
D_ATTN = 1024
D_SSM = 1024
D_KV = 256
CMP_BLOCK = 32
CMP_STRIDE = 16
SEL_BLOCK = 64
N_SEL = 8
WINDOW = 512
Q_BLOCK = 128
FORCE = 1e4
N_SSM_GROUPS = 64
SSM_GROUP = 16
SSM_STATE = 64
SSM_BANDS = 4
MEM_LEN = 256
X_HEADS = 4
X_HEAD_DIM = 512
PEER_KEYS = 128
PEER_HEADS = 8
PEER_TOPK = 16
PAGE = 128
EPS = 1e-6
NEG = -1e30
TINY = 1e-30
LOWEST = -3.0e38
GATE_PAD = 128
VMEM_LIMIT = 56 * 2**20


def _params(sem):
    return pltpu.CompilerParams(dimension_semantics=sem, vmem_limit_bytes=VMEM_LIMIT)


def _full(a):
    nd = a.ndim
    return pl.BlockSpec(a.shape, lambda *_: (0,) * nd)


def _rms(x, g):
    return x * lax.rsqrt(jnp.mean(x * x, axis=-1, keepdims=True) + EPS) * g


def _dot(a, b):
    return jnp.dot(a, b, preferred_element_type=F32)


def _dot_nt(a, b):
    return lax.dot_general(a, b, (((1,), (1,)), ((), ())), preferred_element_type=F32)


def _dot3(a, b_exact):
    hi = a.astype(BF16)
    r1 = a - hi.astype(F32)
    mid = r1.astype(BF16)
    lo = (r1 - mid.astype(F32)).astype(BF16)
    return _dot(hi, b_exact) + _dot(mid, b_exact) + _dot(lo, b_exact)


def _iota(shape, dim):
    return lax.broadcasted_iota(jnp.int32, shape, dim)


def _proj_attn_kernel(x_ref, g_ref, w_ref, q_ref, *rest):
    kv_refs, kvh_refs, gate_ref = rest[:6], rest[6:12], rest[12]
    xn = _rms(x_ref[...], g_ref[...]).astype(BF16)
    z = _dot(xn, w_ref[...])
    for hd in range(N_HEADS):
        q_ref[hd] = (z[:, hd * HEAD_DIM:(hd + 1) * HEAD_DIM] * (HEAD_DIM ** -0.5)).astype(BF16)
    for k in range(6):
        zk = z[:, D_ATTN + D_KV * k:D_ATTN + D_KV * (k + 1)]
        kv_refs[k][...] = zk
        for h in range(N_KV):
            kvh_refs[k][h] = zk[:, h * HEAD_DIM:(h + 1) * HEAD_DIM].astype(BF16)
    gate_ref[...] = jax.nn.sigmoid(z[:, D_ATTN + 6 * D_KV:])


def _proj_attn(x, g, w, tm):
    m = x.shape[0]
    row = lambda n: pl.BlockSpec((tm, n), lambda i: (i, 0))
    hm = lambda n: pl.BlockSpec((n, tm, HEAD_DIM), lambda i: (0, i, 0))
    out_shape = ([jax.ShapeDtypeStruct((N_HEADS, m, HEAD_DIM), BF16)]
                 + [jax.ShapeDtypeStruct((m, D_KV), F32)] * 6
                 + [jax.ShapeDtypeStruct((N_KV, m, HEAD_DIM), BF16)] * 6
                 + [jax.ShapeDtypeStruct((m, GATE_PAD), F32)])
    out_specs = [hm(N_HEADS)] + [row(D_KV)] * 6 + [hm(N_KV)] * 6 + [row(GATE_PAD)]
    outs = pl.pallas_call(
        _proj_attn_kernel, grid=(m // tm,),
        in_specs=[row(D_MODEL), _full(g), _full(w)],
        out_specs=out_specs, out_shape=out_shape,
        compiler_params=_params(("parallel",)), name="proj_attn")(x, g, w)
    return outs[0], outs[1:7], outs[7:13], outs[13]


def _proj_u_kernel(x_ref, g_ref, w_ref, u_ref):
    xn = _rms(x_ref[...], g_ref[...]).astype(BF16)
    u_ref[...] = _dot(xn, w_ref[...])


def _proj_u(x, g, w, nb, tm):
    m = x.shape[0]
    t = m // nb
    nt = t // tm
    return pl.pallas_call(
        _proj_u_kernel, grid=(nb, nt),
        in_specs=[pl.BlockSpec((tm, D_MODEL), lambda b, i: (b * nt + i, 0)), _full(g), _full(w)],
        out_specs=pl.BlockSpec((tm, D_SSM), lambda b, i: (i, b)),
        out_shape=jax.ShapeDtypeStruct((t, nb * D_SSM), F32),
        compiler_params=_params(("parallel", "parallel")), name="proj_u")(x, g, w)


def _mm_kernel(*refs, norm, res, gate_cols):
    it = iter(refs)
    x_ref = next(it)
    g_ref = next(it) if norm else None
    w_ref = next(it)
    r_ref = next(it) if res else None
    o_ref = next(it)
    x = x_ref[...]
    if norm:
        x = _rms(x.astype(F32), g_ref[...])
    z = _dot(x.astype(BF16), w_ref[...])
    if res:
        z = z + r_ref[...]
    if gate_cols is not None:
        col = _iota(z.shape, 1)
        z = jnp.where((col >= gate_cols[0]) & (col < gate_cols[1]), jax.nn.sigmoid(z), z)
    o_ref[...] = z.astype(o_ref.dtype)


def _mm(x, w, tm, out_dtype, g=None, res=None, gate_cols=None, name="mm"):
    m, k = x.shape
    n = w.shape[1]
    row = lambda c: pl.BlockSpec((tm, c), lambda i: (i, 0))
    args, specs = [x], [row(k)]
    if g is not None:
        args.append(g)
        specs.append(_full(g))
    args.append(w)
    specs.append(_full(w))
    if res is not None:
        args.append(res)
        specs.append(row(n))
    return pl.pallas_call(
        functools.partial(_mm_kernel, norm=g is not None, res=res is not None, gate_cols=gate_cols),
        grid=(m // tm,), in_specs=specs, out_specs=row(n),
        out_shape=jax.ShapeDtypeStruct((m, n), out_dtype),
        compiler_params=_params(("parallel",)), name=name)(*args)


def _mm2_kernel(x_ref, g_ref, w_ref, o0_ref, o1_ref):
    xn = _rms(x_ref[...], g_ref[...]).astype(BF16)
    z = _dot(xn, w_ref[...])
    n = o0_ref.shape[1]
    o0_ref[...] = z[:, :n]
    o1_ref[...] = z[:, n:]


def _mem_kv(mem, g, w, tm):
    m = mem.shape[0]
    row = pl.BlockSpec((tm, D_MODEL), lambda i: (i, 0))
    return pl.pallas_call(
        _mm2_kernel, grid=(m // tm,), in_specs=[row, _full(g), _full(w)], out_specs=[row, row],
        out_shape=[jax.ShapeDtypeStruct((m, D_MODEL), F32)] * 2,
        compiler_params=_params(("parallel",)), name="mem_kv")(mem, g, w)


def _glu_kernel(y_ref, w_ref, o_ref):
    y = y_ref[...]
    o_ref[...] = y * jax.nn.sigmoid(_dot(y.astype(BF16), w_ref[...]))


def _glu(y_tm, w, nb, tm):
    t = y_tm.shape[0]
    nt = t // tm
    return pl.pallas_call(
        _glu_kernel, grid=(nb, nt),
        in_specs=[pl.BlockSpec((tm, D_SSM), lambda b, i: (i, b)), _full(w)],
        out_specs=pl.BlockSpec((tm, D_SSM), lambda b, i: (b * nt + i, 0)),
        out_shape=jax.ShapeDtypeStruct((nb * t, D_SSM), F32),
        compiler_params=_params(("parallel", "parallel")), name="glu")(y_tm, w)


def _merge_kernel(oa_ref, os_ref, ga_ref, gs_ref, wa_ref, ws_ref, x_ref, o_ref):
    a = _rms(oa_ref[...], ga_ref[...]).astype(BF16)
    s = _rms(os_ref[...], gs_ref[...]).astype(BF16)
    o_ref[...] = x_ref[...] + (_dot(a, wa_ref[...]) + _dot(s, ws_ref[...]))


def _merge(o_a, o_s, g_a, g_s, w_a, w_s, x, tm):
    m = x.shape[0]
    half = pl.BlockSpec((tm, D_ATTN), lambda i: (i, 0))
    row = pl.BlockSpec((tm, D_MODEL), lambda i: (i, 0))
    return pl.pallas_call(
        _merge_kernel, grid=(m // tm,),
        in_specs=[half, half, _full(g_a), _full(g_s), _full(w_a), _full(w_s), row],
        out_specs=row, out_shape=jax.ShapeDtypeStruct((m, D_MODEL), F32),
        compiler_params=_params(("parallel",)), name="merge_heads")(o_a, o_s, g_a, g_s, w_a, w_s, x)


def _pool16(x_ref, w_ref, tokens):
    halves = []
    for hf in range(2):
        lanes = slice(hf * 128, (hf + 1) * 128)
        a = b = None
        for j in range(CMP_STRIDE):
            xj = x_ref[pl.ds(2 * j + hf, tokens // CMP_STRIDE, stride=2 * CMP_STRIDE), :]
            ta = xj * w_ref[j:j + 1, lanes]
            tb = xj * w_ref[CMP_STRIDE + j:CMP_STRIDE + j + 1, lanes]
            a = ta if a is None else a + ta
            b = tb if b is None else b + tb
        halves.append((a, b))
    return (jnp.concatenate([halves[0][0], halves[1][0]], axis=1),
            jnp.concatenate([halves[0][1], halves[1][1]], axis=1))


def _shift_up(b, last_row):
    n = b.shape[0]
    rolled = pltpu.roll(b, n - 1, 0)
    return jnp.where(_iota(b.shape, 0) == n - 1, last_row, rolled)


def _compress_kernel(k_ref, v_ref, wk_ref, wv_ref, ck_ref, cv_ref):
    tokens = k_ref.shape[0] // 2
    for x_ref, w_ref, o_ref in ((k_ref, wk_ref, ck_ref), (v_ref, wv_ref, cv_ref)):
        a, b = _pool16(x_ref, w_ref, tokens)
        c = (a + _shift_up(b, 0.0)).astype(BF16)
        for h in range(N_KV):
            o_ref[0, h] = c[:, h * HEAD_DIM:(h + 1) * HEAD_DIM]


def _compress(kc, vc, wk, wv, nb):
    s = kc.shape[0] // nb
    nc = s // CMP_STRIDE
    kc = kc.reshape(2 * nb * s, D_KV // 2)
    vc = vc.reshape(2 * nb * s, D_KV // 2)
    row = pl.BlockSpec((2 * s, D_KV // 2), lambda b: (b, 0))
    out = pl.BlockSpec((1, N_KV, nc, HEAD_DIM), lambda b: (b, 0, 0, 0))
    return pl.pallas_call(
        _compress_kernel, grid=(nb,), in_specs=[row, row, _full(wk), _full(wv)], out_specs=[out, out],
        out_shape=[jax.ShapeDtypeStruct((nb, N_KV, nc, HEAD_DIM), BF16)] * 2,
        compiler_params=_params(("parallel",)), name="nsa_compress")(kc, vc, wk, wv)


def _overlap_matrix(nc, width, n_cmp, n_sel):
    i = _iota((nc, width), 0)
    j = _iota((nc, width), 1)
    lo = jnp.maximum(i * CMP_STRIDE, j * SEL_BLOCK)
    hi = jnp.minimum(i * CMP_STRIDE + CMP_BLOCK, (j + 1) * SEL_BLOCK)
    ov = jnp.maximum(hi - lo, 0).astype(F32) * (1.0 / CMP_BLOCK)
    return jnp.where((i < n_cmp) & (j < n_sel), ov, 0.0).astype(BF16)


def _select_blocks(imp, qpos, n_sel):
    blk = _iota(imp.shape, 1)
    valid = blk * SEL_BLOCK <= qpos
    forced = (blk == qpos // SEL_BLOCK) | (blk == 0)
    x = jnp.where(valid, imp + jnp.where(forced, FORCE, 0.0), NEG)
    x = jnp.where(blk < n_sel, x, LOWEST)
    sel = jnp.zeros(imp.shape, F32)
    for _ in range(N_SEL):
        m = jnp.max(x, axis=-1, keepdims=True)
        first = jnp.min(jnp.where(x == m, blk, 4 * imp.shape[1]), axis=-1, keepdims=True)
        pick = blk == first
        sel = jnp.where(pick & (m > 0.5 * NEG), 1.0, sel)
        x = jnp.where(pick, LOWEST, x)
    return sel


def _softmax_step(carry, s, mask, v):
    m, l, acc = carry
    s = jnp.where(mask, s, NEG)
    m_new = jnp.maximum(m, jnp.max(s, axis=-1, keepdims=True))
    p = jnp.where(mask, jnp.exp(s - m_new), 0.0)
    alpha = jnp.exp(m - m_new)
    l = alpha * l + jnp.sum(p, axis=-1, keepdims=True)
    acc = alpha * acc + _dot(p.astype(BF16), v)
    return m_new, l, acc


def _softmax_init(rows, width):
    return jnp.full((rows, 1), NEG, F32), jnp.zeros((rows, 1), F32), jnp.zeros((rows, width), F32)


def _alibi_slope(head):
    return float(2.0 ** (-8.0 * (head + 1) / N_HEADS))


def _nsa_prompt_kernel(q_ref, ck_ref, cv_ref, ks_ref, vs_ref, kw_ref, vw_ref, gate_ref, o_ref, *, kt, nc):
    qi = pl.program_id(1)
    rows = GQ * Q_BLOCK
    q0 = qi * Q_BLOCK
    r = _iota((rows, 1), 0)
    qpos = q0 + (r & (Q_BLOCK - 1))
    qpos_tok = q0 + _iota((Q_BLOCK, 1), 0)
    n_sel = ks_ref.shape[1] // SEL_BLOCK
    ov = _overlap_matrix(nc, 128, nc - 1, n_sel)
    cend = _iota((1, nc), 1) * CMP_STRIDE + (CMP_BLOCK - 1)
    n_hi = (q0 + Q_BLOCK + kt - 1) // kt
    w_lo = jnp.maximum(q0 - (WINDOW - 1), 0) // kt
    gates = gate_ref[...]
    for h in range(N_KV):
        slope = jnp.zeros((rows, 1), F32)
        for g in range(GQ):
            slope = jnp.where(r // Q_BLOCK == g, _alibi_slope(h * GQ + g), slope)
        q = q_ref[h * GQ:(h + 1) * GQ].reshape(rows, HEAD_DIM)
        dist_c = qpos - cend
        mask_c = dist_c >= 0
        s = _dot_nt(q, ck_ref[0, h]) - slope * dist_c.astype(F32)
        s = jnp.where(mask_c, s, NEG)
        p = jnp.where(mask_c, jnp.exp(s - jnp.max(s, axis=-1, keepdims=True)), 0.0)
        p = p / jnp.maximum(jnp.sum(p, axis=-1, keepdims=True), TINY)
        o_c = _dot(p.astype(BF16), cv_ref[0, h])
        p_grp = p[0:Q_BLOCK]
        for g in range(1, GQ):
            p_grp = p_grp + p[g * Q_BLOCK:(g + 1) * Q_BLOCK]
        sel = _select_blocks(_dot3(p_grp, ov), qpos_tok, n_sel).astype(BF16)

        def tile(t, carry, k_ref, v_ref, selected):
            k0 = pl.multiple_of(t * kt, kt)
            kpos = k0 + _iota((1, kt), 1)
            dist = qpos - kpos
            s = _dot_nt(q, k_ref[h, pl.ds(k0, kt), :]) - slope * dist.astype(F32)
            if selected:
                expand = (_iota((128, kt), 0) == (k0 + _iota((128, kt), 1)) // SEL_BLOCK)
                hit = _dot(sel, jnp.where(expand, 1.0, 0.0).astype(BF16))
                mask = (dist >= 0) & (jnp.concatenate([hit] * GQ, axis=0) > 0.5)
            else:
                mask = (dist >= 0) & (dist < WINDOW)
            return _softmax_step(carry, s, mask, v_ref[h, pl.ds(k0, kt), :])

        init = _softmax_init(rows, HEAD_DIM)
        _, l_s, a_s = lax.fori_loop(0, n_hi, functools.partial(tile, k_ref=ks_ref, v_ref=vs_ref, selected=True), init)
        _, l_w, a_w = lax.fori_loop(w_lo, n_hi, functools.partial(tile, k_ref=kw_ref, v_ref=vw_ref, selected=False), init)
        o_s = a_s / jnp.maximum(l_s, TINY)
        o_w = a_w / jnp.maximum(l_w, TINY)
        for g in range(GQ):
            hd = h * GQ + g
            sl = slice(g * Q_BLOCK, (g + 1) * Q_BLOCK)
            o = (gates[:, hd:hd + 1] * o_c[sl] + gates[:, N_HEADS + hd:N_HEADS + hd + 1] * o_s[sl]
                 + gates[:, 2 * N_HEADS + hd:2 * N_HEADS + hd + 1] * o_w[sl])
            o_ref[:, hd * HEAD_DIM:(hd + 1) * HEAD_DIM] = o


def _nsa_prompt(q_hm, ck, cv, ks, vs, kw, vw, gates, nb, kt=256):
    m = gates.shape[0]
    s = m // nb
    nq = s // Q_BLOCK
    nc = ck.shape[2]
    kv = pl.BlockSpec((N_KV, s, HEAD_DIM), lambda b, i: (0, b, 0))
    cmp_spec = pl.BlockSpec((1, N_KV, nc, HEAD_DIM), lambda b, i: (b, 0, 0, 0))
    return pl.pallas_call(
        functools.partial(_nsa_prompt_kernel, kt=kt, nc=nc), grid=(nb, nq),
        in_specs=[pl.BlockSpec((N_HEADS, Q_BLOCK, HEAD_DIM), lambda b, i: (0, b * nq + i, 0)),
                  cmp_spec, cmp_spec, kv, kv, kv, kv,
                  pl.BlockSpec((Q_BLOCK, GATE_PAD), lambda b, i: (b * nq + i, 0))],
        out_specs=pl.BlockSpec((Q_BLOCK, D_ATTN), lambda b, i: (b * nq + i, 0)),
        out_shape=jax.ShapeDtypeStruct((m, D_ATTN), F32),
        compiler_params=_params(("parallel", "parallel")), name="nsa_prompt")(q_hm, ck, cv, ks, vs, kw, vw, gates)


def _nsa_sample_kernel(pt_ref, *refs, n_pages, past):
    del pt_ref
    it = iter(refs)
    pools = [[next(it) for _ in range(n_pages)] for _ in range(4)]
    new = [next(it) for _ in range(6)]
    bkw_ref, bvw_ref, q_ref, gate_ref, slope_ref, wk_ref, wv_ref, o_ref = [next(it) for _ in range(8)]
    kbuf, vbuf = next(it), next(it)
    nc = past // CMP_STRIDE
    n_sel = past // SEL_BLOCK + 1
    rows = N_HEADS
    slope = slope_ref[...]
    own = _iota((rows, D_KV), 1) // HEAD_DIM == _iota((rows, D_KV), 0) // GQ
    q = jnp.where(own, jnp.concatenate([q_ref[0]] * N_KV, axis=1), 0.0)
    qb = q.astype(BF16)

    def new_row(i):
        return new[i][0].astype(BF16).astype(F32)

    def own_heads(o):
        o = jnp.where(own, o, 0.0)
        return o[:, 0:64] + o[:, 64:128] + o[:, 128:192] + o[:, 192:256]

    cmp = []
    for pages, w_ref, x_new in ((pools[0], wk_ref, new[0]), (pools[1], wv_ref, new[1])):
        parts = [_pool16(p_ref.at[0], w_ref, PAGE) for p_ref in pages]
        a = jnp.concatenate([p[0] for p in parts], axis=0)
        b = jnp.concatenate([p[1] for p in parts], axis=0)
        cmp.append((a + _shift_up(b, w_ref[CMP_STRIDE:CMP_STRIDE + 1, :] * x_new[0])).astype(BF16))
    ck, cv = cmp
    dist_c = past - (_iota((1, nc), 1) * CMP_STRIDE + (CMP_BLOCK - 1))
    mask_c = dist_c >= 0
    s = jnp.where(mask_c, _dot_nt(qb, ck) - slope * dist_c.astype(F32), NEG)
    p = jnp.where(mask_c, jnp.exp(s - jnp.max(s, axis=-1, keepdims=True)), 0.0)
    p = p / jnp.maximum(jnp.sum(p, axis=-1, keepdims=True), TINY)
    o_c = own_heads(_dot(p.astype(BF16), cv))
    p_grp = jnp.concatenate(
        [jnp.broadcast_to(jnp.sum(p[h * GQ:(h + 1) * GQ], axis=0, keepdims=True), (GQ, nc)) for h in range(N_KV)], axis=0)
    imp = _dot3(p_grp, _overlap_matrix(nc, 128, nc, n_sel))
    sel = _select_blocks(imp, jnp.full((rows, 1), past, jnp.int32), n_sel)

    def attend(pages, k_new, v_new, mask, new_ok, dist, width):
        for i, (kp, vp) in enumerate(pages):
            kbuf[i * kp.shape[0]:(i + 1) * kp.shape[0], :] = kp[...].astype(BF16)
            vbuf[i * vp.shape[0]:(i + 1) * vp.shape[0], :] = vp[...].astype(BF16)
        s = jnp.where(mask, _dot_nt(qb, kbuf[0:width, :]) - slope * dist.astype(F32), NEG)
        s_new = jnp.where(new_ok, jnp.sum(qb.astype(F32) * k_new, axis=-1, keepdims=True), NEG)
        m = jnp.maximum(jnp.max(s, axis=-1, keepdims=True), s_new)
        p = jnp.where(mask, jnp.exp(s - m), 0.0)
        p_new = jnp.where(new_ok, jnp.exp(s_new - m), 0.0)
        l = jnp.sum(p, axis=-1, keepdims=True) + p_new
        o = _dot(p.astype(BF16), vbuf[0:width, :]) + p_new.astype(BF16).astype(F32) * v_new
        return own_heads(o) / jnp.maximum(l, TINY)

    dist_s = past - _iota((1, past), 1)
    expand = _iota((128, past), 0) == _iota((128, past), 1) // SEL_BLOCK
    hit = _dot(sel.astype(BF16), jnp.where(expand, 1.0, 0.0).astype(BF16))
    o_s = attend([(k.at[0], v.at[0]) for k, v in zip(pools[2], pools[3])], new_row(2), new_row(3),
                 (hit > 0.5) & (dist_s >= 0), sel[:, n_sel - 1:n_sel] > 0.5, dist_s, past)
    wl = bkw_ref.shape[1]
    dist_w = wl - _iota((1, wl), 1)
    o_w = attend([(bkw_ref.at[0], bvw_ref.at[0])], new_row(4), new_row(5),
                 (dist_w >= 0) & (dist_w < WINDOW), jnp.full((rows, 1), True), dist_w, wl)
    gates = gate_ref[0]
    o_ref[0] = gates[:, 0:1] * o_c + gates[:, 1:2] * o_s + gates[:, 2:3] * o_w


def _nsa_sample(page_table, pools, new_rows, buf_kw, buf_vw, q, gates_t, slopes, wk, wv):
    nb, n_pages = page_table.shape
    past = n_pages * PAGE
    page_specs = []
    for pool in pools:
        for p in range(n_pages):
            page_specs.append(pl.BlockSpec((1,) + pool.shape[1:], lambda b, pt, p=p: (pt[b, p], 0, 0)))
    per_b = lambda shape: pl.BlockSpec((1,) + shape, lambda b, pt: (b,) + (0,) * len(shape))
    in_specs = (page_specs + [per_b((1, D_KV))] * 6 + [per_b(buf_kw.shape[1:])] * 2
                + [per_b((N_HEADS, HEAD_DIM)), per_b((N_HEADS, 3)),
                   pl.BlockSpec(slopes.shape, lambda b, pt: (0, 0)),
                   pl.BlockSpec(wk.shape, lambda b, pt: (0, 0)), pl.BlockSpec(wv.shape, lambda b, pt: (0, 0))])
    args = [pool for pool in pools for _ in range(n_pages)] + list(new_rows) + [buf_kw, buf_vw, q, gates_t, slopes, wk, wv]
    return pl.pallas_call(
        functools.partial(_nsa_sample_kernel, n_pages=n_pages, past=past),
        grid_spec=pltpu.PrefetchScalarGridSpec(
            num_scalar_prefetch=1, grid=(nb,), in_specs=in_specs,
            out_specs=per_b((N_HEADS, HEAD_DIM)),
            scratch_shapes=[pltpu.VMEM((past, D_KV), BF16), pltpu.VMEM((past, D_KV), BF16)]),
        out_shape=jax.ShapeDtypeStruct((nb, N_HEADS, HEAD_DIM), F32),
        compiler_params=_params(("arbitrary",)), name="nsa_sample")(page_table, *args)


def _s5_disc_kernel(lr_ref, li_ref, ldt_ref, ar_ref, ai_ref, fr_ref, fi_ref):
    lr, li = lr_ref[...], li_ref[...]
    dt = jnp.exp(ldt_ref[...])
    mag = jnp.exp(lr * dt)
    ar = mag * jnp.cos(li * dt)
    ai = mag * jnp.sin(li * dt)
    den = lr * lr + li * li
    ar_ref[...] = ar
    ai_ref[...] = ai
    fr_ref[...] = ((ar - 1.0) * lr + ai * li) / den
    fi_ref[...] = (ai * lr - (ar - 1.0) * li) / den


def _s5_bbar_kernel(fr_ref, fi_ref, br_ref, bi_ref, or_ref, oi_ref):
    fr, fi, br, bi = fr_ref[...], fi_ref[...], br_ref[...], bi_ref[...]
    or_ref[...] = fr * br - fi * bi
    oi_ref[...] = fr * bi + fi * br


def _s5_weights(lam_re, lam_im, log_dt, b_re, b_im, c_re, c_im):
    g, n = lam_re.shape
    sd = jax.ShapeDtypeStruct((g, n), F32)
    ar, ai, fr, fi = pl.pallas_call(_s5_disc_kernel, out_shape=[sd] * 4, name="s5_discretise")(
        lam_re, lam_im, log_dt.reshape(g, 1))
    sb = jax.ShapeDtypeStruct((g * n, SSM_GROUP), F32)
    bbr, bbi = pl.pallas_call(_s5_bbar_kernel, out_shape=[sb] * 2, name="s5_bbar")(
        fr.reshape(g * n, 1), fi.reshape(g * n, 1), b_re.reshape(g * n, SSM_GROUP), b_im.reshape(g * n, SSM_GROUP))
    eye = jnp.eye(g // SSM_BANDS, dtype=F32)
    gl = g // SSM_BANDS

    def band_in(bb):
        x = bb.reshape(SSM_BANDS, gl, n, SSM_GROUP).transpose(0, 1, 3, 2)
        return jnp.einsum("jgpn,gh->jgphn", x, eye).reshape(SSM_BANDS, gl * SSM_GROUP, gl * n).astype(BF16)

    def band_out(c):
        x = c.reshape(SSM_BANDS, gl, SSM_GROUP, n).transpose(0, 1, 3, 2)
        return jnp.einsum("jgnp,gh->jgnhp", x, eye).reshape(SSM_BANDS, gl * n, gl * SSM_GROUP).astype(BF16)

    return (ar.reshape(1, g * n), ai.reshape(1, g * n), band_in(bbr), band_in(bbi), band_out(c_re), band_out(-c_im))


def _s5_prompt_kernel(u_ref, wbr_ref, wbi_ref, ar_ref, ai_ref, wcr_ref, wci_ref, d_ref,
                      y_ref, sr_ref, si_ref, hr_s, hi_s, cr_s, ci_s, *, nb):
    c = pl.program_id(1)
    rows, width = hr_s.shape
    rep = 8 // nb

    @pl.when(c == 0)
    def _():
        cr_s[...] = jnp.zeros_like(cr_s)
        ci_s[...] = jnp.zeros_like(ci_s)

    u = u_ref[...]
    ub = u.astype(BF16)
    hr_s[...] = _dot(ub, wbr_ref[0])
    hi_s[...] = _dot(ub, wbi_ref[0])
    ar = jnp.broadcast_to(ar_ref[...], (8, width))
    ai = jnp.broadcast_to(ai_ref[...], (8, width))
    sub = _iota((8, width), 0) // nb

    def step(i, carry):
        sr, si = carry
        base = pl.multiple_of(i * 8, 8)
        xr = hr_s[pl.ds(base, 8), :]
        xi = hi_s[pl.ds(base, 8), :]
        outr, outi = xr, xi
        for k in range(rep):
            yr = ar * sr - ai * si + xr
            yi = ar * si + ai * sr + xi
            keep = sub == k
            outr = jnp.where(keep, yr, outr)
            outi = jnp.where(keep, yi, outi)
            zr = jnp.where(keep, yr, 0.0)
            zi = jnp.where(keep, yi, 0.0)
            sr, si = zr, zi
            for sh in range(1, rep):
                sr = sr + pltpu.roll(zr, sh * nb, 0)
                si = si + pltpu.roll(zi, sh * nb, 0)
        hr_s[pl.ds(base, 8), :] = outr
        hi_s[pl.ds(base, 8), :] = outi
        return sr, si

    sr, si = lax.fori_loop(0, rows // 8, step, (cr_s[...], ci_s[...]))
    cr_s[...] = sr
    ci_s[...] = si
    y = _dot(hr_s[...].astype(BF16), wcr_ref[0]) + _dot(hi_s[...].astype(BF16), wci_ref[0]) + d_ref[...] * u
    y_ref[...] = jax.nn.gelu(y)

    @pl.when(c == pl.num_programs(1) - 1)
    def _():
        sr_ref[...] = sr[0:nb]
        si_ref[...] = si[0:nb]


def _s5_prompt(u_tm, weights, d_skip, nb, tc):
    ar, ai, wbr, wbi, wcr, wci = weights
    rows = u_tm.shape[0]
    cw = D_SSM // SSM_BANDS
    sw = ar.shape[1] // SSM_BANDS
    blk = tc * nb
    tile = pl.BlockSpec((blk, cw), lambda j, c: (c, j))
    band = lambda a: pl.BlockSpec((1,) + a.shape[1:], lambda j, c: (j, 0, 0))
    vec = lambda w: pl.BlockSpec((1, w), lambda j, c: (0, j))
    state = pl.BlockSpec((nb, sw), lambda j, c: (0, j))
    return pl.pallas_call(
        functools.partial(_s5_prompt_kernel, nb=nb), grid=(SSM_BANDS, rows // blk),
        in_specs=[tile, band(wbr), band(wbi), vec(sw), vec(sw), band(wcr), band(wci), vec(cw)],
        out_specs=[tile, state, state],
        out_shape=[jax.ShapeDtypeStruct((rows, D_SSM), F32)] + [jax.ShapeDtypeStruct((nb, ar.shape[1]), F32)] * 2,
        scratch_shapes=[pltpu.VMEM((blk, sw), F32)] * 2 + [pltpu.VMEM((8, sw), F32)] * 2,
        compiler_params=_params(("parallel", "arbitrary")), name="s5_prompt")(
            u_tm, wbr, wbi, ar, ai, wcr, wci, d_skip)


def _s5_sample_kernel(u_ref, h0r_ref, h0i_ref, wbr_ref, wbi_ref, ar_ref, ai_ref, wcr_ref, wci_ref, d_ref,
                      y_ref, sr_ref, si_ref):
    cw = D_SSM // SSM_BANDS
    sw = ar_ref.shape[1] // SSM_BANDS
    for j in range(SSM_BANDS):
        cs = slice(j * cw, (j + 1) * cw)
        ss = slice(j * sw, (j + 1) * sw)
        u = u_ref[:, cs]
        ub = u.astype(BF16)
        ar, ai = ar_ref[:, ss], ai_ref[:, ss]
        h0r, h0i = h0r_ref[:, ss], h0i_ref[:, ss]
        hr = _dot(ub, wbr_ref[j]) + (ar * h0r - ai * h0i)
        hi = _dot(ub, wbi_ref[j]) + (ar * h0i + ai * h0r)
        sr_ref[:, ss] = hr
        si_ref[:, ss] = hi
        y = _dot(hr.astype(BF16), wcr_ref[j]) + _dot(hi.astype(BF16), wci_ref[j]) + d_ref[:, cs] * u
        y_ref[:, cs] = jax.nn.gelu(y)


def _s5_sample(u, h0r, h0i, weights, d_skip):
    ar, ai, wbr, wbi, wcr, wci = weights
    nb = u.shape[0]
    st = jax.ShapeDtypeStruct(h0r.shape, F32)
    return pl.pallas_call(
        _s5_sample_kernel, out_shape=[jax.ShapeDtypeStruct((nb, D_SSM), F32), st, st],
        compiler_params=pltpu.CompilerParams(vmem_limit_bytes=VMEM_LIMIT), name="s5_sample")(
            u, h0r, h0i, wbr, wbi, ar, ai, wcr, wci, d_skip)


def _xattn_prompt_kernel(q_ref, k_ref, v_ref, o_ref):
    scale = X_HEAD_DIM ** -0.5
    for h in range(X_HEADS):
        cs = slice(h * X_HEAD_DIM, (h + 1) * X_HEAD_DIM)
        s = _dot_nt(q_ref[:, cs], k_ref[:, cs].astype(BF16)) * scale
        p = jnp.exp(s - jnp.max(s, axis=-1, keepdims=True))
        p = p / jnp.sum(p, axis=-1, keepdims=True)
        o_ref[:, cs] = _dot(p.astype(BF16), v_ref[:, cs].astype(BF16)).astype(o_ref.dtype)


def _xattn_prompt(q, mk, mv, nb, tm):
    m = q.shape[0]
    nt = m // nb // tm
    mem = pl.BlockSpec((MEM_LEN, D_MODEL), lambda b, i: (b, 0))
    row = pl.BlockSpec((tm, D_MODEL), lambda b, i: (b * nt + i, 0))
    return pl.pallas_call(
        _xattn_prompt_kernel, grid=(nb, nt), in_specs=[row, mem, mem], out_specs=row,
        out_shape=jax.ShapeDtypeStruct((m, D_MODEL), BF16),
        compiler_params=_params(("parallel", "parallel")), name="xattn_prompt")(q, mk, mv)


def _xattn_sample_kernel(q_ref, k_ref, v_ref, o_ref):
    scale = X_HEAD_DIM ** -0.5
    own = _iota((8, D_MODEL), 1) // X_HEAD_DIM == _iota((8, D_MODEL), 0)
    q = jnp.where(own, jnp.broadcast_to(q_ref[0], (8, D_MODEL)), 0.0).astype(BF16)
    s = _dot_nt(q, k_ref[0].astype(BF16)) * scale
    p = jnp.exp(s - jnp.max(s, axis=-1, keepdims=True))
    p = p / jnp.sum(p, axis=-1, keepdims=True)
    o = _dot(p.astype(BF16), v_ref[0].astype(BF16))
    o_ref[0] = jnp.sum(jnp.where(own, o, 0.0), axis=0, keepdims=True).astype(o_ref.dtype)


def _xattn_sample(q, mk, mv):
    nb = q.shape[0]
    mem = pl.BlockSpec((1, MEM_LEN, D_MODEL), lambda b: (b, 0, 0))
    row = pl.BlockSpec((1, 1, D_MODEL), lambda b: (b, 0, 0))
    return pl.pallas_call(
        _xattn_sample_kernel, grid=(nb,), in_specs=[row, mem, mem], out_specs=row,
        out_shape=jax.ShapeDtypeStruct((nb, 1, D_MODEL), BF16),
        compiler_params=_params(("parallel",)), name="xattn_sample")(q, mk, mv)


def _top_distinct(x, n):
    rows = []
    for _ in range(n):
        m = jnp.max(x, axis=0, keepdims=True)
        rows.append(m)
        x = jnp.where(x == m, LOWEST, x)
    return jnp.concatenate(rows, axis=0)


def _peer_route_kernel(x_ref, g_ref, w_ref, k1_ref, k2_ref, xt_ref, s1_ref, s2_ref, e2_ref, c1_ref, tau_ref):
    xn = _rms(x_ref[...], g_ref[...])
    xt_ref[...] = xn.T.astype(BF16)
    q = _dot(xn.astype(BF16), w_ref[...]).astype(BF16)
    half = PEER_KEYS
    for h in range(PEER_HEADS):
        s1 = _dot_nt(k1_ref[...], q[:, 2 * half * h:2 * half * h + half])
        s2 = _dot_nt(k2_ref[...], q[:, 2 * half * h + half:2 * half * (h + 1)])
        d1 = _top_distinct(s1, PEER_TOPK)
        d2 = _top_distinct(s2, PEER_TOPK)
        cand = jnp.concatenate([d1[a:a + 1] + d2 for a in range(PEER_TOPK)], axis=0)
        tau = _top_distinct(cand, PEER_TOPK)[PEER_TOPK - 1:PEER_TOPK]
        top = d1[0:1] + d2[0:1]
        z = jnp.sum(jnp.where(cand >= tau, jnp.exp(cand - top), 0.0), axis=0, keepdims=True)
        s1_ref[h] = s1
        s2_ref[h] = s2
        e2_ref[h] = jnp.exp(s2 - d2[0:1])
        c1_ref[h] = jnp.exp(s1 - d1[0:1]) / z
        tau_ref[h] = tau


def _peer_route(x, g, w_pq, k1, k2, tm):
    m = x.shape[0]
    hk = pl.BlockSpec((PEER_HEADS, PEER_KEYS, tm), lambda i: (0, 0, i))
    hks = jax.ShapeDtypeStruct((PEER_HEADS, PEER_KEYS, m), F32)
    return pl.pallas_call(
        _peer_route_kernel, grid=(m // tm,),
        in_specs=[pl.BlockSpec((tm, D_MODEL), lambda i: (i, 0)), _full(g), _full(w_pq), _full(k1), _full(k2)],
        out_specs=[pl.BlockSpec((D_MODEL, tm), lambda i: (0, i)), hk, hk, hk, hk,
                   pl.BlockSpec((PEER_HEADS, 1, tm), lambda i: (0, 0, i))],
        out_shape=[jax.ShapeDtypeStruct((D_MODEL, m), BF16), hks, hks, hks, hks,
                   jax.ShapeDtypeStruct((PEER_HEADS, 1, m), F32)],
        compiler_params=_params(("parallel",)), name="peer_route")(x, g, w_pq, k1, k2)


def _peer_kernel(xt_ref, u_ref, vt_ref, s1_ref, s2_ref, e2_ref, c1_ref, tau_ref, o_ref, act_s, wt_s):
    e = pl.program_id(1)
    et, tt = act_s.shape

    @pl.when(e == 0)
    def _():
        o_ref[...] = jnp.zeros_like(o_ref)

    act_s[...] = _dot(u_ref[...], xt_ref[...])
    for lt in range(tt // 128):
        ls = slice(lt * 128, (lt + 1) * 128)
        for jj in range(et // PEER_KEYS):
            rs = slice(jj * PEER_KEYS, (jj + 1) * PEER_KEYS)

            def head(h, gate):
                mask = (s1_ref[h, jj:jj + 1, ls] + s2_ref[h, :, ls]) >= tau_ref[h, :, ls]
                return gate + jnp.where(mask, e2_ref[h, :, ls], 0.0) * c1_ref[h, jj:jj + 1, ls]

            gate = lax.fori_loop(0, PEER_HEADS, head, jnp.zeros((PEER_KEYS, 128), F32))
            wt_s[rs, ls] = (gate * jax.nn.gelu(act_s[rs, ls])).astype(BF16)
    o_ref[...] += _dot(vt_ref[...], wt_s[...])


def _peer(xt, u, vt, s1, s2, e2, c1, tau, tt, et):
    m = xt.shape[1]
    n_exp = u.shape[0]
    tok3 = lambda r: pl.BlockSpec((PEER_HEADS, r, tt), lambda i, e: (0, 0, i))
    tile3 = pl.BlockSpec((PEER_HEADS, et // PEER_KEYS, tt), lambda i, e: (0, e, i))
    return pl.pallas_call(
        _peer_kernel, grid=(m // tt, n_exp // et),
        in_specs=[pl.BlockSpec((D_MODEL, tt), lambda i, e: (0, i)),
                  pl.BlockSpec((et, D_MODEL), lambda i, e: (e, 0)),
                  pl.BlockSpec((D_MODEL, et), lambda i, e: (0, e)),
                  tile3, tok3(PEER_KEYS), tok3(PEER_KEYS), tile3, tok3(1)],
        out_specs=pl.BlockSpec((D_MODEL, tt), lambda i, e: (0, i)),
        out_shape=jax.ShapeDtypeStruct((D_MODEL, m), F32),
        scratch_shapes=[pltpu.VMEM((et, tt), F32), pltpu.VMEM((et, tt), BF16)],
        compiler_params=_params(("parallel", "arbitrary")), name="peer_dense")(
            xt, u, vt, s1, s2, e2, c1, tau)


def _final_kernel(h_ref, ot_ref, g_ref, y_ref):
    y_ref[...] = _rms(h_ref[...] + ot_ref[...].T, g_ref[...])


def _final_norm(h, out_t, g, tm):
    m = h.shape[0]
    row = pl.BlockSpec((tm, D_MODEL), lambda i: (i, 0))
    return pl.pallas_call(
        _final_kernel, grid=(m // tm,),
        in_specs=[row, pl.BlockSpec((D_MODEL, tm), lambda i: (0, i)), _full(g)], out_specs=row,
        out_shape=jax.ShapeDtypeStruct((m, D_MODEL), F32),
        compiler_params=_params(("parallel",)), name="final_norm")(h, out_t, g)


def _row(v):
    return v.reshape(1, -1).astype(F32)


def _cmp_rows(w):
    return jnp.repeat(w.T.astype(F32), HEAD_DIM, axis=1)


def kernel(x_prompt, x_sample, mem_prompt, cache_k_cmp, cache_v_cmp, cache_k_sel, cache_v_sel, cache_k_win, cache_v_win, state_s5_re, state_s5_im, cache_mem_k, cache_mem_v, page_table, g_mix, w_in, w_cmp_k, w_cmp_v, lam_re, lam_im, log_dt, b_re, b_im, c_re, c_im, d_skip, w_glu, g_attn_out, g_ssm_out, w_out, g_x, g_mem, w_xq, w_xk, w_xv, w_xo, g_ffn, w_pq, peer_k1, peer_k2, peer_u, peer_v, g_final):
    nb, seq, _ = x_prompt.shape
    db = x_sample.shape[0]
    depth = g_mix.shape[0]
    assert depth == 1 and x_sample.shape[1] == 1
    l = 0
    mp = nb * seq
    n_gate = 3 * N_HEADS
    kv_end = D_ATTN + 6 * D_KV

    w_attn = jnp.concatenate([w_in[l][:, :kv_end + n_gate], jnp.zeros((D_MODEL, GATE_PAD - n_gate), F32)], axis=1).astype(BF16)
    w_u = w_in[l][:, kv_end + n_gate:].astype(BF16)
    w_full = jnp.concatenate([w_attn, w_u], axis=1)
    w_glu_b = w_glu[l].astype(BF16)
    w_out_a = w_out[l][:D_ATTN].astype(BF16)
    w_out_s = w_out[l][D_ATTN:].astype(BF16)
    w_mem = jnp.concatenate([w_xk[l], w_xv[l]], axis=1).astype(BF16)
    w_xq_b = w_xq[l].astype(BF16)
    w_xo_b = w_xo[l].astype(BF16)
    w_pq_b = w_pq[l].astype(BF16)
    k1_b = peer_k1[l].astype(BF16)
    k2_b = peer_k2[l].astype(BF16)
    u_b = peer_u[l].astype(BF16)
    vt_b = peer_v[l].T.astype(BF16)
    wk_rows = _cmp_rows(w_cmp_k[l])
    wv_rows = _cmp_rows(w_cmp_v[l])
    gm, gx, gf, gfin = _row(g_mix[l]), _row(g_x[l]), _row(g_ffn[l]), _row(g_final)
    ga, gs, gme = _row(g_attn_out[l]), _row(g_ssm_out[l]), _row(g_mem[l])
    s5w = _s5_weights(lam_re[l], lam_im[l], log_dt[l], b_re[l], b_im[l], c_re[l], c_im[l])
    dsk = _row(d_skip[l])

    xp = x_prompt.reshape(mp, D_MODEL)
    q_hm, kv, kv_hm, gates = _proj_attn(xp, gm, w_attn, tm=256)
    u_tm = _proj_u(xp, gm, w_u, nb, tm=256).reshape(seq * nb, D_SSM)
    ck, cv = _compress(kv[0], kv[1], wk_rows, wv_rows, nb)
    o_a = _nsa_prompt(q_hm, ck, cv, kv_hm[2], kv_hm[3], kv_hm[4], kv_hm[5], gates, nb)
    y_tm, p_sr, p_si = _s5_prompt(u_tm, s5w, dsk, nb, tc=256)
    o_s = _glu(y_tm.reshape(seq, nb * D_SSM), w_glu_b, nb, tm=256)
    h1 = _merge(o_a, o_s, ga, gs, w_out_a, w_out_s, xp, tm=256)
    mk, mv = _mem_kv(mem_prompt.reshape(nb * MEM_LEN, D_MODEL), gme, w_mem, tm=256)
    xq = _mm(h1, w_xq_b, 256, BF16, g=gx, name="xattn_q")
    xo = _xattn_prompt(xq, mk, mv, nb, tm=256)
    h2 = _mm(xo, w_xo_b, 256, F32, res=h1, name="xattn_o")
    routed = _peer_route(h2, gf, w_pq_b, k1_b, k2_b, tm=256)
    y_p = _final_norm(h2, _peer(routed[0], u_b, vt_b, *routed[1:], tt=512, et=1024), gfin, tm=256)

    xs = x_sample.reshape(db, D_MODEL)
    z = _mm(xs, w_full, db, F32, g=gm, gate_cols=(kv_end, kv_end + n_gate), name="proj_sample")
    new_rows = [z[:, D_ATTN + D_KV * i:D_ATTN + D_KV * (i + 1)] for i in range(6)]
    n_phys = cache_k_cmp.shape[1]
    pools = ([c[l].reshape(n_phys, 2 * PAGE, D_KV // 2) for c in (cache_k_cmp, cache_v_cmp)]
             + [c[l].reshape(n_phys, PAGE, D_KV) for c in (cache_k_sel, cache_v_sel)])
    wl = cache_k_win.shape[2]
    buf_kw = cache_k_win[l].reshape(db, wl, D_KV)
    buf_vw = cache_v_win[l].reshape(db, wl, D_KV)
    q_s = (z[:, :D_ATTN] * (HEAD_DIM ** -0.5)).reshape(db, N_HEADS, HEAD_DIM)
    gates_t = z[:, kv_end:kv_end + n_gate].reshape(db, 3, N_HEADS).transpose(0, 2, 1)
    slopes = jnp.asarray(np.array([[_alibi_slope(i)] for i in range(N_HEADS)], np.float32))
    o_a_s = _nsa_sample(page_table, pools, [r.reshape(db, 1, D_KV) for r in new_rows], buf_kw, buf_vw,
                        q_s, gates_t, slopes, wk_rows, wv_rows).reshape(db, D_ATTN)
    y_s, s_sr, s_si = _s5_sample(z[:, kv_end + GATE_PAD:], state_s5_re[l].reshape(db, -1), state_s5_im[l].reshape(db, -1), s5w, dsk)
    o_s_s = _glu(y_s, w_glu_b, 1, tm=db)
    h1s = _merge(o_a_s, o_s_s, ga, gs, w_out_a, w_out_s, xs, tm=db)
    xq_s = _mm(h1s, w_xq_b, db, F32, g=gx, name="xattn_q_sample")
    xo_s = _xattn_sample(xq_s.reshape(db, 1, D_MODEL), cache_mem_k[l].reshape(db, MEM_LEN, D_MODEL),
                         cache_mem_v[l].reshape(db, MEM_LEN, D_MODEL)).reshape(db, D_MODEL)
    h2s = _mm(xo_s, w_xo_b, db, F32, res=h1s, name="xattn_o_sample")
    routed_s = _peer_route(h2s, gf, w_pq_b, k1_b, k2_b, tm=db)
    y_s_out = _final_norm(h2s, _peer(routed_s[0], u_b, vt_b, *routed_s[1:], tt=db, et=1024), gfin, tm=db)

    kvshape = (1, nb, seq, N_KV, HEAD_DIM)
    wlp = min(WINDOW, seq)
    p_kv = [a.reshape(kvshape) for a in kv]
    p_win = [a[:, :, seq - wlp:] for a in p_kv[4:6]]
    s_new = [r.reshape(1, db, 1, N_KV, HEAD_DIM) for r in new_rows]
    s_win = [jnp.concatenate([c[l], n[0]], axis=1)[None, :, -min(WINDOW, wl + 1):] for c, n in ((cache_k_win, s_new[4]), (cache_v_win, s_new[5]))]
    g64 = (1, -1, N_SSM_GROUPS, SSM_STATE)
    return (y_p.reshape(nb, seq, D_MODEL), y_s_out.reshape(db, 1, D_MODEL),
            p_kv[0], p_kv[1], p_kv[2], p_kv[3], p_win[0], p_win[1],
            p_sr.reshape(g64), p_si.reshape(g64),
            mk.reshape(1, nb, MEM_LEN, X_HEADS, X_HEAD_DIM), mv.reshape(1, nb, MEM_LEN, X_HEADS, X_HEAD_DIM),
            s_new[0], s_new[1], s_new[2], s_new[3], s_win[0], s_win[1],
            s_sr.reshape(g64), s_si.reshape(g64))
```

```python
import functools
import math

import jax
import jax.numpy as jnp
import numpy as np
from jax import lax
from jax.experimental import pallas as pl
from jax.experimental.pallas import tpu as pltpu

F32 = jnp.float32
BF16 = jnp.bfloat16

D_MODEL = 2048
N_HEADS = 16
HEAD_DIM = 64
N_KV = 4
GQ = 4
D_ATTN = 1024
D_SSM = 1024
D_KV = 256
CMP_BLOCK = 32
CMP_STRIDE = 16
SEL_BLOCK = 64
N_SEL = 8
WINDOW = 512
Q_BLOCK = 128
FORCE = 1e4
N_SSM_GROUPS = 64
SSM_GROUP = 16
SSM_STATE = 64
SSM_BANDS = 4
MEM_LEN = 256
X_HEADS = 4
X_HEAD_DIM = 512
PEER_KEYS = 128
PEER_HEADS = 8
PEER_TOPK = 16
PAGE = 128
EPS = 1e-6
NEG = -1e30
TINY = 1e-30
LOWEST = -3.0e38
HUGE = 3.0e38
GATE_PAD = 128
VMEM_LIMIT = 56 * 2**20


def _params(sem):
    return pltpu.CompilerParams(dimension_semantics=sem, vmem_limit_bytes=VMEM_LIMIT)


def _full(a):
    nd = a.ndim
    return pl.BlockSpec(a.shape, lambda *_: (0,) * nd)


def _rms(x, g):
    return x * lax.rsqrt(jnp.mean(x * x, axis=-1, keepdims=True) + EPS) * g


def _dot(a, b):
    return jnp.dot(a, b, preferred_element_type=F32)


def _dot_nt(a, b):
    return lax.dot_general(a, b, (((1,), (1,)), ((), ())), preferred_element_type=F32)


def _dot3(a, b_exact):
    hi = a.astype(BF16)
    r1 = a - hi.astype(F32)
    mid = r1.astype(BF16)
    lo = (r1 - mid.astype(F32)).astype(BF16)
    return _dot(hi, b_exact) + _dot(mid, b_exact) + _dot(lo, b_exact)


def _iota(shape, dim):
    return lax.broadcasted_iota(jnp.int32, shape, dim)


def _proj_attn_kernel(x_ref, g_ref, w_ref, q_ref, *rest):
    kv_refs, kvh_refs, gate_ref = rest[:6], rest[6:12], rest[12]
    xn = _rms(x_ref[...], g_ref[...]).astype(BF16)
    z = _dot(xn, w_ref[...])
    for hd in range(N_HEADS):
        q_ref[hd] = (z[:, hd * HEAD_DIM:(hd + 1) * HEAD_DIM] * (HEAD_DIM ** -0.5)).astype(BF16)
    for k in range(6):
        zk = z[:, D_ATTN + D_KV * k:D_ATTN + D_KV * (k + 1)]
        kv_refs[k][...] = zk
        for h in range(N_KV):
            kvh_refs[k][h] = zk[:, h * HEAD_DIM:(h + 1) * HEAD_DIM].astype(BF16)
    gate_ref[...] = jax.nn.sigmoid(z[:, D_ATTN + 6 * D_KV:])


def _proj_attn(x, g, w, tm):
    m = x.shape[0]
    row = lambda n: pl.BlockSpec((tm, n), lambda i: (i, 0))
    hm = lambda n: pl.BlockSpec((n, tm, HEAD_DIM), lambda i: (0, i, 0))
    out_shape = ([jax.ShapeDtypeStruct((N_HEADS, m, HEAD_DIM), BF16)]
                 + [jax.ShapeDtypeStruct((m, D_KV), F32)] * 6
                 + [jax.ShapeDtypeStruct((N_KV, m, HEAD_DIM), BF16)] * 6
                 + [jax.ShapeDtypeStruct((m, GATE_PAD), F32)])
    out_specs = [hm(N_HEADS)] + [row(D_KV)] * 6 + [hm(N_KV)] * 6 + [row(GATE_PAD)]
    outs = pl.pallas_call(
        _proj_attn_kernel, grid=(m // tm,),
        in_specs=[row(D_MODEL), _full(g), _full(w)],
        out_specs=out_specs, out_shape=out_shape,
        compiler_params=_params(("parallel",)), name="proj_attn")(x, g, w)
    return outs[0], outs[1:7], outs[7:13], outs[13]


def _proj_u_kernel(x_ref, g_ref, w_ref, u_ref):
    xn = _rms(x_ref[...], g_ref[...]).astype(BF16)
    u_ref[...] = _dot(xn, w_ref[...])


def _proj_u(x, g, w, nb, tm):
    m = x.shape[0]
    t = m // nb
    nt = t // tm
    return pl.pallas_call(
        _proj_u_kernel, grid=(nb, nt),
        in_specs=[pl.BlockSpec((tm, D_MODEL), lambda b, i: (b * nt + i, 0)), _full(g), _full(w)],
        out_specs=pl.BlockSpec((tm, D_SSM), lambda b, i: (i, b)),
        out_shape=jax.ShapeDtypeStruct((t, nb * D_SSM), F32),
        compiler_params=_params(("parallel", "parallel")), name="proj_u")(x, g, w)


def _mm_kernel(*refs, norm, res, gate_cols):
    it = iter(refs)
    x_ref = next(it)
    g_ref = next(it) if norm else None
    w_ref = next(it)
    r_ref = next(it) if res else None
    o_ref = next(it)
    x = x_ref[...]
    if norm:
        x = _rms(x.astype(F32), g_ref[...])
    z = _dot(x.astype(BF16), w_ref[...])
    if res:
        z = z + r_ref[...]
    if gate_cols is not None:
        col = _iota(z.shape, 1)
        z = jnp.where((col >= gate_cols[0]) & (col < gate_cols[1]), jax.nn.sigmoid(z), z)
    o_ref[...] = z.astype(o_ref.dtype)


def _mm(x, w, tm, out_dtype, g=None, res=None, gate_cols=None, name="mm"):
    m, k = x.shape
    n = w.shape[1]
    row = lambda c: pl.BlockSpec((tm, c), lambda i: (i, 0))
    args, specs = [x], [row(k)]
    if g is not None:
        args.append(g)
        specs.append(_full(g))
    args.append(w)
    specs.append(_full(w))
    if res is not None:
        args.append(res)
        specs.append(row(n))
    return pl.pallas_call(
        functools.partial(_mm_kernel, norm=g is not None, res=res is not None, gate_cols=gate_cols),
        grid=(m // tm,), in_specs=specs, out_specs=row(n),
        out_shape=jax.ShapeDtypeStruct((m, n), out_dtype),
        compiler_params=_params(("parallel",)), name=name)(*args)


def _mm2_kernel(x_ref, g_ref, w_ref, o0_ref, o1_ref):
    xn = _rms(x_ref[...], g_ref[...]).astype(BF16)
    z = _dot(xn, w_ref[...])
    n = o0_ref.shape[1]
    o0_ref[...] = z[:, :n]
    o1_ref[...] = z[:, n:]


def _mem_kv(mem, g, w, tm):
    m = mem.shape[0]
    row = pl.BlockSpec((tm, D_MODEL), lambda i: (i, 0))
    return pl.pallas_call(
        _mm2_kernel, grid=(m // tm,), in_specs=[row, _full(g), _full(w)], out_specs=[row, row],
        out_shape=[jax.ShapeDtypeStruct((m, D_MODEL), F32)] * 2,
        compiler_params=_params(("parallel",)), name="mem_kv")(mem, g, w)


def _glu_kernel(y_ref, w_ref, o_ref):
    y = y_ref[...]
    o_ref[...] = y * jax.nn.sigmoid(_dot(y.astype(BF16), w_ref[...]))


def _glu(y_tm, w, nb, tm):
    t = y_tm.shape[0]
    nt = t // tm
    return pl.pallas_call(
        _glu_kernel, grid=(nb, nt),
        in_specs=[pl.BlockSpec((tm, D_SSM), lambda b, i: (i, b)), _full(w)],
        out_specs=pl.BlockSpec((tm, D_SSM), lambda b, i: (b * nt + i, 0)),
        out_shape=jax.ShapeDtypeStruct((nb * t, D_SSM), F32),
        compiler_params=_params(("parallel", "parallel")), name="glu")(y_tm, w)


def _merge_kernel(oa_ref, os_ref, ga_ref, gs_ref, wa_ref, ws_ref, x_ref, o_ref):
    a = _rms(oa_ref[...], ga_ref[...]).astype(BF16)
    s = _rms(os_ref[...], gs_ref[...]).astype(BF16)
    o_ref[...] = x_ref[...] + (_dot(a, wa_ref[...]) + _dot(s, ws_ref[...]))


def _merge(o_a, o_s, g_a, g_s, w_a, w_s, x, tm):
    m = x.shape[0]
    half = pl.BlockSpec((tm, D_ATTN), lambda i: (i, 0))
    row = pl.BlockSpec((tm, D_MODEL), lambda i: (i, 0))
    return pl.pallas_call(
        _merge_kernel, grid=(m // tm,),
        in_specs=[half, half, _full(g_a), _full(g_s), _full(w_a), _full(w_s), row],
        out_specs=row, out_shape=jax.ShapeDtypeStruct((m, D_MODEL), F32),
        compiler_params=_params(("parallel",)), name="merge_heads")(o_a, o_s, g_a, g_s, w_a, w_s, x)


def _pool16(x_refs, w_ref, tokens):
    halves = []
    for hf, x_ref in enumerate(x_refs):
        lanes = slice(hf * 128, (hf + 1) * 128)
        a = b = None
        for j in range(CMP_STRIDE):
            xj = x_ref[pl.ds(j, tokens // CMP_STRIDE, stride=CMP_STRIDE), :]
            ta = xj * w_ref[j:j + 1, lanes]
            tb = xj * w_ref[CMP_STRIDE + j:CMP_STRIDE + j + 1, lanes]
            a = ta if a is None else a + ta
            b = tb if b is None else b + tb
        halves.append((a, b))
    return (jnp.concatenate([halves[0][0], halves[1][0]], axis=1),
            jnp.concatenate([halves[0][1], halves[1][1]], axis=1))


def _shift_up(b, last_row):
    n = b.shape[0]
    rolled = pltpu.roll(b, n - 1, 0)
    return jnp.where(_iota(b.shape, 0) == n - 1, last_row, rolled)


def _compress_kernel(k0_ref, k1_ref, v0_ref, v1_ref, wk_ref, wv_ref, ck_ref, cv_ref):
    tokens = k0_ref.shape[0]
    for x_refs, w_ref, o_ref in (((k0_ref, k1_ref), wk_ref, ck_ref), ((v0_ref, v1_ref), wv_ref, cv_ref)):
        a, b = _pool16(x_refs, w_ref, tokens)
        c = (a + _shift_up(b, 0.0)).astype(BF16)
        for h in range(N_KV):
            o_ref[0, h] = c[:, h * HEAD_DIM:(h + 1) * HEAD_DIM]


def _compress(kc, vc, wk, wv, nb):
    s = kc.shape[0] // nb
    nc = s // CMP_STRIDE
    lo = pl.BlockSpec((s, D_KV // 2), lambda b: (b, 0))
    hi = pl.BlockSpec((s, D_KV // 2), lambda b: (b, 1))
    out = pl.BlockSpec((1, N_KV, nc, HEAD_DIM), lambda b: (b, 0, 0, 0))
    return pl.pallas_call(
        _compress_kernel, grid=(nb,), in_specs=[lo, hi, lo, hi, _full(wk), _full(wv)], out_specs=[out, out],
        out_shape=[jax.ShapeDtypeStruct((nb, N_KV, nc, HEAD_DIM), BF16)] * 2,
        compiler_params=_params(("parallel",)), name="nsa_compress")(kc, kc, vc, vc, wk, wv)


def _overlap_matrix(nc, width, n_cmp, n_sel):
    i = _iota((nc, width), 0)
    j = _iota((nc, width), 1)
    lo = jnp.maximum(i * CMP_STRIDE, j * SEL_BLOCK)
    hi = jnp.minimum(i * CMP_STRIDE + CMP_BLOCK, (j + 1) * SEL_BLOCK)
    ov = jnp.maximum(hi - lo, 0).astype(F32) * (1.0 / CMP_BLOCK)
    return jnp.where((i < n_cmp) & (j < n_sel), ov, 0.0).astype(BF16)


def _select_blocks(imp, qpos, n_sel):
    blk = _iota(imp.shape, 1)
    valid = blk * SEL_BLOCK <= qpos
    forced = (blk == qpos // SEL_BLOCK) | (blk == 0)
    x = jnp.where(valid, imp + jnp.where(forced, FORCE, 0.0), NEG)
    x = jnp.where(blk < n_sel, x, LOWEST)
    sel = jnp.zeros(imp.shape, F32)
    for _ in range(N_SEL):
        m = jnp.max(x, axis=-1, keepdims=True)
        first = jnp.min(jnp.where(x == m, blk, 4 * imp.shape[1]), axis=-1, keepdims=True)
        pick = blk == first
        sel = jnp.where(pick & (m > 0.5 * NEG), 1.0, sel)
        x = jnp.where(pick, LOWEST, x)
    return sel


def _softmax_step(carry, s, mask, v):
    m, l, acc = carry
    s = jnp.where(mask, s, NEG)
    m_new = jnp.maximum(m, jnp.max(s, axis=-1, keepdims=True))
    p = jnp.where(mask, jnp.exp(s - m_new), 0.0)
    alpha = jnp.exp(m - m_new)
    l = alpha * l + jnp.sum(p, axis=-1, keepdims=True)
    acc = alpha * acc + _dot(p.astype(BF16), v)
    return m_new, l, acc


def _softmax_init(rows, width):
    return jnp.full((rows, 1), NEG, F32), jnp.zeros((rows, 1), F32), jnp.zeros((rows, width), F32)


def _alibi_slope(head):
    return float(2.0 ** (-8.0 * (head + 1) / N_HEADS))


def _nsa_prompt_kernel(q_ref, ck_ref, cv_ref, ks_ref, vs_ref, kw_ref, vw_ref, gate_ref, o_ref, *, kt, nc):
    qi = pl.program_id(1)
    rows = GQ * Q_BLOCK
    q0 = qi * Q_BLOCK
    r = _iota((rows, 1), 0)
    qpos = q0 + (r & (Q_BLOCK - 1))
    qpos_tok = q0 + _iota((Q_BLOCK, 1), 0)
    n_sel = ks_ref.shape[1] // SEL_BLOCK
    ov = _overlap_matrix(nc, 128, nc - 1, n_sel)
    cend = _iota((1, nc), 1) * CMP_STRIDE + (CMP_BLOCK - 1)
    n_hi = (q0 + Q_BLOCK + kt - 1) // kt
    w_lo = jnp.maximum(q0 - (WINDOW - 1), 0) // kt
    gates = gate_ref[...]
    for h in range(N_KV):
        slope = jnp.zeros((rows, 1), F32)
        for g in range(GQ):
            slope = jnp.where(r // Q_BLOCK == g, _alibi_slope(h * GQ + g), slope)
        q = q_ref[h * GQ:(h + 1) * GQ].reshape(rows, HEAD_DIM)
        dist_c = qpos - cend
        mask_c = dist_c >= 0
        s = _dot_nt(q, ck_ref[0, h]) - slope * dist_c.astype(F32)
        s = jnp.where(mask_c, s, NEG)
        p = jnp.where(mask_c, jnp.exp(s - jnp.max(s, axis=-1, keepdims=True)), 0.0)
        p = p / jnp.maximum(jnp.sum(p, axis=-1, keepdims=True), TINY)
        o_c = _dot(p.astype(BF16), cv_ref[0, h])
        p_grp = p[0:Q_BLOCK]
        for g in range(1, GQ):
            p_grp = p_grp + p[g * Q_BLOCK:(g + 1) * Q_BLOCK]
        sel = _select_blocks(_dot3(p_grp, ov), qpos_tok, n_sel).astype(BF16)

        def tile(t, carry, k_ref, v_ref, selected):
            k0 = pl.multiple_of(t * kt, kt)
            kpos = k0 + _iota((1, kt), 1)
            dist = qpos - kpos
            s = _dot_nt(q, k_ref[h, pl.ds(k0, kt), :]) - slope * dist.astype(F32)
            if selected:
                expand = (_iota((128, kt), 0) == (k0 + _iota((128, kt), 1)) // SEL_BLOCK)
                hit = _dot(sel, jnp.where(expand, 1.0, 0.0).astype(BF16))
                mask = (dist >= 0) & (jnp.concatenate([hit] * GQ, axis=0) > 0.5)
            else:
                mask = (dist >= 0) & (dist < WINDOW)
            return _softmax_step(carry, s, mask, v_ref[h, pl.ds(k0, kt), :])

        init = _softmax_init(rows, HEAD_DIM)
        _, l_s, a_s = lax.fori_loop(0, n_hi, functools.partial(tile, k_ref=ks_ref, v_ref=vs_ref, selected=True), init)
        _, l_w, a_w = lax.fori_loop(w_lo, n_hi, functools.partial(tile, k_ref=kw_ref, v_ref=vw_ref, selected=False), init)
        o_s = a_s / jnp.maximum(l_s, TINY)
        o_w = a_w / jnp.maximum(l_w, TINY)
        for g in range(GQ):
            hd = h * GQ + g
            sl = slice(g * Q_BLOCK, (g + 1) * Q_BLOCK)
            o = (gates[:, hd:hd + 1] * o_c[sl] + gates[:, N_HEADS + hd:N_HEADS + hd + 1] * o_s[sl]
                 + gates[:, 2 * N_HEADS + hd:2 * N_HEADS + hd + 1] * o_w[sl])
            o_ref[:, hd * HEAD_DIM:(hd + 1) * HEAD_DIM] = o


def _nsa_prompt(q_hm, ck, cv, ks, vs, kw, vw, gates, nb, kt=256):
    m = gates.shape[0]
    s = m // nb
    nq = s // Q_BLOCK
    nc = ck.shape[2]
    kv = pl.BlockSpec((N_KV, s, HEAD_DIM), lambda b, i: (0, b, 0))
    cmp_spec = pl.BlockSpec((1, N_KV, nc, HEAD_DIM), lambda b, i: (b, 0, 0, 0))
    return pl.pallas_call(
        functools.partial(_nsa_prompt_kernel, kt=kt, nc=nc), grid=(nb, nq),
        in_specs=[pl.BlockSpec((N_HEADS, Q_BLOCK, HEAD_DIM), lambda b, i: (0, b * nq + i, 0)),
                  cmp_spec, cmp_spec, kv, kv, kv, kv,
                  pl.BlockSpec((Q_BLOCK, GATE_PAD), lambda b, i: (b * nq + i, 0))],
        out_specs=pl.BlockSpec((Q_BLOCK, D_ATTN), lambda b, i: (b * nq + i, 0)),
        out_shape=jax.ShapeDtypeStruct((m, D_ATTN), F32),
        compiler_params=_params(("parallel", "parallel")), name="nsa_prompt")(q_hm, ck, cv, ks, vs, kw, vw, gates)


def _nsa_sample_kernel(pt_ref, *refs, n_pages, past):
    del pt_ref
    it = iter(refs)
    pools = [[next(it) for _ in range(n_pages)] for _ in range(4)]
    new = [next(it) for _ in range(6)]
    bkw_ref, bvw_ref, q_ref, gate_ref, slope_ref, wk_ref, wv_ref, o_ref = [next(it) for _ in range(8)]
    kbuf, vbuf = next(it), next(it)
    tok = next(it), next(it)
    nc = past // CMP_STRIDE
    n_sel = past // SEL_BLOCK + 1
    rows = N_HEADS
    slope = slope_ref[...]
    own = _iota((rows, D_KV), 1) // HEAD_DIM == _iota((rows, D_KV), 0) // GQ
    q = jnp.where(own, jnp.concatenate([q_ref[0]] * N_KV, axis=1), 0.0)
    qb = q.astype(BF16)

    def new_row(i):
        return new[i][0].astype(BF16).astype(F32)

    def own_heads(o):
        o = jnp.where(own, o, 0.0)
        return o[:, 0:64] + o[:, 64:128] + o[:, 128:192] + o[:, 192:256]

    cmp = []
    for pages, w_ref, x_new in ((pools[0], wk_ref, new[0]), (pools[1], wv_ref, new[1])):
        for i, p_ref in enumerate(pages):
            pt = p_ref[0].T
            tok[0][i * PAGE:(i + 1) * PAGE, :] = pt[:, 0:128]
            tok[1][i * PAGE:(i + 1) * PAGE, :] = pt[:, 128:256]
        a, b = _pool16(tok, w_ref, past)
        cmp.append((a + _shift_up(b, w_ref[CMP_STRIDE:CMP_STRIDE + 1, :] * x_new[0])).astype(BF16))
    ck, cv = cmp
    dist_c = past - (_iota((1, nc), 1) * CMP_STRIDE + (CMP_BLOCK - 1))
    mask_c = dist_c >= 0
    s = jnp.where(mask_c, _dot_nt(qb, ck) - slope * dist_c.astype(F32), NEG)
    p = jnp.where(mask_c, jnp.exp(s - jnp.max(s, axis=-1, keepdims=True)), 0.0)
    p = p / jnp.maximum(jnp.sum(p, axis=-1, keepdims=True), TINY)
    o_c = own_heads(_dot(p.astype(BF16), cv))
    p_grp = jnp.concatenate(
        [jnp.broadcast_to(jnp.sum(p[h * GQ:(h + 1) * GQ], axis=0, keepdims=True), (GQ, nc)) for h in range(N_KV)], axis=0)
    imp = _dot3(p_grp, _overlap_matrix(nc, 128, nc, n_sel))
    sel = _select_blocks(imp, jnp.full((rows, 1), past, jnp.int32), n_sel)

    def attend(pages, k_new, v_new, mask, new_ok, dist, width):
        for i, (kp, vp) in enumerate(pages):
            kbuf[:, i * kp.shape[1]:(i + 1) * kp.shape[1]] = kp[...].astype(BF16)
            vbuf[:, i * vp.shape[1]:(i + 1) * vp.shape[1]] = vp[...].astype(BF16)
        s = jnp.where(mask, _dot(qb, kbuf[:, 0:width]) - slope * dist.astype(F32), NEG)
        s_new = jnp.where(new_ok, jnp.sum(qb.astype(F32) * k_new, axis=-1, keepdims=True), NEG)
        m = jnp.maximum(jnp.max(s, axis=-1, keepdims=True), s_new)
        p = jnp.where(mask, jnp.exp(s - m), 0.0)
        p_new = jnp.where(new_ok, jnp.exp(s_new - m), 0.0)
        l = jnp.sum(p, axis=-1, keepdims=True) + p_new
        o = _dot_nt(p.astype(BF16), vbuf[:, 0:width]) + p_new.astype(BF16).astype(F32) * v_new
        return own_heads(o) / jnp.maximum(l, TINY)

    dist_s = past - _iota((1, past), 1)
    expand = _iota((128, past), 0) == _iota((128, past), 1) // SEL_BLOCK
    hit = _dot(sel.astype(BF16), jnp.where(expand, 1.0, 0.0).astype(BF16))
    o_s = attend([(k.at[0], v.at[0]) for k, v in zip(pools[2], pools[3])], new_row(2), new_row(3),
                 (hit > 0.5) & (dist_s >= 0), sel[:, n_sel - 1:n_sel] > 0.5, dist_s, past)
    wl = bkw_ref.shape[2]
    dist_w = wl - _iota((1, wl), 1)
    o_w = attend([(bkw_ref.at[0], bvw_ref.at[0])], new_row(4), new_row(5),
                 (dist_w >= 0) & (dist_w < WINDOW), jnp.full((rows, 1), True), dist_w, wl)
    gates = gate_ref[0]
    o_ref[0] = gates[:, 0:1] * o_c + gates[:, 1:2] * o_s + gates[:, 2:3] * o_w


def _nsa_sample(page_table, pools, new_rows, buf_kw, buf_vw, q, gates_t, slopes, wk, wv):
    nb, n_pages = page_table.shape
    past = n_pages * PAGE
    page_specs = []
    for pool in pools:
        for p in range(n_pages):
            page_specs.append(pl.BlockSpec((1,) + pool.shape[1:], lambda b, pt, p=p: (pt[b, p], 0, 0)))
    per_b = lambda shape: pl.BlockSpec((1,) + shape, lambda b, pt: (b,) + (0,) * len(shape))
    in_specs = (page_specs + [per_b((1, D_KV))] * 6 + [per_b(buf_kw.shape[1:])] * 2
                + [per_b((N_HEADS, HEAD_DIM)), per_b((N_HEADS, 3)),
                   pl.BlockSpec(slopes.shape, lambda b, pt: (0, 0)),
                   pl.BlockSpec(wk.shape, lambda b, pt: (0, 0)), pl.BlockSpec(wv.shape, lambda b, pt: (0, 0))])
    args = [pool for pool in pools for _ in range(n_pages)] + list(new_rows) + [buf_kw, buf_vw, q, gates_t, slopes, wk, wv]
    return pl.pallas_call(
        functools.partial(_nsa_sample_kernel, n_pages=n_pages, past=past),
        grid_spec=pltpu.PrefetchScalarGridSpec(
            num_scalar_prefetch=1, grid=(nb,), in_specs=in_specs,
            out_specs=per_b((N_HEADS, HEAD_DIM)),
            scratch_shapes=[pltpu.VMEM((D_KV, past), BF16)] * 2 + [pltpu.VMEM((past, D_KV // 2), F32)] * 2),
        out_shape=jax.ShapeDtypeStruct((nb, N_HEADS, HEAD_DIM), F32),
        compiler_params=_params(("arbitrary",)), name="nsa_sample")(page_table, *args)


def _s5_disc_kernel(lr_ref, li_ref, ldt_ref, ar_ref, ai_ref, fr_ref, fi_ref):
    lr, li = lr_ref[...], li_ref[...]
    dt = jnp.exp(ldt_ref[...])
    mag = jnp.exp(lr * dt)
    ar = mag * jnp.cos(li * dt)
    ai = mag * jnp.sin(li * dt)
    den = lr * lr + li * li
    ar_ref[...] = ar
    ai_ref[...] = ai
    fr_ref[...] = ((ar - 1.0) * lr + ai * li) / den
    fi_ref[...] = (ai * lr - (ar - 1.0) * li) / den


def _s5_bbar_kernel(fr_ref, fi_ref, br_ref, bi_ref, or_ref, oi_ref):
    fr, fi, br, bi = fr_ref[...], fi_ref[...], br_ref[...], bi_ref[...]
    or_ref[...] = fr * br - fi * bi
    oi_ref[...] = fr * bi + fi * br


def _s5_weights(lam_re, lam_im, log_dt, b_re, b_im, c_re, c_im):
    g, n = lam_re.shape
    sd = jax.ShapeDtypeStruct((g, n), F32)
    ar, ai, fr, fi = pl.pallas_call(_s5_disc_kernel, out_shape=[sd] * 4, name="s5_discretise")(
        lam_re, lam_im, log_dt.reshape(g, 1))
    sb = jax.ShapeDtypeStruct((g * n, SSM_GROUP), F32)
    bbr, bbi = pl.pallas_call(_s5_bbar_kernel, out_shape=[sb] * 2, name="s5_bbar")(
        fr.reshape(g * n, 1), fi.reshape(g * n, 1), b_re.reshape(g * n, SSM_GROUP), b_im.reshape(g * n, SSM_GROUP))
    eye = jnp.eye(g // SSM_BANDS, dtype=F32)
    gl = g // SSM_BANDS

    def band_in(bb):
        x = bb.reshape(SSM_BANDS, gl, n, SSM_GROUP).transpose(0, 1, 3, 2)
        return jnp.einsum("jgpn,gh->jgphn", x, eye).reshape(SSM_BANDS, gl * SSM_GROUP, gl * n).astype(BF16)

    def band_out(c):
        x = c.reshape(SSM_BANDS, gl, SSM_GROUP, n).transpose(0, 1, 3, 2)
        return jnp.einsum("jgnp,gh->jgnhp", x, eye).reshape(SSM_BANDS, gl * n, gl * SSM_GROUP).astype(BF16)

    return (ar.reshape(1, g * n), ai.reshape(1, g * n), band_in(bbr), band_in(bbi), band_out(c_re), band_out(-c_im))


def _s5_prompt_kernel(u_ref, wbr_ref, wbi_ref, ar_ref, ai_ref, wcr_ref, wci_ref, d_ref,
                      y_ref, sr_ref, si_ref, hr_s, hi_s, cr_s, ci_s, *, nb):
    c = pl.program_id(1)
    rows, width = hr_s.shape
    rep = 8 // nb

    @pl.when(c == 0)
    def _():
        cr_s[...] = jnp.zeros_like(cr_s)
        ci_s[...] = jnp.zeros_like(ci_s)

    u = u_ref[...]
    ub = u.astype(BF16)
    hr_s[...] = _dot(ub, wbr_ref[0])
    hi_s[...] = _dot(ub, wbi_ref[0])
    ar = jnp.broadcast_to(ar_ref[...], (8, width))
    ai = jnp.broadcast_to(ai_ref[...], (8, width))
    sub = _iota((8, width), 0) // nb

    def step(i, carry):
        sr, si = carry
        base = pl.multiple_of(i * 8, 8)
        xr = hr_s[pl.ds(base, 8), :]
        xi = hi_s[pl.ds(base, 8), :]
        outr, outi = xr, xi
        for k in range(rep):
            yr = ar * sr - ai * si + xr
            yi = ar * si + ai * sr + xi
            keep = sub == k
            outr = jnp.where(keep, yr, outr)
            outi = jnp.where(keep, yi, outi)
            zr = jnp.where(keep, yr, 0.0)
            zi = jnp.where(keep, yi, 0.0)
            sr, si = zr, zi
            for sh in range(1, rep):
                sr = sr + pltpu.roll(zr, sh * nb, 0)
                si = si + pltpu.roll(zi, sh * nb, 0)
        hr_s[pl.ds(base, 8), :] = outr
        hi_s[pl.ds(base, 8), :] = outi
        return sr, si

    sr, si = lax.fori_loop(0, rows // 8, step, (cr_s[...], ci_s[...]))
    cr_s[...] = sr
    ci_s[...] = si
    y = _dot(hr_s[...].astype(BF16), wcr_ref[0]) + _dot(hi_s[...].astype(BF16), wci_ref[0]) + d_ref[...] * u
    y_ref[...] = jax.nn.gelu(y)

    @pl.when(c == pl.num_programs(1) - 1)
    def _():
        sr_ref[...] = sr[0:nb]
        si_ref[...] = si[0:nb]


def _s5_prompt(u_tm, weights, d_skip, nb, tc):
    ar, ai, wbr, wbi, wcr, wci = weights
    rows = u_tm.shape[0]
    cw = D_SSM // SSM_BANDS
    sw = ar.shape[1] // SSM_BANDS
    blk = tc * nb
    tile = pl.BlockSpec((blk, cw), lambda j, c: (c, j))
    band = lambda a: pl.BlockSpec((1,) + a.shape[1:], lambda j, c: (j, 0, 0))
    vec = lambda w: pl.BlockSpec((1, w), lambda j, c: (0, j))
    state = pl.BlockSpec((nb, sw), lambda j, c: (0, j))
    return pl.pallas_call(
        functools.partial(_s5_prompt_kernel, nb=nb), grid=(SSM_BANDS, rows // blk),
        in_specs=[tile, band(wbr), band(wbi), vec(sw), vec(sw), band(wcr), band(wci), vec(cw)],
        out_specs=[tile, state, state],
        out_shape=[jax.ShapeDtypeStruct((rows, D_SSM), F32)] + [jax.ShapeDtypeStruct((nb, ar.shape[1]), F32)] * 2,
        scratch_shapes=[pltpu.VMEM((blk, sw), F32)] * 2 + [pltpu.VMEM((8, sw), F32)] * 2,
        compiler_params=_params(("parallel", "arbitrary")), name="s5_prompt")(
            u_tm, wbr, wbi, ar, ai, wcr, wci, d_skip)


def _s5_sample_kernel(u_ref, h0r_ref, h0i_ref, wbr_ref, wbi_ref, ar_ref, ai_ref, wcr_ref, wci_ref, d_ref,
                      y_ref, sr_ref, si_ref):
    cw = D_SSM // SSM_BANDS
    sw = ar_ref.shape[1] // SSM_BANDS
    for j in range(SSM_BANDS):
        cs = slice(j * cw, (j + 1) * cw)
        ss = slice(j * sw, (j + 1) * sw)
        u = u_ref[:, cs]
        ub = u.astype(BF16)
        ar, ai = ar_ref[:, ss], ai_ref[:, ss]
        h0r, h0i = h0r_ref[:, ss], h0i_ref[:, ss]
        hr = _dot(ub, wbr_ref[j]) + (ar * h0r - ai * h0i)
        hi = _dot(ub, wbi_ref[j]) + (ar * h0i + ai * h0r)
        sr_ref[:, ss] = hr
        si_ref[:, ss] = hi
        y = _dot(hr.astype(BF16), wcr_ref[j]) + _dot(hi.astype(BF16), wci_ref[j]) + d_ref[:, cs] * u
        y_ref[:, cs] = jax.nn.gelu(y)


def _s5_sample(u, h0r, h0i, weights, d_skip):
    ar, ai, wbr, wbi, wcr, wci = weights
    nb = u.shape[0]
    st = jax.ShapeDtypeStruct(h0r.shape, F32)
    return pl.pallas_call(
        _s5_sample_kernel, out_shape=[jax.ShapeDtypeStruct((nb, D_SSM), F32), st, st],
        compiler_params=pltpu.CompilerParams(vmem_limit_bytes=VMEM_LIMIT), name="s5_sample")(
            u, h0r, h0i, wbr, wbi, ar, ai, wcr, wci, d_skip)


def _xattn_prompt_kernel(q_ref, k_ref, v_ref, o_ref):
    scale = X_HEAD_DIM ** -0.5
    for h in range(X_HEADS):
        cs = slice(h * X_HEAD_DIM, (h + 1) * X_HEAD_DIM)
        s = _dot_nt(q_ref[:, cs], k_ref[:, cs].astype(BF16)) * scale
        p = jnp.exp(s - jnp.max(s, axis=-1, keepdims=True))
        p = p / jnp.sum(p, axis=-1, keepdims=True)
        o_ref[:, cs] = _dot(p.astype(BF16), v_ref[:, cs].astype(BF16)).astype(o_ref.dtype)


def _xattn_prompt(q, mk, mv, nb, tm):
    m = q.shape[0]
    nt = m // nb // tm
    mem = pl.BlockSpec((MEM_LEN, D_MODEL), lambda b, i: (b, 0))
    row = pl.BlockSpec((tm, D_MODEL), lambda b, i: (b * nt + i, 0))
    return pl.pallas_call(
        _xattn_prompt_kernel, grid=(nb, nt), in_specs=[row, mem, mem], out_specs=row,
        out_shape=jax.ShapeDtypeStruct((m, D_MODEL), BF16),
        compiler_params=_params(("parallel", "parallel")), name="xattn_prompt")(q, mk, mv)


def _xattn_sample_kernel(q_ref, k_ref, v_ref, o_ref):
    scale = X_HEAD_DIM ** -0.5
    nt = X_HEAD_DIM // 128
    rows = MEM_LEN * nt * X_HEADS

    def head(ref, h):
        parts = [ref[0, pl.ds(t * X_HEADS + h, MEM_LEN, stride=nt * X_HEADS), :] for t in range(nt)]
        return jnp.concatenate(parts, axis=1).astype(BF16)

    assert k_ref.shape[1] == rows
    for h in range(X_HEADS):
        cs = slice(h * X_HEAD_DIM, (h + 1) * X_HEAD_DIM)
        q = jnp.broadcast_to(q_ref[0, :, cs], (8, X_HEAD_DIM)).astype(BF16)
        s = _dot_nt(q, head(k_ref, h)) * scale
        p = jnp.exp(s - jnp.max(s, axis=-1, keepdims=True))
        p = p / jnp.sum(p, axis=-1, keepdims=True)
        o_ref[0, :, cs] = _dot(p.astype(BF16), head(v_ref, h))[0:1].astype(o_ref.dtype)


def _xattn_sample(q, mk, mv):
    nb = q.shape[0]
    mem = pl.BlockSpec((1,) + mk.shape[1:], lambda b: (b, 0, 0))
    row = pl.BlockSpec((1, 1, D_MODEL), lambda b: (b, 0, 0))
    return pl.pallas_call(
        _xattn_sample_kernel, grid=(nb,), in_specs=[row, mem, mem], out_specs=row,
        out_shape=jax.ShapeDtypeStruct((nb, 1, D_MODEL), BF16),
        compiler_params=_params(("parallel",)), name="xattn_sample")(q, mk, mv)


def _top_distinct(x, n):
    rows = []
    for _ in range(n):
        m = jnp.max(x, axis=0, keepdims=True)
        rows.append(m)
        x = jnp.where(x == m, LOWEST, x)
    return jnp.concatenate(rows, axis=0)


def _peer_route_kernel(x_ref, g_ref, w_ref, k1_ref, k2_ref, xt_ref, thr_ref, s2_ref, e2_ref, c1_ref):
    xn = _rms(x_ref[...], g_ref[...])
    xt_ref[...] = xn.T.astype(BF16)
    q = _dot(xn.astype(BF16), w_ref[...]).astype(BF16)
    half = PEER_KEYS
    for h in range(PEER_HEADS):
        s1 = _dot_nt(k1_ref[...], q[:, 2 * half * h:2 * half * h + half])
        s2 = _dot_nt(k2_ref[...], q[:, 2 * half * h + half:2 * half * (h + 1)])
        d1 = _top_distinct(s1, PEER_TOPK)
        d2 = _top_distinct(s2, PEER_TOPK)
        cand = jnp.concatenate([d1[a:a + 1] + d2 for a in range(PEER_TOPK)], axis=0)
        tau = _top_distinct(cand, PEER_TOPK)[PEER_TOPK - 1:PEER_TOPK]
        top = d1[0:1] + d2[0:1]
        z = jnp.sum(jnp.where(cand >= tau, jnp.exp(cand - top), 0.0), axis=0, keepdims=True)
        thr = jnp.full(s1.shape, HUGE, F32)
        for a in range(PEER_TOPK):
            ok = (d1[a:a + 1] + d2) >= tau
            thr_a = jnp.min(jnp.where(ok, d2, HUGE), axis=0, keepdims=True)
            thr = jnp.where(s1 == d1[a:a + 1], thr_a, thr)
        thr_ref[h] = thr
        s2_ref[h] = s2
        e2_ref[h] = jnp.exp(s2 - d2[0:1])
        c1_ref[h] = jnp.exp(s1 - d1[0:1]) / z


def _peer_route(x, g, w_pq, k1, k2, tm):
    m = x.shape[0]
    hk = pl.BlockSpec((PEER_HEADS, PEER_KEYS, tm), lambda i: (0, 0, i))
    hks = jax.ShapeDtypeStruct((PEER_HEADS, PEER_KEYS, m), F32)
    return pl.pallas_call(
        _peer_route_kernel, grid=(m // tm,),
        in_specs=[pl.BlockSpec((tm, D_MODEL), lambda i: (i, 0)), _full(g), _full(w_pq), _full(k1), _full(k2)],
        out_specs=[pl.BlockSpec((D_MODEL, tm), lambda i: (0, i)), hk, hk, hk, hk],
        out_shape=[jax.ShapeDtypeStruct((D_MODEL, m), BF16), hks, hks, hks, hks],
        compiler_params=_params(("parallel",)), name="peer_route")(x, g, w_pq, k1, k2)


def _peer_kernel(xt_ref, u_ref, vt_ref, thr_ref, s2_ref, e2_ref, c1_ref, o_ref, wa_s, wb_s, act_s):
    e = pl.program_id(1)
    et, tt = act_s.shape
    n_sub = et // PEER_KEYS
    group = 2
    oc = o_ref.shape[0] // n_sub

    @pl.when(e == 0)
    def _():
        o_ref[...] = jnp.zeros_like(o_ref)
        wb_s[...] = jnp.zeros_like(wb_s)

    def step(cur_s, prev_s):
        def score(jj):
            rs = slice(jj * PEER_KEYS, (jj + 1) * PEER_KEYS)
            act_s[rs, :] = _dot(u_ref[rs, :], xt_ref[...])

        def emit(c):
            rs = slice(c * oc, (c + 1) * oc)
            o_ref[rs, :] += _dot(vt_ref[rs, :], prev_s[...])

        def gate_group(j0, lt):
            ls = slice(lt * 128, (lt + 1) * 128)
            gates = [None] * group
            for h in range(PEER_HEADS):
                s2 = s2_ref[h, :, ls]
                e2 = e2_ref[h, :, ls]
                for k in range(group):
                    jj = j0 + k
                    g = jnp.where(s2 >= thr_ref[h, jj:jj + 1, ls], e2, 0.0) * c1_ref[h, jj:jj + 1, ls]
                    gates[k] = g if gates[k] is None else gates[k] + g
            for k in range(group):
                rs = slice((j0 + k) * PEER_KEYS, (j0 + k + 1) * PEER_KEYS)
                cur_s[rs, ls] = (gates[k] * jax.nn.gelu(act_s[rs, ls])).astype(BF16)

        scores = list(range(n_sub))
        emits = list(range(n_sub))
        for j0 in range(0, n_sub, group):
            while scores and scores[0] < j0 + group:
                score(scores.pop(0))
            for lt in range(tt // 128):
                gate_group(j0, lt)
                if scores:
                    score(scores.pop(0))
                elif emits:
                    emit(emits.pop(0))
        for c in emits:
            emit(c)

    @pl.when(e % 2 == 0)
    def _():
        step(wa_s, wb_s)

    @pl.when(e % 2 == 1)
    def _():
        step(wb_s, wa_s)


def _peer(xt, u, vt, thr, s2, e2, c1, tt, et):
    m = xt.shape[1]
    n_e = u.shape[0] // et
    tok3 = lambda r: pl.BlockSpec((PEER_HEADS, r, tt), lambda i, e: (0, 0, i))
    tile3 = pl.BlockSpec((PEER_HEADS, et // PEER_KEYS, tt), lambda i, e: (0, jnp.minimum(e, n_e - 1), i))
    return pl.pallas_call(
        _peer_kernel, grid=(m // tt, n_e + 1),
        in_specs=[pl.BlockSpec((D_MODEL, tt), lambda i, e: (0, i)),
                  pl.BlockSpec((et, D_MODEL), lambda i, e: (jnp.minimum(e, n_e - 1), 0)),
                  pl.BlockSpec((D_MODEL, et), lambda i, e: (0, jnp.maximum(e - 1, 0))),
                  tile3, tok3(PEER_KEYS), tok3(PEER_KEYS), tile3],
        out_specs=pl.BlockSpec((D_MODEL, tt), lambda i, e: (0, i)),
        out_shape=jax.ShapeDtypeStruct((D_MODEL, m), F32),
        scratch_shapes=[pltpu.VMEM((et, tt), BF16)] * 2 + [pltpu.VMEM((et, tt), F32)],
        compiler_params=_params(("parallel", "arbitrary")), name="peer_dense")(
            xt, u, vt, thr, s2, e2, c1)


def _final_kernel(h_ref, ot_ref, g_ref, y_ref):
    y_ref[...] = _rms(h_ref[...] + ot_ref[...].T, g_ref[...])


def _final_norm(h, out_t, g, tm):
    m = h.shape[0]
    row = pl.BlockSpec((tm, D_MODEL), lambda i: (i, 0))
    return pl.pallas_call(
        _final_kernel, grid=(m // tm,),
        in_specs=[row, pl.BlockSpec((D_MODEL, tm), lambda i: (0, i)), _full(g)], out_specs=row,
        out_shape=jax.ShapeDtypeStruct((m, D_MODEL), F32),
        compiler_params=_params(("parallel",)), name="final_norm")(h, out_t, g)


def _row(v):
    return v.reshape(1, -1).astype(F32)


def _cmp_rows(w):
    return jnp.repeat(w.T.astype(F32), HEAD_DIM, axis=1)


def kernel(x_prompt, x_sample, mem_prompt, cache_k_cmp, cache_v_cmp, cache_k_sel, cache_v_sel, cache_k_win, cache_v_win, state_s5_re, state_s5_im, cache_mem_k, cache_mem_v, page_table, g_mix, w_in, w_cmp_k, w_cmp_v, lam_re, lam_im, log_dt, b_re, b_im, c_re, c_im, d_skip, w_glu, g_attn_out, g_ssm_out, w_out, g_x, g_mem, w_xq, w_xk, w_xv, w_xo, g_ffn, w_pq, peer_k1, peer_k2, peer_u, peer_v, g_final):
    nb, seq, _ = x_prompt.shape
    db = x_sample.shape[0]
    depth = g_mix.shape[0]
    assert depth == 1 and x_sample.shape[1] == 1
    l = 0
    mp = nb * seq
    n_gate = 3 * N_HEADS
    kv_end = D_ATTN + 6 * D_KV

    w_attn = jnp.concatenate([w_in[l][:, :kv_end + n_gate], jnp.zeros((D_MODEL, GATE_PAD - n_gate), F32)], axis=1).astype(BF16)
    w_u = w_in[l][:, kv_end + n_gate:].astype(BF16)
    w_full = jnp.concatenate([w_attn, w_u], axis=1)
    w_glu_b = w_glu[l].astype(BF16)
    w_out_a = w_out[l][:D_ATTN].astype(BF16)
    w_out_s = w_out[l][D_ATTN:].astype(BF16)
    w_mem = jnp.concatenate([w_xk[l], w_xv[l]], axis=1).astype(BF16)
    w_xq_b = w_xq[l].astype(BF16)
    w_xo_b = w_xo[l].astype(BF16)
    w_pq_b = w_pq[l].astype(BF16)
    k1_b = peer_k1[l].astype(BF16)
    k2_b = peer_k2[l].astype(BF16)
    u_b = peer_u[l].astype(BF16)
    vt_b = peer_v[l].T.astype(BF16)
    wk_rows = _cmp_rows(w_cmp_k[l])
    wv_rows = _cmp_rows(w_cmp_v[l])
    gm, gx, gf, gfin = _row(g_mix[l]), _row(g_x[l]), _row(g_ffn[l]), _row(g_final)
    ga, gs, gme = _row(g_attn_out[l]), _row(g_ssm_out[l]), _row(g_mem[l])
    s5w = _s5_weights(lam_re[l], lam_im[l], log_dt[l], b_re[l], b_im[l], c_re[l], c_im[l])
    dsk = _row(d_skip[l])

    xp = x_prompt.reshape(mp, D_MODEL)
    q_hm, kv, kv_hm, gates = _proj_attn(xp, gm, w_attn, tm=256)
    u_tm = _proj_u(xp, gm, w_u, nb, tm=256).reshape(seq * nb, D_SSM)
    ck, cv = _compress(kv[0], kv[1], wk_rows, wv_rows, nb)
    o_a = _nsa_prompt(q_hm, ck, cv, kv_hm[2], kv_hm[3], kv_hm[4], kv_hm[5], gates, nb)
    y_tm, p_sr, p_si = _s5_prompt(u_tm, s5w, dsk, nb, tc=256)
    o_s = _glu(y_tm.reshape(seq, nb * D_SSM), w_glu_b, nb, tm=256)
    h1 = _merge(o_a, o_s, ga, gs, w_out_a, w_out_s, xp, tm=256)
    mk, mv = _mem_kv(mem_prompt.reshape(nb * MEM_LEN, D_MODEL), gme, w_mem, tm=256)
    xq = _mm(h1, w_xq_b, 256, BF16, g=gx, name="xattn_q")
    xo = _xattn_prompt(xq, mk, mv, nb, tm=256)
    h2 = _mm(xo, w_xo_b, 256, F32, res=h1, name="xattn_o")
    routed = _peer_route(h2, gf, w_pq_b, k1_b, k2_b, tm=256)
    y_p = _final_norm(h2, _peer(routed[0], u_b, vt_b, *routed[1:], tt=512, et=1024), gfin, tm=256)

    xs = x_sample.reshape(db, D_MODEL)
    z = _mm(xs, w_full, db, F32, g=gm, gate_cols=(kv_end, kv_end + n_gate), name="proj_sample")
    new_rows = [z[:, D_ATTN + D_KV * i:D_ATTN + D_KV * (i + 1)] for i in range(6)]
    n_phys = cache_k_cmp.shape[1]
    pools = [c[l].transpose(0, 2, 3, 1).reshape(n_phys, D_KV, PAGE)
             for c in (cache_k_cmp, cache_v_cmp, cache_k_sel, cache_v_sel)]
    wl = cache_k_win.shape[2]
    buf_kw = cache_k_win[l].transpose(0, 2, 3, 1).reshape(db, D_KV, wl)
    buf_vw = cache_v_win[l].transpose(0, 2, 3, 1).reshape(db, D_KV, wl)
    q_s = (z[:, :D_ATTN] * (HEAD_DIM ** -0.5)).reshape(db, N_HEADS, HEAD_DIM)
    gates_t = z[:, kv_end:kv_end + n_gate].reshape(db, 3, N_HEADS).transpose(0, 2, 1)
    slopes = jnp.asarray(np.array([[_alibi_slope(i)] for i in range(N_HEADS)], np.float32))
    o_a_s = _nsa_sample(page_table, pools, [r.reshape(db, 1, D_KV) for r in new_rows], buf_kw, buf_vw,
                        q_s, gates_t, slopes, wk_rows, wv_rows).reshape(db, D_ATTN)
    y_s, s_sr, s_si = _s5_sample(z[:, kv_end + GATE_PAD:], state_s5_re[l].reshape(db, -1), state_s5_im[l].reshape(db, -1), s5w, dsk)
    o_s_s = _glu(y_s, w_glu_b, 1, tm=db)
    h1s = _merge(o_a_s, o_s_s, ga, gs, w_out_a, w_out_s, xs, tm=db)
    xq_s = _mm(h1s, w_xq_b, db, F32, g=gx, name="xattn_q_sample")
    nt = X_HEAD_DIM // 128

    def mem_rows(c):
        return c.reshape(db, MEM_LEN, X_HEADS, nt, 128).transpose(0, 1, 3, 2, 4).reshape(db, MEM_LEN * nt * X_HEADS, 128)

    xo_s = _xattn_sample(xq_s.reshape(db, 1, D_MODEL), mem_rows(cache_mem_k[l]), mem_rows(cache_mem_v[l])).reshape(db, D_MODEL)
    h2s = _mm(xo_s, w_xo_b, db, F32, res=h1s, name="xattn_o_sample")
    routed_s = _peer_route(h2s, gf, w_pq_b, k1_b, k2_b, tm=db)
    y_s_out = _final_norm(h2s, _peer(routed_s[0], u_b, vt_b, *routed_s[1:], tt=db, et=1024), gfin, tm=db)

    kvshape = (1, nb, seq, N_KV, HEAD_DIM)
    wlp = min(WINDOW, seq)
    p_kv = [a.reshape(kvshape) for a in kv]
    p_win = [a[:, :, seq - wlp:] for a in p_kv[4:6]]
    s_new = [r.reshape(1, db, 1, N_KV, HEAD_DIM) for r in new_rows]
    s_win = [jnp.concatenate([c[l], n[0]], axis=1)[None, :, -min(WINDOW, wl + 1):] for c, n in ((cache_k_win, s_new[4]), (cache_v_win, s_new[5]))]
    g64 = (1, -1, N_SSM_GROUPS, SSM_STATE)
    return (y_p.reshape(nb, seq, D_MODEL), y_s_out.reshape(db, 1, D_MODEL),
            p_kv[0], p_kv[1], p_kv[2], p_kv[3], p_win[0], p_win[1],
            p_sr.reshape(g64), p_si.reshape(g64),
            mk.reshape(1, nb, MEM_LEN, X_HEADS, X_HEAD_DIM), mv.reshape(1, nb, MEM_LEN, X_HEADS, X_HEAD_DIM),
            s_new[0], s_new[1], s_new[2], s_new[3], s_win[0], s_win[1],
            s_sr.reshape(g64), s_si.reshape(g64))
```

```python
import functools
import math

import jax
import jax.numpy as jnp
import numpy as np
from jax import lax
from jax.experimental import pallas as pl
from jax.experimental.pallas import tpu as pltpu

F32 = jnp.float32
BF16 = jnp.bfloat16

D_MODEL = 2048
N_HEADS = 16
HEAD_DIM = 64
N_KV = 4
GQ = 4
D_ATTN = 1024
D_SSM = 1024
D_KV = 256
CMP_BLOCK = 32
CMP_STRIDE = 16
SEL_BLOCK = 64
N_SEL = 8
WINDOW = 512
Q_BLOCK = 128
HEADS_PER_LOOP = 2
FORCE = 1e4
N_SSM_GROUPS = 64
SSM_GROUP = 16
SSM_STATE = 64
SSM_BANDS = 4
MEM_LEN = 256
X_HEADS = 4
X_HEAD_DIM = 512
PEER_KEYS = 128
PEER_HEADS = 8
PEER_TOPK = 16
GATE_ROWS = 32
SCORE_ROWS = 128
EMIT_ROWS = 256
PAGE = 128
EPS = 1e-6
NEG = -1e30
TINY = 1e-30
LOWEST = -3.0e38
HUGE = 3.0e38
GATE_PAD = 128
VMEM_LIMIT = 56 * 2**20


def _params(sem, flags=None):
    return pltpu.CompilerParams(dimension_semantics=sem, vmem_limit_bytes=VMEM_LIMIT, flags=flags)


def _full(a):
    nd = a.ndim
    return pl.BlockSpec(a.shape, lambda *_: (0,) * nd)


def _rms(x, g):
    return x * lax.rsqrt(jnp.mean(x * x, axis=-1, keepdims=True) + EPS) * g


def _dot(a, b):
    return jnp.dot(a, b, preferred_element_type=F32)


def _dot_nt(a, b):
    return lax.dot_general(a, b, (((1,), (1,)), ((), ())), preferred_element_type=F32)


def _dot3(a, b_exact):
    hi = a.astype(BF16)
    r1 = a - hi.astype(F32)
    mid = r1.astype(BF16)
    lo = (r1 - mid.astype(F32)).astype(BF16)
    return _dot(hi, b_exact) + _dot(mid, b_exact) + _dot(lo, b_exact)


def _iota(shape, dim):
    return lax.broadcasted_iota(jnp.int32, shape, dim)


def _proj_attn_kernel(x_ref, g_ref, w_ref, q_ref, *rest):
    kv_refs, kvh_refs, gate_ref = rest[:6], rest[6:12], rest[12]
    xn = _rms(x_ref[...], g_ref[...]).astype(BF16)
    z = _dot(xn, w_ref[...])
    for hd in range(N_HEADS):
        q_ref[hd] = (z[:, hd * HEAD_DIM:(hd + 1) * HEAD_DIM] * (HEAD_DIM ** -0.5)).astype(BF16)
    for k in range(6):
        zk = z[:, D_ATTN + D_KV * k:D_ATTN + D_KV * (k + 1)]
        kv_refs[k][...] = zk
        for h in range(N_KV):
            kvh_refs[k][h] = zk[:, h * HEAD_DIM:(h + 1) * HEAD_DIM].astype(BF16)
    gate_ref[...] = jax.nn.sigmoid(z[:, D_ATTN + 6 * D_KV:])


def _proj_attn(x, g, w, tm):
    m = x.shape[0]
    row = lambda n: pl.BlockSpec((tm, n), lambda i: (i, 0))
    hm = lambda n: pl.BlockSpec((n, tm, HEAD_DIM), lambda i: (0, i, 0))
    out_shape = ([jax.ShapeDtypeStruct((N_HEADS, m, HEAD_DIM), BF16)]
                 + [jax.ShapeDtypeStruct((m, D_KV), F32)] * 6
                 + [jax.ShapeDtypeStruct((N_KV, m, HEAD_DIM), BF16)] * 6
                 + [jax.ShapeDtypeStruct((m, GATE_PAD), F32)])
    out_specs = [hm(N_HEADS)] + [row(D_KV)] * 6 + [hm(N_KV)] * 6 + [row(GATE_PAD)]
    outs = pl.pallas_call(
        _proj_attn_kernel, grid=(m // tm,),
        in_specs=[row(D_MODEL), _full(g), _full(w)],
        out_specs=out_specs, out_shape=out_shape,
        compiler_params=_params(("parallel",)), name="proj_attn")(x, g, w)
    return outs[0], outs[1:7], outs[7:13], outs[13]


def _proj_u_kernel(x_ref, g_ref, w_ref, u_ref):
    xn = _rms(x_ref[...], g_ref[...]).astype(BF16)
    u_ref[...] = _dot(xn, w_ref[...])


def _proj_u(x, g, w, nb, tm):
    m = x.shape[0]
    t = m // nb
    nt = t // tm
    return pl.pallas_call(
        _proj_u_kernel, grid=(nb, nt),
        in_specs=[pl.BlockSpec((tm, D_MODEL), lambda b, i: (b * nt + i, 0)), _full(g), _full(w)],
        out_specs=pl.BlockSpec((tm, D_SSM), lambda b, i: (i, b)),
        out_shape=jax.ShapeDtypeStruct((t, nb * D_SSM), F32),
        compiler_params=_params(("parallel", "parallel")), name="proj_u")(x, g, w)


def _mm_kernel(*refs, norm, res, gate_cols):
    it = iter(refs)
    x_ref = next(it)
    g_ref = next(it) if norm else None
    w_ref = next(it)
    r_ref = next(it) if res else None
    o_ref = next(it)
    x = x_ref[...]
    if norm:
        x = _rms(x.astype(F32), g_ref[...])
    z = _dot(x.astype(BF16), w_ref[...])
    if res:
        z = z + r_ref[...]
    if gate_cols is not None:
        col = _iota(z.shape, 1)
        z = jnp.where((col >= gate_cols[0]) & (col < gate_cols[1]), jax.nn.sigmoid(z), z)
    o_ref[...] = z.astype(o_ref.dtype)


def _mm(x, w, tm, out_dtype, g=None, res=None, gate_cols=None, name="mm"):
    m, k = x.shape
    n = w.shape[1]
    row = lambda c: pl.BlockSpec((tm, c), lambda i: (i, 0))
    args, specs = [x], [row(k)]
    if g is not None:
        args.append(g)
        specs.append(_full(g))
    args.append(w)
    specs.append(_full(w))
    if res is not None:
        args.append(res)
        specs.append(row(n))
    return pl.pallas_call(
        functools.partial(_mm_kernel, norm=g is not None, res=res is not None, gate_cols=gate_cols),
        grid=(m // tm,), in_specs=specs, out_specs=row(n),
        out_shape=jax.ShapeDtypeStruct((m, n), out_dtype),
        compiler_params=_params(("parallel",)), name=name)(*args)


def _mm2_kernel(x_ref, g_ref, w_ref, o0_ref, o1_ref):
    xn = _rms(x_ref[...], g_ref[...]).astype(BF16)
    z = _dot(xn, w_ref[...])
    n = o0_ref.shape[1]
    o0_ref[...] = z[:, :n]
    o1_ref[...] = z[:, n:]


def _mem_kv(mem, g, w, tm):
    m = mem.shape[0]
    row = pl.BlockSpec((tm, D_MODEL), lambda i: (i, 0))
    return pl.pallas_call(
        _mm2_kernel, grid=(m // tm,), in_specs=[row, _full(g), _full(w)], out_specs=[row, row],
        out_shape=[jax.ShapeDtypeStruct((m, D_MODEL), F32)] * 2,
        compiler_params=_params(("parallel",)), name="mem_kv")(mem, g, w)


def _glu_kernel(y_ref, w_ref, o_ref):
    y = y_ref[...]
    o_ref[...] = y * jax.nn.sigmoid(_dot(y.astype(BF16), w_ref[...]))


def _glu(y_tm, w, nb, tm):
    t = y_tm.shape[0]
    nt = t // tm
    return pl.pallas_call(
        _glu_kernel, grid=(nb, nt),
        in_specs=[pl.BlockSpec((tm, D_SSM), lambda b, i: (i, b)), _full(w)],
        out_specs=pl.BlockSpec((tm, D_SSM), lambda b, i: (b * nt + i, 0)),
        out_shape=jax.ShapeDtypeStruct((nb * t, D_SSM), F32),
        compiler_params=_params(("parallel", "parallel")), name="glu")(y_tm, w)


def _merge_kernel(oa_ref, os_ref, ga_ref, gs_ref, wa_ref, ws_ref, x_ref, o_ref):
    a = _rms(oa_ref[...], ga_ref[...]).astype(BF16)
    s = _rms(os_ref[...], gs_ref[...]).astype(BF16)
    o_ref[...] = x_ref[...] + (_dot(a, wa_ref[...]) + _dot(s, ws_ref[...]))


def _merge(o_a, o_s, g_a, g_s, w_a, w_s, x, tm):
    m = x.shape[0]
    half = pl.BlockSpec((tm, D_ATTN), lambda i: (i, 0))
    row = pl.BlockSpec((tm, D_MODEL), lambda i: (i, 0))
    return pl.pallas_call(
        _merge_kernel, grid=(m // tm,),
        in_specs=[half, half, _full(g_a), _full(g_s), _full(w_a), _full(w_s), row],
        out_specs=row, out_shape=jax.ShapeDtypeStruct((m, D_MODEL), F32),
        compiler_params=_params(("parallel",)), name="merge_heads")(o_a, o_s, g_a, g_s, w_a, w_s, x)


def _pool16(x_refs, w_ref, tokens):
    halves = []
    for hf, x_ref in enumerate(x_refs):
        lanes = slice(hf * 128, (hf + 1) * 128)
        a = b = None
        for j in range(CMP_STRIDE):
            xj = x_ref[pl.ds(j, tokens // CMP_STRIDE, stride=CMP_STRIDE), :]
            ta = xj * w_ref[j:j + 1, lanes]
            tb = xj * w_ref[CMP_STRIDE + j:CMP_STRIDE + j + 1, lanes]
            a = ta if a is None else a + ta
            b = tb if b is None else b + tb
        halves.append((a, b))
    return (jnp.concatenate([halves[0][0], halves[1][0]], axis=1),
            jnp.concatenate([halves[0][1], halves[1][1]], axis=1))


def _shift_up(b, last_row):
    n = b.shape[0]
    rolled = pltpu.roll(b, n - 1, 0)
    return jnp.where(_iota(b.shape, 0) == n - 1, last_row, rolled)


def _compress_kernel(k0_ref, k1_ref, v0_ref, v1_ref, wk_ref, wv_ref, ck_ref, cv_ref):
    tokens = k0_ref.shape[0]
    for x_refs, w_ref, o_ref in (((k0_ref, k1_ref), wk_ref, ck_ref), ((v0_ref, v1_ref), wv_ref, cv_ref)):
        a, b = _pool16(x_refs, w_ref, tokens)
        c = (a + _shift_up(b, 0.0)).astype(BF16)
        for h in range(N_KV):
            o_ref[0, h] = c[:, h * HEAD_DIM:(h + 1) * HEAD_DIM]


def _compress(kc, vc, wk, wv, nb):
    s = kc.shape[0] // nb
    nc = s // CMP_STRIDE
    lo = pl.BlockSpec((s, D_KV // 2), lambda b: (b, 0))
    hi = pl.BlockSpec((s, D_KV // 2), lambda b: (b, 1))
    out = pl.BlockSpec((1, N_KV, nc, HEAD_DIM), lambda b: (b, 0, 0, 0))
    return pl.pallas_call(
        _compress_kernel, grid=(nb,), in_specs=[lo, hi, lo, hi, _full(wk), _full(wv)], out_specs=[out, out],
        out_shape=[jax.ShapeDtypeStruct((nb, N_KV, nc, HEAD_DIM), BF16)] * 2,
        compiler_params=_params(("parallel",)), name="nsa_compress")(kc, kc, vc, vc, wk, wv)


def _overlap_matrix(nc, width, n_cmp, n_sel):
    i = _iota((nc, width), 0)
    j = _iota((nc, width), 1)
    lo = jnp.maximum(i * CMP_STRIDE, j * SEL_BLOCK)
    hi = jnp.minimum(i * CMP_STRIDE + CMP_BLOCK, (j + 1) * SEL_BLOCK)
    ov = jnp.maximum(hi - lo, 0).astype(F32) * (1.0 / CMP_BLOCK)
    return jnp.where((i < n_cmp) & (j < n_sel), ov, 0.0).astype(BF16)


def _select_blocks(imp, qpos, n_sel):
    blk = _iota(imp.shape, 1)
    valid = blk * SEL_BLOCK <= qpos
    forced = (blk == qpos // SEL_BLOCK) | (blk == 0)
    x = jnp.where(valid, imp + jnp.where(forced, FORCE, 0.0), NEG)
    x = jnp.where(blk < n_sel, x, LOWEST)
    sel = jnp.zeros(imp.shape, F32)
    blk_f = blk.astype(F32)
    for _ in range(N_SEL):
        m = jnp.max(x, axis=-1, keepdims=True)
        first = jnp.min(jnp.where(x == m, blk_f, 4.0 * imp.shape[1]), axis=-1, keepdims=True)
        pick = blk_f == first
        sel = jnp.where(pick & (m > 0.5 * NEG), 1.0, sel)
        x = jnp.where(pick, LOWEST, x)
    return sel


M_INIT = 0.1 * NEG


def _softmax_step(carry, s, v):
    m, l, acc = carry
    m_new = jnp.maximum(m, jnp.max(s, axis=-1, keepdims=True))
    p = jnp.exp(s - m_new)
    alpha = jnp.exp(m - m_new)
    l = alpha * l + jnp.sum(p, axis=-1, keepdims=True)
    acc = alpha * acc + _dot(p.astype(BF16), v)
    return m_new, l, acc


def _softmax_init(rows, width):
    return jnp.full((rows, 1), M_INIT, F32), jnp.zeros((rows, 1), F32), jnp.zeros((rows, width), F32)


def _alibi_slope(head):
    return float(2.0 ** (-8.0 * (head + 1) / N_HEADS))


def _nsa_prompt_kernel(q_ref, ck_ref, cv_ref, ks_ref, vs_ref, kw_ref, vw_ref, gate_ref, o_ref, *, kt, nc):
    qi = pl.program_id(1)
    rows = GQ * Q_BLOCK
    q0 = qi * Q_BLOCK
    r = _iota((rows, 1), 0)
    qpos = q0 + (r & (Q_BLOCK - 1))
    qpos_tok = q0 + _iota((Q_BLOCK, 1), 0)
    n_sel = ks_ref.shape[1] // SEL_BLOCK
    ov = _overlap_matrix(nc, 128, nc - 1, n_sel)
    cend = _iota((1, nc), 1) * CMP_STRIDE + (CMP_BLOCK - 1)
    n_hi = (q0 + Q_BLOCK + kt - 1) // kt
    w_lo = jnp.maximum(q0 - (WINDOW - 1), 0) // kt
    gates = gate_ref[...]
    dist_c = qpos - cend
    mask_c = dist_c >= 0

    def prepare(h):
        slope = jnp.zeros((rows, 1), F32)
        for g in range(GQ):
            slope = jnp.where(r // Q_BLOCK == g, _alibi_slope(h * GQ + g), slope)
        q = q_ref[h * GQ:(h + 1) * GQ].reshape(rows, HEAD_DIM)
        s = _dot_nt(q, ck_ref[0, h]) - slope * dist_c.astype(F32)
        s = jnp.where(mask_c, s, NEG)
        p = jnp.where(mask_c, jnp.exp(s - jnp.max(s, axis=-1, keepdims=True)), 0.0)
        p = p / jnp.maximum(jnp.sum(p, axis=-1, keepdims=True), TINY)
        o_c = _dot(p.astype(BF16), cv_ref[0, h])
        p_grp = p[0:Q_BLOCK]
        for g in range(1, GQ):
            p_grp = p_grp + p[g * Q_BLOCK:(g + 1) * Q_BLOCK]
        sel = _select_blocks(_dot3(p_grp, ov), qpos_tok, n_sel).astype(BF16)
        return q, slope, sel, o_c

    def tile(t, carry, h, q, slope, sel, k_ref, v_ref):
        k0 = pl.multiple_of(t * kt, kt)
        kpos = k0 + _iota((1, kt), 1)
        kpos_f = kpos.astype(F32)
        s = _dot_nt(q, k_ref[h, pl.ds(k0, kt), :])
        ok = qpos_tok >= kpos
        if sel is not None:
            expand = (_iota((128, kt), 0) == (k0 + _iota((128, kt), 1)) // SEL_BLOCK)
            ok = ok & (_dot(sel, jnp.where(expand, 1.0, 0.0).astype(BF16)) > 0.5)
        else:
            ok = ok & (qpos_tok - kpos < WINDOW)
        s = jnp.concatenate([jnp.where(ok, s[g * Q_BLOCK:(g + 1) * Q_BLOCK] + _alibi_slope(h * GQ + g) * kpos_f, NEG)
                             for g in range(GQ)], axis=0)
        return _softmax_step(carry, s, v_ref[h, pl.ds(k0, kt), :])

    for h0 in range(0, N_KV, HEADS_PER_LOOP):
        heads = range(h0, h0 + HEADS_PER_LOOP)
        prep = [prepare(h) for h in heads]

        def sel_tiles(t, carry):
            return [tile(t, c, h, q, slope, sel, ks_ref, vs_ref) for c, h, (q, slope, sel, _) in zip(carry, heads, prep)]

        def win_tiles(t, carry):
            return [tile(t, c, h, q, slope, None, kw_ref, vw_ref) for c, h, (q, slope, _, _) in zip(carry, heads, prep)]

        init = [_softmax_init(rows, HEAD_DIM) for _ in heads]
        sel_c = lax.fori_loop(0, w_lo, sel_tiles, init)
        sel_c, win_c = lax.fori_loop(w_lo, n_hi, lambda t, c: (sel_tiles(t, c[0]), win_tiles(t, c[1])), (sel_c, init))
        for h, (_, _, _, o_c), (_, l_s, a_s), (_, l_w, a_w) in zip(heads, prep, sel_c, win_c):
            o_s = a_s / jnp.maximum(l_s, TINY)
            o_w = a_w / jnp.maximum(l_w, TINY)
            for g in range(GQ):
                hd = h * GQ + g
                sl = slice(g * Q_BLOCK, (g + 1) * Q_BLOCK)
                o = (gates[:, hd:hd + 1] * o_c[sl] + gates[:, N_HEADS + hd:N_HEADS + hd + 1] * o_s[sl]
                     + gates[:, 2 * N_HEADS + hd:2 * N_HEADS + hd + 1] * o_w[sl])
                o_ref[:, hd * HEAD_DIM:(hd + 1) * HEAD_DIM] = o


def _nsa_prompt(q_hm, ck, cv, ks, vs, kw, vw, gates, nb, kt=256):
    m = gates.shape[0]
    s = m // nb
    nq = s // Q_BLOCK
    nc = ck.shape[2]
    kv = pl.BlockSpec((N_KV, s, HEAD_DIM), lambda b, i: (0, b, 0))
    cmp_spec = pl.BlockSpec((1, N_KV, nc, HEAD_DIM), lambda b, i: (b, 0, 0, 0))
    return pl.pallas_call(
        functools.partial(_nsa_prompt_kernel, kt=kt, nc=nc), grid=(nb, nq),
        in_specs=[pl.BlockSpec((N_HEADS, Q_BLOCK, HEAD_DIM), lambda b, i: (0, b * nq + i, 0)),
                  cmp_spec, cmp_spec, kv, kv, kv, kv,
                  pl.BlockSpec((Q_BLOCK, GATE_PAD), lambda b, i: (b * nq + i, 0))],
        out_specs=pl.BlockSpec((Q_BLOCK, D_ATTN), lambda b, i: (b * nq + i, 0)),
        out_shape=jax.ShapeDtypeStruct((m, D_ATTN), F32),
        compiler_params=_params(("parallel", "parallel")), name="nsa_prompt")(q_hm, ck, cv, ks, vs, kw, vw, gates)


def _nsa_sample_kernel(pt_ref, *refs, n_pages, past):
    del pt_ref
    it = iter(refs)
    pools = [[next(it) for _ in range(n_pages)] for _ in range(4)]
    new = [next(it) for _ in range(6)]
    bkw_ref, bvw_ref, q_ref, gate_ref, slope_ref, wk_ref, wv_ref, o_ref = [next(it) for _ in range(8)]
    kbuf, vbuf = next(it), next(it)
    tok = next(it), next(it)
    nc = past // CMP_STRIDE
    n_sel = past // SEL_BLOCK + 1
    rows = N_HEADS
    slope = slope_ref[...]
    own = _iota((rows, D_KV), 1) // HEAD_DIM == _iota((rows, D_KV), 0) // GQ
    q = jnp.where(own, jnp.concatenate([q_ref[0]] * N_KV, axis=1), 0.0)
    qb = q.astype(BF16)

    def new_row(i):
        return new[i][0].astype(BF16).astype(F32)

    def own_heads(o):
        o = jnp.where(own, o, 0.0)
        return o[:, 0:64] + o[:, 64:128] + o[:, 128:192] + o[:, 192:256]

    cmp = []
    for pages, w_ref, x_new in ((pools[0], wk_ref, new[0]), (pools[1], wv_ref, new[1])):
        for i, p_ref in enumerate(pages):
            pt = p_ref[0].T
            tok[0][i * PAGE:(i + 1) * PAGE, :] = pt[:, 0:128]
            tok[1][i * PAGE:(i + 1) * PAGE, :] = pt[:, 128:256]
        a, b = _pool16(tok, w_ref, past)
        cmp.append((a + _shift_up(b, w_ref[CMP_STRIDE:CMP_STRIDE + 1, :] * x_new[0])).astype(BF16))
    ck, cv = cmp
    dist_c = past - (_iota((1, nc), 1) * CMP_STRIDE + (CMP_BLOCK - 1))
    mask_c = dist_c >= 0
    s = jnp.where(mask_c, _dot_nt(qb, ck) - slope * dist_c.astype(F32), NEG)
    p = jnp.where(mask_c, jnp.exp(s - jnp.max(s, axis=-1, keepdims=True)), 0.0)
    p = p / jnp.maximum(jnp.sum(p, axis=-1, keepdims=True), TINY)
    o_c = own_heads(_dot(p.astype(BF16), cv))
    p_grp = jnp.concatenate(
        [jnp.broadcast_to(jnp.sum(p[h * GQ:(h + 1) * GQ], axis=0, keepdims=True), (GQ, nc)) for h in range(N_KV)], axis=0)
    imp = _dot3(p_grp, _overlap_matrix(nc, 128, nc, n_sel))
    sel = _select_blocks(imp, jnp.full((rows, 1), past, jnp.int32), n_sel)

    def attend(pages, k_new, v_new, mask, new_ok, dist, width):
        for i, (kp, vp) in enumerate(pages):
            kbuf[:, i * kp.shape[1]:(i + 1) * kp.shape[1]] = kp[...].astype(BF16)
            vbuf[:, i * vp.shape[1]:(i + 1) * vp.shape[1]] = vp[...].astype(BF16)
        s = jnp.where(mask, _dot(qb, kbuf[:, 0:width]) - slope * dist.astype(F32), NEG)
        s_new = jnp.where(new_ok, jnp.sum(qb.astype(F32) * k_new, axis=-1, keepdims=True), NEG)
        m = jnp.maximum(jnp.max(s, axis=-1, keepdims=True), s_new)
        p = jnp.where(mask, jnp.exp(s - m), 0.0)
        p_new = jnp.where(new_ok, jnp.exp(s_new - m), 0.0)
        l = jnp.sum(p, axis=-1, keepdims=True) + p_new
        o = _dot_nt(p.astype(BF16), vbuf[:, 0:width]) + p_new.astype(BF16).astype(F32) * v_new
        return own_heads(o) / jnp.maximum(l, TINY)

    dist_s = past - _iota((1, past), 1)
    expand = _iota((128, past), 0) == _iota((128, past), 1) // SEL_BLOCK
    hit = _dot(sel.astype(BF16), jnp.where(expand, 1.0, 0.0).astype(BF16))
    o_s = attend([(k.at[0], v.at[0]) for k, v in zip(pools[2], pools[3])], new_row(2), new_row(3),
                 (hit > 0.5) & (dist_s >= 0), sel[:, n_sel - 1:n_sel] > 0.5, dist_s, past)
    wl = bkw_ref.shape[2]
    dist_w = wl - _iota((1, wl), 1)
    o_w = attend([(bkw_ref.at[0], bvw_ref.at[0])], new_row(4), new_row(5),
                 (dist_w >= 0) & (dist_w < WINDOW), jnp.full((rows, 1), True), dist_w, wl)
    gates = gate_ref[0]
    o_ref[0] = gates[:, 0:1] * o_c + gates[:, 1:2] * o_s + gates[:, 2:3] * o_w


def _nsa_sample(page_table, pools, new_rows, buf_kw, buf_vw, q, gates_t, slopes, wk, wv):
    nb, n_pages = page_table.shape
    past = n_pages * PAGE
    page_specs = []
    for pool in pools:
        for p in range(n_pages):
            page_specs.append(pl.BlockSpec((1,) + pool.shape[1:], lambda b, pt, p=p: (pt[b, p], 0, 0)))
    per_b = lambda shape: pl.BlockSpec((1,) + shape, lambda b, pt: (b,) + (0,) * len(shape))
    in_specs = (page_specs + [per_b((1, D_KV))] * 6 + [per_b(buf_kw.shape[1:])] * 2
                + [per_b((N_HEADS, HEAD_DIM)), per_b((N_HEADS, 3)),
                   pl.BlockSpec(slopes.shape, lambda b, pt: (0, 0)),
                   pl.BlockSpec(wk.shape, lambda b, pt: (0, 0)), pl.BlockSpec(wv.shape, lambda b, pt: (0, 0))])
    args = [pool for pool in pools for _ in range(n_pages)] + list(new_rows) + [buf_kw, buf_vw, q, gates_t, slopes, wk, wv]
    return pl.pallas_call(
        functools.partial(_nsa_sample_kernel, n_pages=n_pages, past=past),
        grid_spec=pltpu.PrefetchScalarGridSpec(
            num_scalar_prefetch=1, grid=(nb,), in_specs=in_specs,
            out_specs=per_b((N_HEADS, HEAD_DIM)),
            scratch_shapes=[pltpu.VMEM((D_KV, past), BF16)] * 2 + [pltpu.VMEM((past, D_KV // 2), F32)] * 2),
        out_shape=jax.ShapeDtypeStruct((nb, N_HEADS, HEAD_DIM), F32),
        compiler_params=_params(("arbitrary",)), name="nsa_sample")(page_table, *args)


def _s5_disc_kernel(lr_ref, li_ref, ldt_ref, ar_ref, ai_ref, fr_ref, fi_ref):
    lr, li = lr_ref[...], li_ref[...]
    dt = jnp.exp(ldt_ref[...])
    mag = jnp.exp(lr * dt)
    ar = mag * jnp.cos(li * dt)
    ai = mag * jnp.sin(li * dt)
    den = lr * lr + li * li
    ar_ref[...] = ar
    ai_ref[...] = ai
    fr_ref[...] = ((ar - 1.0) * lr + ai * li) / den
    fi_ref[...] = (ai * lr - (ar - 1.0) * li) / den


def _s5_bbar_kernel(fr_ref, fi_ref, br_ref, bi_ref, or_ref, oi_ref):
    fr, fi, br, bi = fr_ref[...], fi_ref[...], br_ref[...], bi_ref[...]
    or_ref[...] = fr * br - fi * bi
    oi_ref[...] = fr * bi + fi * br


def _s5_weights(lam_re, lam_im, log_dt, b_re, b_im, c_re, c_im):
    g, n = lam_re.shape
    sd = jax.ShapeDtypeStruct((g, n), F32)
    ar, ai, fr, fi = pl.pallas_call(_s5_disc_kernel, out_shape=[sd] * 4, name="s5_discretise")(
        lam_re, lam_im, log_dt.reshape(g, 1))
    sb = jax.ShapeDtypeStruct((g * n, SSM_GROUP), F32)
    bbr, bbi = pl.pallas_call(_s5_bbar_kernel, out_shape=[sb] * 2, name="s5_bbar")(
        fr.reshape(g * n, 1), fi.reshape(g * n, 1), b_re.reshape(g * n, SSM_GROUP), b_im.reshape(g * n, SSM_GROUP))
    eye = jnp.eye(g // SSM_BANDS, dtype=F32)
    gl = g // SSM_BANDS

    def band_in(bb):
        x = bb.reshape(SSM_BANDS, gl, n, SSM_GROUP).transpose(0, 1, 3, 2)
        return jnp.einsum("jgpn,gh->jgphn", x, eye).reshape(SSM_BANDS, gl * SSM_GROUP, gl * n).astype(BF16)

    def band_out(c):
        x = c.reshape(SSM_BANDS, gl, SSM_GROUP, n).transpose(0, 1, 3, 2)
        return jnp.einsum("jgnp,gh->jgnhp", x, eye).reshape(SSM_BANDS, gl * n, gl * SSM_GROUP).astype(BF16)

    return (ar.reshape(1, g * n), ai.reshape(1, g * n), band_in(bbr), band_in(bbi), band_out(c_re), band_out(-c_im))


def _s5_prompt_kernel(u_ref, wbr_ref, wbi_ref, ar_ref, ai_ref, wcr_ref, wci_ref, d_ref,
                      y_ref, sr_ref, si_ref, hr_s, hi_s, cr_s, ci_s, *, nb):
    c = pl.program_id(1)
    rows, width = hr_s.shape
    rep = 8 // nb

    @pl.when(c == 0)
    def _():
        cr_s[...] = jnp.zeros_like(cr_s)
        ci_s[...] = jnp.zeros_like(ci_s)

    u = u_ref[...]
    ub = u.astype(BF16)
    hr_s[...] = _dot(ub, wbr_ref[0])
    hi_s[...] = _dot(ub, wbi_ref[0])
    ar = jnp.broadcast_to(ar_ref[...], (8, width))
    ai = jnp.broadcast_to(ai_ref[...], (8, width))
    sub = _iota((8, width), 0) // nb

    def step(i, carry):
        sr, si = carry
        base = pl.multiple_of(i * 8, 8)
        xr = hr_s[pl.ds(base, 8), :]
        xi = hi_s[pl.ds(base, 8), :]
        outr, outi = xr, xi
        for k in range(rep):
            yr = ar * sr - ai * si + xr
            yi = ar * si + ai * sr + xi
            keep = sub == k
            outr = jnp.where(keep, yr, outr)
            outi = jnp.where(keep, yi, outi)
            zr = jnp.where(keep, yr, 0.0)
            zi = jnp.where(keep, yi, 0.0)
            sr, si = zr, zi
            for sh in range(1, rep):
                sr = sr + pltpu.roll(zr, sh * nb, 0)
                si = si + pltpu.roll(zi, sh * nb, 0)
        hr_s[pl.ds(base, 8), :] = outr
        hi_s[pl.ds(base, 8), :] = outi
        return sr, si

    sr, si = lax.fori_loop(0, rows // 8, step, (cr_s[...], ci_s[...]))
    cr_s[...] = sr
    ci_s[...] = si
    y = _dot(hr_s[...].astype(BF16), wcr_ref[0]) + _dot(hi_s[...].astype(BF16), wci_ref[0]) + d_ref[...] * u
    y_ref[...] = jax.nn.gelu(y)

    @pl.when(c == pl.num_programs(1) - 1)
    def _():
        sr_ref[...] = sr[0:nb]
        si_ref[...] = si[0:nb]


def _s5_prompt(u_tm, weights, d_skip, nb, tc):
    ar, ai, wbr, wbi, wcr, wci = weights
    rows = u_tm.shape[0]
    cw = D_SSM // SSM_BANDS
    sw = ar.shape[1] // SSM_BANDS
    blk = tc * nb
    tile = pl.BlockSpec((blk, cw), lambda j, c: (c, j))
    band = lambda a: pl.BlockSpec((1,) + a.shape[1:], lambda j, c: (j, 0, 0))
    vec = lambda w: pl.BlockSpec((1, w), lambda j, c: (0, j))
    state = pl.BlockSpec((nb, sw), lambda j, c: (0, j))
    return pl.pallas_call(
        functools.partial(_s5_prompt_kernel, nb=nb), grid=(SSM_BANDS, rows // blk),
        in_specs=[tile, band(wbr), band(wbi), vec(sw), vec(sw), band(wcr), band(wci), vec(cw)],
        out_specs=[tile, state, state],
        out_shape=[jax.ShapeDtypeStruct((rows, D_SSM), F32)] + [jax.ShapeDtypeStruct((nb, ar.shape[1]), F32)] * 2,
        scratch_shapes=[pltpu.VMEM((blk, sw), F32)] * 2 + [pltpu.VMEM((8, sw), F32)] * 2,
        compiler_params=_params(("parallel", "arbitrary")), name="s5_prompt")(
            u_tm, wbr, wbi, ar, ai, wcr, wci, d_skip)


def _s5_sample_kernel(u_ref, h0r_ref, h0i_ref, wbr_ref, wbi_ref, ar_ref, ai_ref, wcr_ref, wci_ref, d_ref,
                      y_ref, sr_ref, si_ref):
    cw = D_SSM // SSM_BANDS
    sw = ar_ref.shape[1] // SSM_BANDS
    for j in range(SSM_BANDS):
        cs = slice(j * cw, (j + 1) * cw)
        ss = slice(j * sw, (j + 1) * sw)
        u = u_ref[:, cs]
        ub = u.astype(BF16)
        ar, ai = ar_ref[:, ss], ai_ref[:, ss]
        h0r, h0i = h0r_ref[:, ss], h0i_ref[:, ss]
        hr = _dot(ub, wbr_ref[j]) + (ar * h0r - ai * h0i)
        hi = _dot(ub, wbi_ref[j]) + (ar * h0i + ai * h0r)
        sr_ref[:, ss] = hr
        si_ref[:, ss] = hi
        y = _dot(hr.astype(BF16), wcr_ref[j]) + _dot(hi.astype(BF16), wci_ref[j]) + d_ref[:, cs] * u
        y_ref[:, cs] = jax.nn.gelu(y)


def _s5_sample(u, h0r, h0i, weights, d_skip):
    ar, ai, wbr, wbi, wcr, wci = weights
    nb = u.shape[0]
    st = jax.ShapeDtypeStruct(h0r.shape, F32)
    return pl.pallas_call(
        _s5_sample_kernel, out_shape=[jax.ShapeDtypeStruct((nb, D_SSM), F32), st, st],
        compiler_params=pltpu.CompilerParams(vmem_limit_bytes=VMEM_LIMIT), name="s5_sample")(
            u, h0r, h0i, wbr, wbi, ar, ai, wcr, wci, d_skip)


def _xattn_prompt_kernel(q_ref, k_ref, v_ref, o_ref):
    scale = X_HEAD_DIM ** -0.5
    for h in range(X_HEADS):
        cs = slice(h * X_HEAD_DIM, (h + 1) * X_HEAD_DIM)
        s = _dot_nt(q_ref[:, cs], k_ref[:, cs].astype(BF16)) * scale
        p = jnp.exp(s - jnp.max(s, axis=-1, keepdims=True))
        p = p / jnp.sum(p, axis=-1, keepdims=True)
        o_ref[:, cs] = _dot(p.astype(BF16), v_ref[:, cs].astype(BF16)).astype(o_ref.dtype)


def _xattn_prompt(q, mk, mv, nb, tm):
    m = q.shape[0]
    nt = m // nb // tm
    mem = pl.BlockSpec((MEM_LEN, D_MODEL), lambda b, i: (b, 0))
    row = pl.BlockSpec((tm, D_MODEL), lambda b, i: (b * nt + i, 0))
    return pl.pallas_call(
        _xattn_prompt_kernel, grid=(nb, nt), in_specs=[row, mem, mem], out_specs=row,
        out_shape=jax.ShapeDtypeStruct((m, D_MODEL), BF16),
        compiler_params=_params(("parallel", "parallel")), name="xattn_prompt")(q, mk, mv)


def _xattn_sample_kernel(q_ref, k_ref, v_ref, o_ref):
    scale = X_HEAD_DIM ** -0.5
    nt = X_HEAD_DIM // 128
    rows = MEM_LEN * nt * X_HEADS

    def head(ref, h):
        parts = [ref[0, pl.ds(t * X_HEADS + h, MEM_LEN, stride=nt * X_HEADS), :] for t in range(nt)]
        return jnp.concatenate(parts, axis=1).astype(BF16)

    assert k_ref.shape[1] == rows
    for h in range(X_HEADS):
        cs = slice(h * X_HEAD_DIM, (h + 1) * X_HEAD_DIM)
        q = jnp.broadcast_to(q_ref[0, :, cs], (8, X_HEAD_DIM)).astype(BF16)
        s = _dot_nt(q, head(k_ref, h)) * scale
        p = jnp.exp(s - jnp.max(s, axis=-1, keepdims=True))
        p = p / jnp.sum(p, axis=-1, keepdims=True)
        o_ref[0, :, cs] = _dot(p.astype(BF16), head(v_ref, h))[0:1].astype(o_ref.dtype)


def _xattn_sample(q, mk, mv):
    nb = q.shape[0]
    mem = pl.BlockSpec((1,) + mk.shape[1:], lambda b: (b, 0, 0))
    row = pl.BlockSpec((1, 1, D_MODEL), lambda b: (b, 0, 0))
    return pl.pallas_call(
        _xattn_sample_kernel, grid=(nb,), in_specs=[row, mem, mem], out_specs=row,
        out_shape=jax.ShapeDtypeStruct((nb, 1, D_MODEL), BF16),
        compiler_params=_params(("parallel",)), name="xattn_sample")(q, mk, mv)


def _top_distinct(x, n):
    rows = []
    for _ in range(n):
        m = jnp.max(x, axis=0, keepdims=True)
        rows.append(m)
        x = jnp.where(x == m, LOWEST, x)
    return jnp.concatenate(rows, axis=0)


def _peer_route_kernel(x_ref, g_ref, w_ref, k1_ref, k2_ref, xt_ref, thr_ref, s2_ref, e2_ref, c1_ref):
    xn = _rms(x_ref[...], g_ref[...])
    xt_ref[...] = xn.T.astype(BF16)
    q = _dot(xn.astype(BF16), w_ref[...]).astype(BF16)
    half = PEER_KEYS
    for h in range(PEER_HEADS):
        s1 = _dot_nt(k1_ref[...], q[:, 2 * half * h:2 * half * h + half])
        s2 = _dot_nt(k2_ref[...], q[:, 2 * half * h + half:2 * half * (h + 1)])
        d1 = _top_distinct(s1, PEER_TOPK)
        d2 = _top_distinct(s2, PEER_TOPK)
        cand = jnp.concatenate([d1[a:a + 1] + d2[0:PEER_TOPK // (a + 1)] for a in range(PEER_TOPK)], axis=0)
        tau = _top_distinct(cand, PEER_TOPK)[PEER_TOPK - 1:PEER_TOPK]
        top = d1[0:1] + d2[0:1]
        z = jnp.sum(jnp.where(cand >= tau, jnp.exp(cand - top), 0.0), axis=0, keepdims=True)
        thr = jnp.full(s1.shape, HUGE, F32)
        for a in range(PEER_TOPK):
            ok = (d1[a:a + 1] + d2) >= tau
            thr_a = jnp.min(jnp.where(ok, d2, HUGE), axis=0, keepdims=True)
            thr = jnp.where(s1 == d1[a:a + 1], thr_a, thr)
        thr_ref[h] = thr
        s2_ref[h] = s2
        e2_ref[h] = jnp.exp(s2 - d2[0:1])
        c1_ref[h] = jnp.exp(s1 - d1[0:1]) / z


def _peer_route(x, g, w_pq, k1, k2, tm):
    m = x.shape[0]
    hk = pl.BlockSpec((PEER_HEADS, PEER_KEYS, tm), lambda i: (0, 0, i))
    hks = jax.ShapeDtypeStruct((PEER_HEADS, PEER_KEYS, m), F32)
    return pl.pallas_call(
        _peer_route_kernel, grid=(m // tm,),
        in_specs=[pl.BlockSpec((tm, D_MODEL), lambda i: (i, 0)), _full(g), _full(w_pq), _full(k1), _full(k2)],
        out_specs=[pl.BlockSpec((D_MODEL, tm), lambda i: (0, i)), hk, hk, hk, hk],
        out_shape=[jax.ShapeDtypeStruct((D_MODEL, m), BF16), hks, hks, hks, hks],
        compiler_params=_params(("parallel",)), name="peer_route")(x, g, w_pq, k1, k2)


def _peer_kernel(xt_ref, u_ref, vt_ref, thr_ref, s2_ref, e2_ref, c1_ref, o_ref, wa_s, wb_s, act_s):
    e = pl.program_id(1)
    et, tt = act_s.shape
    n_sub = et // PEER_KEYS
    group = 2
    oc = o_ref.shape[0] // n_sub

    @pl.when(e == 0)
    def _():
        o_ref[...] = jnp.zeros_like(o_ref)
        wb_s[...] = jnp.zeros_like(wb_s)

    def step(cur_s, prev_s):
        def score(c):
            rs = slice(c * SCORE_ROWS, (c + 1) * SCORE_ROWS)
            act_s[rs, :] = _dot(u_ref[rs, :], xt_ref[...])

        def emit(c):
            rs = slice(c * EMIT_ROWS, (c + 1) * EMIT_ROWS)
            o_ref[rs, :] += _dot(vt_ref[rs, :], prev_s[...])

        def gate_group(j0, lt):
            ls = slice(lt * 128, (lt + 1) * 128)
            for part in range(PEER_KEYS // GATE_ROWS):
                ks = slice(part * GATE_ROWS, (part + 1) * GATE_ROWS)
                gates = [None] * group
                for h in range(PEER_HEADS):
                    s2 = s2_ref[h, ks, ls]
                    e2 = e2_ref[h, ks, ls]
                    for k in range(group):
                        jj = j0 + k
                        g = jnp.where(s2 >= thr_ref[h, jj:jj + 1, ls], e2, 0.0) * c1_ref[h, jj:jj + 1, ls]
                        gates[k] = g if gates[k] is None else gates[k] + g
                for k in range(group):
                    r0 = (j0 + k) * PEER_KEYS + part * GATE_ROWS
                    rs = slice(r0, r0 + GATE_ROWS)
                    cur_s[rs, ls] = (gates[k] * jax.nn.gelu(act_s[rs, ls])).astype(BF16)

        scores = list(range(et // SCORE_ROWS))
        emits = list(range(o_ref.shape[0] // EMIT_ROWS))
        for j0 in range(0, n_sub, group):
            while scores and scores[0] * SCORE_ROWS < (j0 + group) * PEER_KEYS:
                score(scores.pop(0))
            for lt in range(tt // 128):
                gate_group(j0, lt)
                if scores:
                    score(scores.pop(0))
                elif emits:
                    emit(emits.pop(0))
        for c in emits:
            emit(c)

    @pl.when(e % 2 == 0)
    def _():
        step(wa_s, wb_s)

    @pl.when(e % 2 == 1)
    def _():
        step(wb_s, wa_s)


def _peer(xt, u, vt, thr, s2, e2, c1, tt, et):
    m = xt.shape[1]
    n_e = u.shape[0] // et
    tok3 = lambda r: pl.BlockSpec((PEER_HEADS, r, tt), lambda i, e: (0, 0, i))
    tile3 = pl.BlockSpec((PEER_HEADS, et // PEER_KEYS, tt), lambda i, e: (0, jnp.minimum(e, n_e - 1), i))
    return pl.pallas_call(
        _peer_kernel, grid=(m // tt, n_e + 1),
        in_specs=[pl.BlockSpec((D_MODEL, tt), lambda i, e: (0, i)),
                  pl.BlockSpec((et, D_MODEL), lambda i, e: (jnp.minimum(e, n_e - 1), 0)),
                  pl.BlockSpec((D_MODEL, et), lambda i, e: (0, jnp.maximum(e - 1, 0))),
                  tile3, tok3(PEER_KEYS), tok3(PEER_KEYS), tile3],
        out_specs=pl.BlockSpec((D_MODEL, tt), lambda i, e: (0, i)),
        out_shape=jax.ShapeDtypeStruct((D_MODEL, m), F32),
        scratch_shapes=[pltpu.VMEM((et, tt), BF16)] * 2 + [pltpu.VMEM((et, tt), F32)],
        compiler_params=_params(("parallel", "arbitrary")), name="peer_dense")(
            xt, u, vt, thr, s2, e2, c1)


def _final_kernel(h_ref, ot_ref, g_ref, y_ref):
    y_ref[...] = _rms(h_ref[...] + ot_ref[...].T, g_ref[...])


def _final_norm(h, out_t, g, tm):
    m = h.shape[0]
    row = pl.BlockSpec((tm, D_MODEL), lambda i: (i, 0))
    return pl.pallas_call(
        _final_kernel, grid=(m // tm,),
        in_specs=[row, pl.BlockSpec((D_MODEL, tm), lambda i: (0, i)), _full(g)], out_specs=row,
        out_shape=jax.ShapeDtypeStruct((m, D_MODEL), F32),
        compiler_params=_params(("parallel",)), name="final_norm")(h, out_t, g)


def _row(v):
    return v.reshape(1, -1).astype(F32)


def _cmp_rows(w):
    return jnp.repeat(w.T.astype(F32), HEAD_DIM, axis=1)


def kernel(x_prompt, x_sample, mem_prompt, cache_k_cmp, cache_v_cmp, cache_k_sel, cache_v_sel, cache_k_win, cache_v_win, state_s5_re, state_s5_im, cache_mem_k, cache_mem_v, page_table, g_mix, w_in, w_cmp_k, w_cmp_v, lam_re, lam_im, log_dt, b_re, b_im, c_re, c_im, d_skip, w_glu, g_attn_out, g_ssm_out, w_out, g_x, g_mem, w_xq, w_xk, w_xv, w_xo, g_ffn, w_pq, peer_k1, peer_k2, peer_u, peer_v, g_final):
    nb, seq, _ = x_prompt.shape
    db = x_sample.shape[0]
    depth = g_mix.shape[0]
    assert depth == 1 and x_sample.shape[1] == 1
    l = 0
    mp = nb * seq
    n_gate = 3 * N_HEADS
    kv_end = D_ATTN + 6 * D_KV

    w_attn = jnp.concatenate([w_in[l][:, :kv_end + n_gate], jnp.zeros((D_MODEL, GATE_PAD - n_gate), F32)], axis=1).astype(BF16)
    w_u = w_in[l][:, kv_end + n_gate:].astype(BF16)
    w_full = jnp.concatenate([w_attn, w_u], axis=1)
    w_glu_b = w_glu[l].astype(BF16)
    w_out_a = w_out[l][:D_ATTN].astype(BF16)
    w_out_s = w_out[l][D_ATTN:].astype(BF16)
    w_mem = jnp.concatenate([w_xk[l], w_xv[l]], axis=1).astype(BF16)
    w_xq_b = w_xq[l].astype(BF16)
    w_xo_b = w_xo[l].astype(BF16)
    w_pq_b = w_pq[l].astype(BF16)
    k1_b = peer_k1[l].astype(BF16)
    k2_b = peer_k2[l].astype(BF16)
    u_b = peer_u[l].astype(BF16)
    vt_b = peer_v[l].T.astype(BF16)
    wk_rows = _cmp_rows(w_cmp_k[l])
    wv_rows = _cmp_rows(w_cmp_v[l])
    gm, gx, gf, gfin = _row(g_mix[l]), _row(g_x[l]), _row(g_ffn[l]), _row(g_final)
    ga, gs, gme = _row(g_attn_out[l]), _row(g_ssm_out[l]), _row(g_mem[l])
    s5w = _s5_weights(lam_re[l], lam_im[l], log_dt[l], b_re[l], b_im[l], c_re[l], c_im[l])
    dsk = _row(d_skip[l])

    xp = x_prompt.reshape(mp, D_MODEL)
    q_hm, kv, kv_hm, gates = _proj_attn(xp, gm, w_attn, tm=256)
    u_tm = _proj_u(xp, gm, w_u, nb, tm=256).reshape(seq * nb, D_SSM)
    ck, cv = _compress(kv[0], kv[1], wk_rows, wv_rows, nb)
    o_a = _nsa_prompt(q_hm, ck, cv, kv_hm[2], kv_hm[3], kv_hm[4], kv_hm[5], gates, nb)
    y_tm, p_sr, p_si = _s5_prompt(u_tm, s5w, dsk, nb, tc=256)
    o_s = _glu(y_tm.reshape(seq, nb * D_SSM), w_glu_b, nb, tm=256)
    h1 = _merge(o_a, o_s, ga, gs, w_out_a, w_out_s, xp, tm=256)
    mk, mv = _mem_kv(mem_prompt.reshape(nb * MEM_LEN, D_MODEL), gme, w_mem, tm=256)
    xq = _mm(h1, w_xq_b, 256, BF16, g=gx, name="xattn_q")
    xo = _xattn_prompt(xq, mk, mv, nb, tm=256)
    h2 = _mm(xo, w_xo_b, 256, F32, res=h1, name="xattn_o")
    routed = _peer_route(h2, gf, w_pq_b, k1_b, k2_b, tm=256)
    y_p = _final_norm(h2, _peer(routed[0], u_b, vt_b, *routed[1:], tt=512, et=1024), gfin, tm=256)

    xs = x_sample.reshape(db, D_MODEL)
    z = _mm(xs, w_full, db, F32, g=gm, gate_cols=(kv_end, kv_end + n_gate), name="proj_sample")
    new_rows = [z[:, D_ATTN + D_KV * i:D_ATTN + D_KV * (i + 1)] for i in range(6)]
    n_phys = cache_k_cmp.shape[1]
    pools = [c[l].transpose(0, 2, 3, 1).reshape(n_phys, D_KV, PAGE)
             for c in (cache_k_cmp, cache_v_cmp, cache_k_sel, cache_v_sel)]
    wl = cache_k_win.shape[2]
    buf_kw = cache_k_win[l].transpose(0, 2, 3, 1).reshape(db, D_KV, wl)
    buf_vw = cache_v_win[l].transpose(0, 2, 3, 1).reshape(db, D_KV, wl)
    q_s = (z[:, :D_ATTN] * (HEAD_DIM ** -0.5)).reshape(db, N_HEADS, HEAD_DIM)
    gates_t = z[:, kv_end:kv_end + n_gate].reshape(db, 3, N_HEADS).transpose(0, 2, 1)
    slopes = jnp.asarray(np.array([[_alibi_slope(i)] for i in range(N_HEADS)], np.float32))
    o_a_s = _nsa_sample(page_table, pools, [r.reshape(db, 1, D_KV) for r in new_rows], buf_kw, buf_vw,
                        q_s, gates_t, slopes, wk_rows, wv_rows).reshape(db, D_ATTN)
    y_s, s_sr, s_si = _s5_sample(z[:, kv_end + GATE_PAD:], state_s5_re[l].reshape(db, -1), state_s5_im[l].reshape(db, -1), s5w, dsk)
    o_s_s = _glu(y_s, w_glu_b, 1, tm=db)
    h1s = _merge(o_a_s, o_s_s, ga, gs, w_out_a, w_out_s, xs, tm=db)
    xq_s = _mm(h1s, w_xq_b, db, F32, g=gx, name="xattn_q_sample")
    nt = X_HEAD_DIM // 128

    def mem_rows(c):
        return c.reshape(db, MEM_LEN, X_HEADS, nt, 128).transpose(0, 1, 3, 2, 4).reshape(db, MEM_LEN * nt * X_HEADS, 128)

    xo_s = _xattn_sample(xq_s.reshape(db, 1, D_MODEL), mem_rows(cache_mem_k[l]), mem_rows(cache_mem_v[l])).reshape(db, D_MODEL)
    h2s = _mm(xo_s, w_xo_b, db, F32, res=h1s, name="xattn_o_sample")
    routed_s = _peer_route(h2s, gf, w_pq_b, k1_b, k2_b, tm=db)
    y_s_out = _final_norm(h2s, _peer(routed_s[0], u_b, vt_b, *routed_s[1:], tt=db, et=1024), gfin, tm=db)

    kvshape = (1, nb, seq, N_KV, HEAD_DIM)
    wlp = min(WINDOW, seq)
    p_kv = [a.reshape(kvshape) for a in kv]
    p_win = [a[:, :, seq - wlp:] for a in p_kv[4:6]]
    s_new = [r.reshape(1, db, 1, N_KV, HEAD_DIM) for r in new_rows]
    s_win = [jnp.concatenate([c[l], n[0]], axis=1)[None, :, -min(WINDOW, wl + 1):] for c, n in ((cache_k_win, s_new[4]), (cache_v_win, s_new[5]))]
    g64 = (1, -1, N_SSM_GROUPS, SSM_STATE)
    return (y_p.reshape(nb, seq, D_MODEL), y_s_out.reshape(db, 1, D_MODEL),
            p_kv[0], p_kv[1], p_kv[2], p_kv[3], p_win[0], p_win[1],
            p_sr.reshape(g64), p_si.reshape(g64),
            mk.reshape(1, nb, MEM_LEN, X_HEADS, X_HEAD_DIM), mv.reshape(1, nb, MEM_LEN, X_HEADS, X_HEAD_DIM),
            s_new[0], s_new[1], s_new[2], s_new[3], s_win[0], s_win[1],
            s_sr.reshape(g64), s_si.reshape(g64))
```

```python
import functools
import math

import jax
import jax.numpy as jnp
import numpy as np
from jax import lax
from jax.experimental import pallas as pl
from jax.experimental.pallas import tpu as pltpu

F32 = jnp.float32
BF16 = jnp.bfloat16

D_MODEL = 2048
N_HEADS = 16
HEAD_DIM = 64
N_KV = 4
GQ = 4
D_ATTN = 1024
D_SSM = 1024
D_KV = 256
CMP_BLOCK = 32
CMP_STRIDE = 16
SEL_BLOCK = 64
N_SEL = 8
WINDOW = 512
Q_BLOCK = 128
VT_ROWS = 80
HEADS_PER_LOOP = 4
FORCE = 1e4
N_SSM_GROUPS = 64
SSM_GROUP = 16
SSM_STATE = 64
SSM_BANDS = 4
MEM_LEN = 256
X_HEADS = 4
X_HEAD_DIM = 512
PEER_KEYS = 128
PEER_HEADS = 8
PEER_TOPK = 16
GATE_ROWS = 32
SCORE_ROWS = 128
EMIT_ROWS = 256
PAGE = 128
EPS = 1e-6
NEG = -1e30
TINY = 1e-30
LOWEST = -3.0e38
HUGE = 3.0e38
GATE_PAD = 128
VMEM_LIMIT = 56 * 2**20


def _params(sem, flags=None):
    return pltpu.CompilerParams(dimension_semantics=sem, vmem_limit_bytes=VMEM_LIMIT, flags=flags)


def _full(a):
    nd = a.ndim
    return pl.BlockSpec(a.shape, lambda *_: (0,) * nd)


def _rms(x, g):
    return x * lax.rsqrt(jnp.mean(x * x, axis=-1, keepdims=True) + EPS) * g


def _dot(a, b):
    return jnp.dot(a, b, preferred_element_type=F32)


def _dot_nt(a, b):
    return lax.dot_general(a, b, (((1,), (1,)), ((), ())), preferred_element_type=F32)


def _dot3(a, b_exact):
    hi = a.astype(BF16)
    r1 = a - hi.astype(F32)
    mid = r1.astype(BF16)
    lo = (r1 - mid.astype(F32)).astype(BF16)
    return _dot(hi, b_exact) + _dot(mid, b_exact) + _dot(lo, b_exact)


def _iota(shape, dim):
    return lax.broadcasted_iota(jnp.int32, shape, dim)


def _proj_attn_kernel(x_ref, g_ref, w_ref, q_ref, *rest):
    kv_refs, kh_refs, gate_ref, vt_refs = rest[:6], rest[6:8], rest[8], rest[9:11]
    xn = _rms(x_ref[...], g_ref[...]).astype(BF16)
    z = _dot(xn, w_ref[...])
    for hd in range(N_HEADS):
        q_ref[hd] = (z[:, hd * HEAD_DIM:(hd + 1) * HEAD_DIM] * (HEAD_DIM ** -0.5)).astype(BF16)
    for k in range(6):
        kv_refs[k][...] = z[:, D_ATTN + D_KV * k:D_ATTN + D_KV * (k + 1)]
    for kh_ref, k in zip(kh_refs, (2, 4)):
        for h in range(N_KV):
            c0 = D_ATTN + D_KV * k + h * HEAD_DIM
            kh_ref[h] = z[:, c0:c0 + HEAD_DIM].astype(BF16)
    for vt_ref, k in zip(vt_refs, (3, 5)):
        zt = z[:, D_ATTN + D_KV * k:D_ATTN + D_KV * (k + 1)].T.astype(BF16)
        for h in range(N_KV):
            vt_ref[h, 0, 0:HEAD_DIM, :] = zt[h * HEAD_DIM:(h + 1) * HEAD_DIM]
            vt_ref[h, 0, HEAD_DIM:, :] = jnp.ones((VT_ROWS - HEAD_DIM, zt.shape[1]), BF16)
    gate_ref[...] = jax.nn.sigmoid(z[:, D_ATTN + 6 * D_KV:])


def _proj_attn(x, g, w, tm):
    m = x.shape[0]
    row = lambda n: pl.BlockSpec((tm, n), lambda i: (i, 0))
    hm = lambda n: pl.BlockSpec((n, tm, HEAD_DIM), lambda i: (0, i, 0))
    out_shape = ([jax.ShapeDtypeStruct((N_HEADS, m, HEAD_DIM), BF16)]
                 + [jax.ShapeDtypeStruct((m, D_KV), F32)] * 6
                 + [jax.ShapeDtypeStruct((N_KV, m, HEAD_DIM), BF16)] * 2
                 + [jax.ShapeDtypeStruct((m, GATE_PAD), F32)]
                 + [jax.ShapeDtypeStruct((N_KV, m // tm, VT_ROWS, tm), BF16)] * 2)
    vt = pl.BlockSpec((N_KV, 1, VT_ROWS, tm), lambda i: (0, i, 0, 0))
    out_specs = [hm(N_HEADS)] + [row(D_KV)] * 6 + [hm(N_KV)] * 2 + [row(GATE_PAD)] + [vt] * 2
    outs = pl.pallas_call(
        _proj_attn_kernel, grid=(m // tm,),
        in_specs=[row(D_MODEL), _full(g), _full(w)],
        out_specs=out_specs, out_shape=out_shape,
        compiler_params=_params(("parallel",)), name="proj_attn")(x, g, w)
    return outs[0], outs[1:7], outs[7:9], outs[9], outs[10:12]


def _proj_u_kernel(x_ref, g_ref, w_ref, u_ref):
    xn = _rms(x_ref[...], g_ref[...]).astype(BF16)
    u_ref[...] = _dot(xn, w_ref[...])


def _proj_u(x, g, w, nb, tm):
    m = x.shape[0]
    t = m // nb
    nt = t // tm
    return pl.pallas_call(
        _proj_u_kernel, grid=(nb, nt),
        in_specs=[pl.BlockSpec((tm, D_MODEL), lambda b, i: (b * nt + i, 0)), _full(g), _full(w)],
        out_specs=pl.BlockSpec((tm, D_SSM), lambda b, i: (i, b)),
        out_shape=jax.ShapeDtypeStruct((t, nb * D_SSM), F32),
        compiler_params=_params(("parallel", "parallel")), name="proj_u")(x, g, w)


def _mm_kernel(*refs, norm, res, gate_cols):
    it = iter(refs)
    x_ref = next(it)
    g_ref = next(it) if norm else None
    w_ref = next(it)
    r_ref = next(it) if res else None
    o_ref = next(it)
    x = x_ref[...]
    if norm:
        x = _rms(x.astype(F32), g_ref[...])
    z = _dot(x.astype(BF16), w_ref[...])
    if res:
        z = z + r_ref[...]
    if gate_cols is not None:
        col = _iota(z.shape, 1)
        z = jnp.where((col >= gate_cols[0]) & (col < gate_cols[1]), jax.nn.sigmoid(z), z)
    o_ref[...] = z.astype(o_ref.dtype)


def _mm(x, w, tm, out_dtype, g=None, res=None, gate_cols=None, name="mm"):
    m, k = x.shape
    n = w.shape[1]
    row = lambda c: pl.BlockSpec((tm, c), lambda i: (i, 0))
    args, specs = [x], [row(k)]
    if g is not None:
        args.append(g)
        specs.append(_full(g))
    args.append(w)
    specs.append(_full(w))
    if res is not None:
        args.append(res)
        specs.append(row(n))
    return pl.pallas_call(
        functools.partial(_mm_kernel, norm=g is not None, res=res is not None, gate_cols=gate_cols),
        grid=(m // tm,), in_specs=specs, out_specs=row(n),
        out_shape=jax.ShapeDtypeStruct((m, n), out_dtype),
        compiler_params=_params(("parallel",)), name=name)(*args)


def _mm2_kernel(x_ref, g_ref, w_ref, o0_ref, o1_ref):
    xn = _rms(x_ref[...], g_ref[...]).astype(BF16)
    z = _dot(xn, w_ref[...])
    n = o0_ref.shape[1]
    o0_ref[...] = z[:, :n]
    o1_ref[...] = z[:, n:]


def _mem_kv(mem, g, w, tm):
    m = mem.shape[0]
    row = pl.BlockSpec((tm, D_MODEL), lambda i: (i, 0))
    return pl.pallas_call(
        _mm2_kernel, grid=(m // tm,), in_specs=[row, _full(g), _full(w)], out_specs=[row, row],
        out_shape=[jax.ShapeDtypeStruct((m, D_MODEL), F32)] * 2,
        compiler_params=_params(("parallel",)), name="mem_kv")(mem, g, w)


def _glu_kernel(y_ref, w_ref, o_ref):
    y = y_ref[...]
    o_ref[...] = y * jax.nn.sigmoid(_dot(y.astype(BF16), w_ref[...]))


def _glu(y_tm, w, nb, tm):
    t = y_tm.shape[0]
    nt = t // tm
    return pl.pallas_call(
        _glu_kernel, grid=(nb, nt),
        in_specs=[pl.BlockSpec((tm, D_SSM), lambda b, i: (i, b)), _full(w)],
        out_specs=pl.BlockSpec((tm, D_SSM), lambda b, i: (b * nt + i, 0)),
        out_shape=jax.ShapeDtypeStruct((nb * t, D_SSM), F32),
        compiler_params=_params(("parallel", "parallel")), name="glu")(y_tm, w)


def _merge_kernel(oa_ref, os_ref, ga_ref, gs_ref, wa_ref, ws_ref, x_ref, o_ref):
    a = _rms(oa_ref[...], ga_ref[...]).astype(BF16)
    s = _rms(os_ref[...], gs_ref[...]).astype(BF16)
    o_ref[...] = x_ref[...] + (_dot(a, wa_ref[...]) + _dot(s, ws_ref[...]))


def _merge(o_a, o_s, g_a, g_s, w_a, w_s, x, tm):
    m = x.shape[0]
    half = pl.BlockSpec((tm, D_ATTN), lambda i: (i, 0))
    row = pl.BlockSpec((tm, D_MODEL), lambda i: (i, 0))
    return pl.pallas_call(
        _merge_kernel, grid=(m // tm,),
        in_specs=[half, half, _full(g_a), _full(g_s), _full(w_a), _full(w_s), row],
        out_specs=row, out_shape=jax.ShapeDtypeStruct((m, D_MODEL), F32),
        compiler_params=_params(("parallel",)), name="merge_heads")(o_a, o_s, g_a, g_s, w_a, w_s, x)


def _pool16(x_refs, w_ref, tokens):
    halves = []
    for hf, x_ref in enumerate(x_refs):
        lanes = slice(hf * 128, (hf + 1) * 128)
        a = b = None
        for j in range(CMP_STRIDE):
            xj = x_ref[pl.ds(j, tokens // CMP_STRIDE, stride=CMP_STRIDE), :]
            ta = xj * w_ref[j:j + 1, lanes]
            tb = xj * w_ref[CMP_STRIDE + j:CMP_STRIDE + j + 1, lanes]
            a = ta if a is None else a + ta
            b = tb if b is None else b + tb
        halves.append((a, b))
    return (jnp.concatenate([halves[0][0], halves[1][0]], axis=1),
            jnp.concatenate([halves[0][1], halves[1][1]], axis=1))


def _shift_up(b, last_row):
    n = b.shape[0]
    rolled = pltpu.roll(b, n - 1, 0)
    return jnp.where(_iota(b.shape, 0) == n - 1, last_row, rolled)


def _compress_kernel(k0_ref, k1_ref, v0_ref, v1_ref, wk_ref, wv_ref, ck_ref, cv_ref):
    tokens = k0_ref.shape[0]
    a, b = _pool16((k0_ref, k1_ref), wk_ref, tokens)
    c = (a + _shift_up(b, 0.0)).astype(BF16)
    for h in range(N_KV):
        ck_ref[0, h] = c[:, h * HEAD_DIM:(h + 1) * HEAD_DIM]
    a, b = _pool16((v0_ref, v1_ref), wv_ref, tokens)
    ct = (a + _shift_up(b, 0.0)).T.astype(BF16)
    for h in range(N_KV):
        cv_ref[0, h] = ct[h * HEAD_DIM:(h + 1) * HEAD_DIM]


def _compress(kc, vc, wk, wv, nb):
    s = kc.shape[0] // nb
    nc = s // CMP_STRIDE
    lo = pl.BlockSpec((s, D_KV // 2), lambda b: (b, 0))
    hi = pl.BlockSpec((s, D_KV // 2), lambda b: (b, 1))
    out = pl.BlockSpec((1, N_KV, nc, HEAD_DIM), lambda b: (b, 0, 0, 0))
    out_t = pl.BlockSpec((1, N_KV, HEAD_DIM, nc), lambda b: (b, 0, 0, 0))
    return pl.pallas_call(
        _compress_kernel, grid=(nb,), in_specs=[lo, hi, lo, hi, _full(wk), _full(wv)], out_specs=[out, out_t],
        out_shape=[jax.ShapeDtypeStruct((nb, N_KV, nc, HEAD_DIM), BF16),
                   jax.ShapeDtypeStruct((nb, N_KV, HEAD_DIM, nc), BF16)],
        compiler_params=_params(("parallel",)), name="nsa_compress")(kc, kc, vc, vc, wk, wv)


def _overlap_matrix(nc, width, n_cmp, n_sel):
    i = _iota((nc, width), 0)
    j = _iota((nc, width), 1)
    lo = jnp.maximum(i * CMP_STRIDE, j * SEL_BLOCK)
    hi = jnp.minimum(i * CMP_STRIDE + CMP_BLOCK, (j + 1) * SEL_BLOCK)
    ov = jnp.maximum(hi - lo, 0).astype(F32) * (1.0 / CMP_BLOCK)
    return jnp.where((i < n_cmp) & (j < n_sel), ov, 0.0).astype(BF16)


def _select_blocks(imp, qpos, n_sel, axis=1):
    blk = _iota(imp.shape, axis)
    valid = blk * SEL_BLOCK <= qpos
    forced = (blk == qpos // SEL_BLOCK) | (blk == 0)
    x = jnp.where(valid, imp + jnp.where(forced, FORCE, 0.0), NEG)
    x = jnp.where(blk < n_sel, x, LOWEST)
    sel = jnp.zeros(imp.shape, F32)
    blk_f = blk.astype(F32)
    for _ in range(N_SEL):
        m = jnp.max(x, axis=axis, keepdims=True)
        first = jnp.min(jnp.where(x == m, blk_f, 4.0 * imp.shape[axis]), axis=axis, keepdims=True)
        pick = blk_f == first
        sel = jnp.where(pick & (m > 0.5 * NEG), 1.0, sel)
        x = jnp.where(pick, LOWEST, x)
    return sel


M_INIT = 0.1 * NEG


def _alibi_slope(head):
    return float(2.0 ** (-8.0 * (head + 1) / N_HEADS))


def _nsa_prompt_t_kernel(q_ref, ck_ref, cvt_ref, ks_ref, vst_ref, kw_ref, vwt_ref, gate_ref, o_ref, *, kt, nc):
    qi = pl.program_id(1)
    cols = GQ * Q_BLOCK
    q0 = qi * Q_BLOCK
    qpos = q0 + _iota((1, Q_BLOCK), 1)
    n_sel = ks_ref.shape[1] // SEL_BLOCK
    r_sel = -(-n_sel // 8) * 8
    bi = _iota((r_sel, nc), 0)
    ci = _iota((r_sel, nc), 1)
    ov = jnp.maximum(jnp.minimum(ci * CMP_STRIDE + CMP_BLOCK, (bi + 1) * SEL_BLOCK)
                     - jnp.maximum(ci * CMP_STRIDE, bi * SEL_BLOCK), 0).astype(F32) * (1.0 / CMP_BLOCK)
    ov = jnp.where((ci < nc - 1) & (bi < n_sel), ov, 0.0).astype(BF16)
    cend = _iota((nc, Q_BLOCK), 0) * CMP_STRIDE + (CMP_BLOCK - 1)
    cend_f = cend.astype(F32)
    mask_c = qpos >= cend
    n_hi = (q0 + Q_BLOCK + kt - 1) // kt
    w_lo = jnp.maximum(q0 - (WINDOW - 1), 0) // kt
    gates_t = gate_ref[...].T
    lanes = lambda g: slice(g * Q_BLOCK, (g + 1) * Q_BLOCK)

    def prepare(h):
        q = q_ref[h * GQ:(h + 1) * GQ].reshape(cols, HEAD_DIM)
        s = _dot_nt(ck_ref[0, h], q)
        ps, p_grp = [], None
        for g in range(GQ):
            sg = jnp.where(mask_c, s[:, lanes(g)] + _alibi_slope(h * GQ + g) * cend_f, NEG)
            p = jnp.where(mask_c, jnp.exp(sg - jnp.max(sg, axis=0, keepdims=True)), 0.0)
            p = p / jnp.maximum(jnp.sum(p, axis=0, keepdims=True), TINY)
            ps.append(p.astype(BF16))
            p_grp = p if p_grp is None else p_grp + p
        o_c = _dot(cvt_ref[0, h], jnp.concatenate(ps, axis=1))
        hi = p_grp.astype(BF16)
        r1 = p_grp - hi.astype(F32)
        mid = r1.astype(BF16)
        lo = (r1 - mid.astype(F32)).astype(BF16)
        imp = _dot(ov, hi) + _dot(ov, mid) + _dot(ov, lo)
        sel = _select_blocks(imp, qpos, n_sel, axis=0).astype(BF16)
        return q, sel, o_c

    def tile(t, carry, h, q, sel, k_ref, vt_ref):
        m, acc = carry
        k0 = pl.multiple_of(t * kt, kt)
        kpos = k0 + _iota((kt, Q_BLOCK), 0)
        kpos_f = kpos.astype(F32)
        s = _dot_nt(k_ref[h, pl.ds(k0, kt), :], q)
        ok = kpos <= qpos
        if sel is not None:
            expand = (k0 + _iota((kt, r_sel), 0)) // SEL_BLOCK == _iota((kt, r_sel), 1)
            ok = ok & (_dot(jnp.where(expand, 1.0, 0.0).astype(BF16), sel) > 0.5)
        else:
            ok = ok & (qpos - kpos < WINDOW)
        s = jnp.concatenate([jnp.where(ok, s[:, lanes(g)] + _alibi_slope(h * GQ + g) * kpos_f, NEG)
                             for g in range(GQ)], axis=1)
        m_new = jnp.maximum(m, jnp.max(s, axis=0, keepdims=True))
        p = jnp.exp(s - m_new).astype(BF16)
        acc = jnp.exp(m - m_new) * acc + _dot(vt_ref[h, t], p)
        return m_new, acc

    outs = [None] * N_HEADS
    for h0 in range(0, N_KV, HEADS_PER_LOOP):
        heads = range(h0, h0 + HEADS_PER_LOOP)
        prep = [prepare(h) for h in heads]

        def sel_tiles(t, carry):
            return [tile(t, c, h, q, sel, ks_ref, vst_ref) for c, h, (q, sel, _) in zip(carry, heads, prep)]

        def win_tiles(t, carry):
            return [tile(t, c, h, q, None, kw_ref, vwt_ref) for c, h, (q, _, _) in zip(carry, heads, prep)]

        init = [(jnp.full((1, cols), M_INIT, F32), jnp.zeros((VT_ROWS, cols), F32)) for _ in heads]
        sel_c = lax.fori_loop(0, w_lo, sel_tiles, init)
        sel_c, win_c = lax.fori_loop(w_lo, n_hi, lambda t, c: (sel_tiles(t, c[0]), win_tiles(t, c[1])), (sel_c, init))
        for h, (_, _, o_c), (_, a_s), (_, a_w) in zip(heads, prep, sel_c, win_c):
            o_s = a_s[0:HEAD_DIM] / jnp.maximum(a_s[HEAD_DIM:HEAD_DIM + 1], TINY)
            o_w = a_w[0:HEAD_DIM] / jnp.maximum(a_w[HEAD_DIM:HEAD_DIM + 1], TINY)
            for g in range(GQ):
                hd = h * GQ + g
                outs[hd] = (gates_t[hd:hd + 1] * o_c[:, lanes(g)] + gates_t[N_HEADS + hd:N_HEADS + hd + 1] * o_s[:, lanes(g)]
                            + gates_t[2 * N_HEADS + hd:2 * N_HEADS + hd + 1] * o_w[:, lanes(g)])
    o_ref[...] = jnp.concatenate(outs, axis=0).T


def _nsa_prompt_t(q_hm, ck, cvt, ks, vst, kw, vwt, gates, nb):
    m = gates.shape[0]
    s = m // nb
    nq = s // Q_BLOCK
    nc = ck.shape[2]
    kt = vst.shape[3]
    kv = pl.BlockSpec((N_KV, s, HEAD_DIM), lambda b, i: (0, b, 0))
    vt = pl.BlockSpec((N_KV, s // kt, VT_ROWS, kt), lambda b, i: (0, b, 0, 0))
    return pl.pallas_call(
        functools.partial(_nsa_prompt_t_kernel, kt=kt, nc=nc), grid=(nb, nq),
        in_specs=[pl.BlockSpec((N_HEADS, Q_BLOCK, HEAD_DIM), lambda b, i: (0, b * nq + i, 0)),
                  pl.BlockSpec((1, N_KV, nc, HEAD_DIM), lambda b, i: (b, 0, 0, 0)),
                  pl.BlockSpec((1, N_KV, HEAD_DIM, nc), lambda b, i: (b, 0, 0, 0)),
                  kv, vt, kv, vt,
                  pl.BlockSpec((Q_BLOCK, GATE_PAD), lambda b, i: (b * nq + i, 0))],
        out_specs=pl.BlockSpec((Q_BLOCK, D_ATTN), lambda b, i: (b * nq + i, 0)),
        out_shape=jax.ShapeDtypeStruct((m, D_ATTN), F32),
        compiler_params=_params(("parallel", "parallel")), name="nsa_prompt")(q_hm, ck, cvt, ks, vst, kw, vwt, gates)


def _nsa_sample_kernel(pt_ref, *refs, n_pages, past):
    del pt_ref
    it = iter(refs)
    pools = [[next(it) for _ in range(n_pages)] for _ in range(4)]
    new = [next(it) for _ in range(6)]
    bkw_ref, bvw_ref, q_ref, gate_ref, slope_ref, wk_ref, wv_ref, o_ref = [next(it) for _ in range(8)]
    kbuf, vbuf = next(it), next(it)
    tok = next(it), next(it)
    nc = past // CMP_STRIDE
    n_sel = past // SEL_BLOCK + 1
    rows = N_HEADS
    slope = slope_ref[...]
    own = _iota((rows, D_KV), 1) // HEAD_DIM == _iota((rows, D_KV), 0) // GQ
    q = jnp.where(own, jnp.concatenate([q_ref[0]] * N_KV, axis=1), 0.0)
    qb = q.astype(BF16)

    def new_row(i):
        return new[i][0].astype(BF16).astype(F32)

    def own_heads(o):
        o = jnp.where(own, o, 0.0)
        return o[:, 0:64] + o[:, 64:128] + o[:, 128:192] + o[:, 192:256]

    cmp = []
    for pages, w_ref, x_new in ((pools[0], wk_ref, new[0]), (pools[1], wv_ref, new[1])):
        for i, p_ref in enumerate(pages):
            pt = p_ref[0].T
            tok[0][i * PAGE:(i + 1) * PAGE, :] = pt[:, 0:128]
            tok[1][i * PAGE:(i + 1) * PAGE, :] = pt[:, 128:256]
        a, b = _pool16(tok, w_ref, past)
        cmp.append((a + _shift_up(b, w_ref[CMP_STRIDE:CMP_STRIDE + 1, :] * x_new[0])).astype(BF16))
    ck, cv = cmp
    dist_c = past - (_iota((1, nc), 1) * CMP_STRIDE + (CMP_BLOCK - 1))
    mask_c = dist_c >= 0
    s = jnp.where(mask_c, _dot_nt(qb, ck) - slope * dist_c.astype(F32), NEG)
    p = jnp.where(mask_c, jnp.exp(s - jnp.max(s, axis=-1, keepdims=True)), 0.0)
    p = p / jnp.maximum(jnp.sum(p, axis=-1, keepdims=True), TINY)
    o_c = own_heads(_dot(p.astype(BF16), cv))
    p_grp = jnp.concatenate(
        [jnp.broadcast_to(jnp.sum(p[h * GQ:(h + 1) * GQ], axis=0, keepdims=True), (GQ, nc)) for h in range(N_KV)], axis=0)
    imp = _dot3(p_grp, _overlap_matrix(nc, 128, nc, n_sel))
    sel = _select_blocks(imp, jnp.full((rows, 1), past, jnp.int32), n_sel)

    def attend(pages, k_new, v_new, mask, new_ok, dist, width):
        for i, (kp, vp) in enumerate(pages):
            kbuf[:, i * kp.shape[1]:(i + 1) * kp.shape[1]] = kp[...].astype(BF16)
            vbuf[:, i * vp.shape[1]:(i + 1) * vp.shape[1]] = vp[...].astype(BF16)
        s = jnp.where(mask, _dot(qb, kbuf[:, 0:width]) - slope * dist.astype(F32), NEG)
        s_new = jnp.where(new_ok, jnp.sum(qb.astype(F32) * k_new, axis=-1, keepdims=True), NEG)
        m = jnp.maximum(jnp.max(s, axis=-1, keepdims=True), s_new)
        p = jnp.where(mask, jnp.exp(s - m), 0.0)
        p_new = jnp.where(new_ok, jnp.exp(s_new - m), 0.0)
        l = jnp.sum(p, axis=-1, keepdims=True) + p_new
        o = _dot_nt(p.astype(BF16), vbuf[:, 0:width]) + p_new.astype(BF16).astype(F32) * v_new
        return own_heads(o) / jnp.maximum(l, TINY)

    dist_s = past - _iota((1, past), 1)
    expand = _iota((128, past), 0) == _iota((128, past), 1) // SEL_BLOCK
    hit = _dot(sel.astype(BF16), jnp.where(expand, 1.0, 0.0).astype(BF16))
    o_s = attend([(k.at[0], v.at[0]) for k, v in zip(pools[2], pools[3])], new_row(2), new_row(3),
                 (hit > 0.5) & (dist_s >= 0), sel[:, n_sel - 1:n_sel] > 0.5, dist_s, past)
    wl = bkw_ref.shape[2]
    dist_w = wl - _iota((1, wl), 1)
    o_w = attend([(bkw_ref.at[0], bvw_ref.at[0])], new_row(4), new_row(5),
                 (dist_w >= 0) & (dist_w < WINDOW), jnp.full((rows, 1), True), dist_w, wl)
    gates = gate_ref[0]
    o_ref[0] = gates[:, 0:1] * o_c + gates[:, 1:2] * o_s + gates[:, 2:3] * o_w


def _nsa_sample(page_table, pools, new_rows, buf_kw, buf_vw, q, gates_t, slopes, wk, wv):
    nb, n_pages = page_table.shape
    past = n_pages * PAGE
    page_specs = []
    for pool in pools:
        for p in range(n_pages):
            page_specs.append(pl.BlockSpec((1,) + pool.shape[1:], lambda b, pt, p=p: (pt[b, p], 0, 0)))
    per_b = lambda shape: pl.BlockSpec((1,) + shape, lambda b, pt: (b,) + (0,) * len(shape))
    in_specs = (page_specs + [per_b((1, D_KV))] * 6 + [per_b(buf_kw.shape[1:])] * 2
                + [per_b((N_HEADS, HEAD_DIM)), per_b((N_HEADS, 3)),
                   pl.BlockSpec(slopes.shape, lambda b, pt: (0, 0)),
                   pl.BlockSpec(wk.shape, lambda b, pt: (0, 0)), pl.BlockSpec(wv.shape, lambda b, pt: (0, 0))])
    args = [pool for pool in pools for _ in range(n_pages)] + list(new_rows) + [buf_kw, buf_vw, q, gates_t, slopes, wk, wv]
    return pl.pallas_call(
        functools.partial(_nsa_sample_kernel, n_pages=n_pages, past=past),
        grid_spec=pltpu.PrefetchScalarGridSpec(
            num_scalar_prefetch=1, grid=(nb,), in_specs=in_specs,
            out_specs=per_b((N_HEADS, HEAD_DIM)),
            scratch_shapes=[pltpu.VMEM((D_KV, past), BF16)] * 2 + [pltpu.VMEM((past, D_KV // 2), F32)] * 2),
        out_shape=jax.ShapeDtypeStruct((nb, N_HEADS, HEAD_DIM), F32),
        compiler_params=_params(("arbitrary",)), name="nsa_sample")(page_table, *args)


def _s5_disc_kernel(lr_ref, li_ref, ldt_ref, ar_ref, ai_ref, fr_ref, fi_ref):
    lr, li = lr_ref[...], li_ref[...]
    dt = jnp.exp(ldt_ref[...])
    mag = jnp.exp(lr * dt)
    ar = mag * jnp.cos(li * dt)
    ai = mag * jnp.sin(li * dt)
    den = lr * lr + li * li
    ar_ref[...] = ar
    ai_ref[...] = ai
    fr_ref[...] = ((ar - 1.0) * lr + ai * li) / den
    fi_ref[...] = (ai * lr - (ar - 1.0) * li) / den


def _s5_bbar_kernel(fr_ref, fi_ref, br_ref, bi_ref, or_ref, oi_ref):
    fr, fi, br, bi = fr_ref[...], fi_ref[...], br_ref[...], bi_ref[...]
    or_ref[...] = fr * br - fi * bi
    oi_ref[...] = fr * bi + fi * br


def _s5_weights(lam_re, lam_im, log_dt, b_re, b_im, c_re, c_im):
    g, n = lam_re.shape
    sd = jax.ShapeDtypeStruct((g, n), F32)
    ar, ai, fr, fi = pl.pallas_call(_s5_disc_kernel, out_shape=[sd] * 4, name="s5_discretise")(
        lam_re, lam_im, log_dt.reshape(g, 1))
    sb = jax.ShapeDtypeStruct((g * n, SSM_GROUP), F32)
    bbr, bbi = pl.pallas_call(_s5_bbar_kernel, out_shape=[sb] * 2, name="s5_bbar")(
        fr.reshape(g * n, 1), fi.reshape(g * n, 1), b_re.reshape(g * n, SSM_GROUP), b_im.reshape(g * n, SSM_GROUP))
    eye = jnp.eye(g // SSM_BANDS, dtype=F32)
    gl = g // SSM_BANDS

    def band_in(bb):
        x = bb.reshape(SSM_BANDS, gl, n, SSM_GROUP).transpose(0, 1, 3, 2)
        return jnp.einsum("jgpn,gh->jgphn", x, eye).reshape(SSM_BANDS, gl * SSM_GROUP, gl * n).astype(BF16)

    def band_out(c):
        x = c.reshape(SSM_BANDS, gl, SSM_GROUP, n).transpose(0, 1, 3, 2)
        return jnp.einsum("jgnp,gh->jgnhp", x, eye).reshape(SSM_BANDS, gl * n, gl * SSM_GROUP).astype(BF16)

    return (ar.reshape(1, g * n), ai.reshape(1, g * n), band_in(bbr), band_in(bbi), band_out(c_re), band_out(-c_im))


def _s5_prompt_kernel(u_ref, wbr_ref, wbi_ref, ar_ref, ai_ref, wcr_ref, wci_ref, d_ref,
                      y_ref, sr_ref, si_ref, hr_s, hi_s, cr_s, ci_s, *, nb):
    c = pl.program_id(1)
    rows, width = hr_s.shape
    rep = 8 // nb

    @pl.when(c == 0)
    def _():
        cr_s[...] = jnp.zeros_like(cr_s)
        ci_s[...] = jnp.zeros_like(ci_s)

    u = u_ref[...]
    ub = u.astype(BF16)
    hr_s[...] = _dot(ub, wbr_ref[0])
    hi_s[...] = _dot(ub, wbi_ref[0])
    ar = jnp.broadcast_to(ar_ref[...], (8, width))
    ai = jnp.broadcast_to(ai_ref[...], (8, width))
    sub = _iota((8, width), 0) // nb

    def step(i, carry):
        sr, si = carry
        base = pl.multiple_of(i * 8, 8)
        xr = hr_s[pl.ds(base, 8), :]
        xi = hi_s[pl.ds(base, 8), :]
        outr, outi = xr, xi
        for k in range(rep):
            yr = ar * sr - ai * si + xr
            yi = ar * si + ai * sr + xi
            keep = sub == k
            outr = jnp.where(keep, yr, outr)
            outi = jnp.where(keep, yi, outi)
            zr = jnp.where(keep, yr, 0.0)
            zi = jnp.where(keep, yi, 0.0)
            sr, si = zr, zi
            for sh in range(1, rep):
                sr = sr + pltpu.roll(zr, sh * nb, 0)
                si = si + pltpu.roll(zi, sh * nb, 0)
        hr_s[pl.ds(base, 8), :] = outr
        hi_s[pl.ds(base, 8), :] = outi
        return sr, si

    sr, si = lax.fori_loop(0, rows // 8, step, (cr_s[...], ci_s[...]))
    cr_s[...] = sr
    ci_s[...] = si
    y = _dot(hr_s[...].astype(BF16), wcr_ref[0]) + _dot(hi_s[...].astype(BF16), wci_ref[0]) + d_ref[...] * u
    y_ref[...] = jax.nn.gelu(y)

    @pl.when(c == pl.num_programs(1) - 1)
    def _():
        sr_ref[...] = sr[0:nb]
        si_ref[...] = si[0:nb]


def _s5_prompt(u_tm, weights, d_skip, nb, tc):
    ar, ai, wbr, wbi, wcr, wci = weights
    rows = u_tm.shape[0]
    cw = D_SSM // SSM_BANDS
    sw = ar.shape[1] // SSM_BANDS
    blk = tc * nb
    tile = pl.BlockSpec((blk, cw), lambda j, c: (c, j))
    band = lambda a: pl.BlockSpec((1,) + a.shape[1:], lambda j, c: (j, 0, 0))
    vec = lambda w: pl.BlockSpec((1, w), lambda j, c: (0, j))
    state = pl.BlockSpec((nb, sw), lambda j, c: (0, j))
    return pl.pallas_call(
        functools.partial(_s5_prompt_kernel, nb=nb), grid=(SSM_BANDS, rows // blk),
        in_specs=[tile, band(wbr), band(wbi), vec(sw), vec(sw), band(wcr), band(wci), vec(cw)],
        out_specs=[tile, state, state],
        out_shape=[jax.ShapeDtypeStruct((rows, D_SSM), F32)] + [jax.ShapeDtypeStruct((nb, ar.shape[1]), F32)] * 2,
        scratch_shapes=[pltpu.VMEM((blk, sw), F32)] * 2 + [pltpu.VMEM((8, sw), F32)] * 2,
        compiler_params=_params(("parallel", "arbitrary")), name="s5_prompt")(
            u_tm, wbr, wbi, ar, ai, wcr, wci, d_skip)


def _s5_sample_kernel(u_ref, h0r_ref, h0i_ref, wbr_ref, wbi_ref, ar_ref, ai_ref, wcr_ref, wci_ref, d_ref,
                      y_ref, sr_ref, si_ref):
    cw = D_SSM // SSM_BANDS
    sw = ar_ref.shape[1] // SSM_BANDS
    for j in range(SSM_BANDS):
        cs = slice(j * cw, (j + 1) * cw)
        ss = slice(j * sw, (j + 1) * sw)
        u = u_ref[:, cs]
        ub = u.astype(BF16)
        ar, ai = ar_ref[:, ss], ai_ref[:, ss]
        h0r, h0i = h0r_ref[:, ss], h0i_ref[:, ss]
        hr = _dot(ub, wbr_ref[j]) + (ar * h0r - ai * h0i)
        hi = _dot(ub, wbi_ref[j]) + (ar * h0i + ai * h0r)
        sr_ref[:, ss] = hr
        si_ref[:, ss] = hi
        y = _dot(hr.astype(BF16), wcr_ref[j]) + _dot(hi.astype(BF16), wci_ref[j]) + d_ref[:, cs] * u
        y_ref[:, cs] = jax.nn.gelu(y)


def _s5_sample(u, h0r, h0i, weights, d_skip):
    ar, ai, wbr, wbi, wcr, wci = weights
    nb = u.shape[0]
    st = jax.ShapeDtypeStruct(h0r.shape, F32)
    return pl.pallas_call(
        _s5_sample_kernel, out_shape=[jax.ShapeDtypeStruct((nb, D_SSM), F32), st, st],
        compiler_params=pltpu.CompilerParams(vmem_limit_bytes=VMEM_LIMIT), name="s5_sample")(
            u, h0r, h0i, wbr, wbi, ar, ai, wcr, wci, d_skip)


def _xattn_prompt_kernel(q_ref, k_ref, v_ref, o_ref):
    scale = X_HEAD_DIM ** -0.5
    for h in range(X_HEADS):
        cs = slice(h * X_HEAD_DIM, (h + 1) * X_HEAD_DIM)
        s = _dot_nt(q_ref[:, cs], k_ref[:, cs].astype(BF16)) * scale
        p = jnp.exp(s - jnp.max(s, axis=-1, keepdims=True))
        p = p / jnp.sum(p, axis=-1, keepdims=True)
        o_ref[:, cs] = _dot(p.astype(BF16), v_ref[:, cs].astype(BF16)).astype(o_ref.dtype)


def _xattn_prompt(q, mk, mv, nb, tm):
    m = q.shape[0]
    nt = m // nb // tm
    mem = pl.BlockSpec((MEM_LEN, D_MODEL), lambda b, i: (b, 0))
    row = pl.BlockSpec((tm, D_MODEL), lambda b, i: (b * nt + i, 0))
    return pl.pallas_call(
        _xattn_prompt_kernel, grid=(nb, nt), in_specs=[row, mem, mem], out_specs=row,
        out_shape=jax.ShapeDtypeStruct((m, D_MODEL), BF16),
        compiler_params=_params(("parallel", "parallel")), name="xattn_prompt")(q, mk, mv)


def _xattn_sample_kernel(q_ref, k_ref, v_ref, o_ref):
    scale = X_HEAD_DIM ** -0.5
    nt = X_HEAD_DIM // 128
    rows = MEM_LEN * nt * X_HEADS

    def head(ref, h):
        parts = [ref[0, pl.ds(t * X_HEADS + h, MEM_LEN, stride=nt * X_HEADS), :] for t in range(nt)]
        return jnp.concatenate(parts, axis=1).astype(BF16)

    assert k_ref.shape[1] == rows
    for h in range(X_HEADS):
        cs = slice(h * X_HEAD_DIM, (h + 1) * X_HEAD_DIM)
        q = jnp.broadcast_to(q_ref[0, :, cs], (8, X_HEAD_DIM)).astype(BF16)
        s = _dot_nt(q, head(k_ref, h)) * scale
        p = jnp.exp(s - jnp.max(s, axis=-1, keepdims=True))
        p = p / jnp.sum(p, axis=-1, keepdims=True)
        o_ref[0, :, cs] = _dot(p.astype(BF16), head(v_ref, h))[0:1].astype(o_ref.dtype)


def _xattn_sample(q, mk, mv):
    nb = q.shape[0]
    mem = pl.BlockSpec((1,) + mk.shape[1:], lambda b: (b, 0, 0))
    row = pl.BlockSpec((1, 1, D_MODEL), lambda b: (b, 0, 0))
    return pl.pallas_call(
        _xattn_sample_kernel, grid=(nb,), in_specs=[row, mem, mem], out_specs=row,
        out_shape=jax.ShapeDtypeStruct((nb, 1, D_MODEL), BF16),
        compiler_params=_params(("parallel",)), name="xattn_sample")(q, mk, mv)


def _top_distinct(x, n):
    rows = []
    for _ in range(n):
        m = jnp.max(x, axis=0, keepdims=True)
        rows.append(m)
        x = jnp.where(x == m, LOWEST, x)
    return jnp.concatenate(rows, axis=0)


def _peer_route_kernel(x_ref, g_ref, w_ref, k1_ref, k2_ref, xt_ref, thr_ref, s2_ref, e2_ref, c1_ref):
    xn = _rms(x_ref[...], g_ref[...])
    xt_ref[...] = xn.T.astype(BF16)
    q = _dot(xn.astype(BF16), w_ref[...]).astype(BF16)
    half = PEER_KEYS
    for h in range(PEER_HEADS):
        s1 = _dot_nt(k1_ref[...], q[:, 2 * half * h:2 * half * h + half])
        s2 = _dot_nt(k2_ref[...], q[:, 2 * half * h + half:2 * half * (h + 1)])
        d1 = _top_distinct(s1, PEER_TOPK)
        d2 = _top_distinct(s2, PEER_TOPK)
        cand = jnp.concatenate([d1[a:a + 1] + d2[0:PEER_TOPK // (a + 1)] for a in range(PEER_TOPK)], axis=0)
        tau = _top_distinct(cand, PEER_TOPK)[PEER_TOPK - 1:PEER_TOPK]
        top = d1[0:1] + d2[0:1]
        z = jnp.sum(jnp.where(cand >= tau, jnp.exp(cand - top), 0.0), axis=0, keepdims=True)
        thr = jnp.full(s1.shape, HUGE, F32)
        for a in range(PEER_TOPK):
            ok = (d1[a:a + 1] + d2) >= tau
            thr_a = jnp.min(jnp.where(ok, d2, HUGE), axis=0, keepdims=True)
            thr = jnp.where(s1 == d1[a:a + 1], thr_a, thr)
        thr_ref[h] = thr
        s2_ref[h] = s2
        e2_ref[h] = jnp.exp(s2 - d2[0:1])
        c1_ref[h] = jnp.exp(s1 - d1[0:1]) / z


def _peer_route(x, g, w_pq, k1, k2, tm):
    m = x.shape[0]
    hk = pl.BlockSpec((PEER_HEADS, PEER_KEYS, tm), lambda i: (0, 0, i))
    hks = jax.ShapeDtypeStruct((PEER_HEADS, PEER_KEYS, m), F32)
    return pl.pallas_call(
        _peer_route_kernel, grid=(m // tm,),
        in_specs=[pl.BlockSpec((tm, D_MODEL), lambda i: (i, 0)), _full(g), _full(w_pq), _full(k1), _full(k2)],
        out_specs=[pl.BlockSpec((D_MODEL, tm), lambda i: (0, i)), hk, hk, hk, hk],
        out_shape=[jax.ShapeDtypeStruct((D_MODEL, m), BF16), hks, hks, hks, hks],
        compiler_params=_params(("parallel",)), name="peer_route")(x, g, w_pq, k1, k2)


def _peer_kernel(xt_ref, u_ref, vt_ref, thr_ref, s2_ref, e2_ref, c1_ref, o_ref, wa_s, wb_s, act_s):
    e = pl.program_id(1)
    et, tt = act_s.shape
    n_sub = et // PEER_KEYS
    group = 2
    oc = o_ref.shape[0] // n_sub

    @pl.when(e == 0)
    def _():
        o_ref[...] = jnp.zeros_like(o_ref)
        wb_s[...] = jnp.zeros_like(wb_s)

    def step(cur_s, prev_s):
        def score(c):
            rs = slice(c * SCORE_ROWS, (c + 1) * SCORE_ROWS)
            act_s[rs, :] = _dot(u_ref[rs, :], xt_ref[...])

        def emit(c):
            rs = slice(c * EMIT_ROWS, (c + 1) * EMIT_ROWS)
            o_ref[rs, :] += _dot(vt_ref[rs, :], prev_s[...])

        def gate_group(j0, lt):
            ls = slice(lt * 128, (lt + 1) * 128)
            for part in range(PEER_KEYS // GATE_ROWS):
                ks = slice(part * GATE_ROWS, (part + 1) * GATE_ROWS)
                gates = [None] * group
                for h in range(PEER_HEADS):
                    s2 = s2_ref[h, ks, ls]
                    e2 = e2_ref[h, ks, ls]
                    for k in range(group):
                        jj = j0 + k
                        g = jnp.where(s2 >= thr_ref[h, jj:jj + 1, ls], e2, 0.0) * c1_ref[h, jj:jj + 1, ls]
                        gates[k] = g if gates[k] is None else gates[k] + g
                for k in range(group):
                    r0 = (j0 + k) * PEER_KEYS + part * GATE_ROWS
                    rs = slice(r0, r0 + GATE_ROWS)
                    cur_s[rs, ls] = (gates[k] * jax.nn.gelu(act_s[rs, ls])).astype(BF16)

        scores = list(range(et // SCORE_ROWS))
        emits = list(range(o_ref.shape[0] // EMIT_ROWS))
        for j0 in range(0, n_sub, group):
            while scores and scores[0] * SCORE_ROWS < (j0 + group) * PEER_KEYS:
                score(scores.pop(0))
            for lt in range(tt // 128):
                gate_group(j0, lt)
                if scores:
                    score(scores.pop(0))
                elif emits:
                    emit(emits.pop(0))
        for c in emits:
            emit(c)

    @pl.when(e % 2 == 0)
    def _():
        step(wa_s, wb_s)

    @pl.when(e % 2 == 1)
    def _():
        step(wb_s, wa_s)


def _peer(xt, u, vt, thr, s2, e2, c1, tt, et):
    m = xt.shape[1]
    n_e = u.shape[0] // et
    tok3 = lambda r: pl.BlockSpec((PEER_HEADS, r, tt), lambda i, e: (0, 0, i))
    tile3 = pl.BlockSpec((PEER_HEADS, et // PEER_KEYS, tt), lambda i, e: (0, jnp.minimum(e, n_e - 1), i))
    return pl.pallas_call(
        _peer_kernel, grid=(m // tt, n_e + 1),
        in_specs=[pl.BlockSpec((D_MODEL, tt), lambda i, e: (0, i)),
                  pl.BlockSpec((et, D_MODEL), lambda i, e: (jnp.minimum(e, n_e - 1), 0)),
                  pl.BlockSpec((D_MODEL, et), lambda i, e: (0, jnp.maximum(e - 1, 0))),
                  tile3, tok3(PEER_KEYS), tok3(PEER_KEYS), tile3],
        out_specs=pl.BlockSpec((D_MODEL, tt), lambda i, e: (0, i)),
        out_shape=jax.ShapeDtypeStruct((D_MODEL, m), F32),
        scratch_shapes=[pltpu.VMEM((et, tt), BF16)] * 2 + [pltpu.VMEM((et, tt), F32)],
        compiler_params=_params(("parallel", "arbitrary")), name="peer_dense")(
            xt, u, vt, thr, s2, e2, c1)


def _final_kernel(h_ref, ot_ref, g_ref, y_ref):
    y_ref[...] = _rms(h_ref[...] + ot_ref[...].T, g_ref[...])


def _final_norm(h, out_t, g, tm):
    m = h.shape[0]
    row = pl.BlockSpec((tm, D_MODEL), lambda i: (i, 0))
    return pl.pallas_call(
        _final_kernel, grid=(m // tm,),
        in_specs=[row, pl.BlockSpec((D_MODEL, tm), lambda i: (0, i)), _full(g)], out_specs=row,
        out_shape=jax.ShapeDtypeStruct((m, D_MODEL), F32),
        compiler_params=_params(("parallel",)), name="final_norm")(h, out_t, g)


def _row(v):
    return v.reshape(1, -1).astype(F32)


def _cmp_rows(w):
    return jnp.repeat(w.T.astype(F32), HEAD_DIM, axis=1)


def kernel(x_prompt, x_sample, mem_prompt, cache_k_cmp, cache_v_cmp, cache_k_sel, cache_v_sel, cache_k_win, cache_v_win, state_s5_re, state_s5_im, cache_mem_k, cache_mem_v, page_table, g_mix, w_in, w_cmp_k, w_cmp_v, lam_re, lam_im, log_dt, b_re, b_im, c_re, c_im, d_skip, w_glu, g_attn_out, g_ssm_out, w_out, g_x, g_mem, w_xq, w_xk, w_xv, w_xo, g_ffn, w_pq, peer_k1, peer_k2, peer_u, peer_v, g_final):
    nb, seq, _ = x_prompt.shape
    db = x_sample.shape[0]
    depth = g_mix.shape[0]
    assert depth == 1 and x_sample.shape[1] == 1
    l = 0
    mp = nb * seq
    n_gate = 3 * N_HEADS
    kv_end = D_ATTN + 6 * D_KV

    w_attn = jnp.concatenate([w_in[l][:, :kv_end + n_gate], jnp.zeros((D_MODEL, GATE_PAD - n_gate), F32)], axis=1).astype(BF16)
    w_u = w_in[l][:, kv_end + n_gate:].astype(BF16)
    w_full = jnp.concatenate([w_attn, w_u], axis=1)
    w_glu_b = w_glu[l].astype(BF16)
    w_out_a = w_out[l][:D_ATTN].astype(BF16)
    w_out_s = w_out[l][D_ATTN:].astype(BF16)
    w_mem = jnp.concatenate([w_xk[l], w_xv[l]], axis=1).astype(BF16)
    w_xq_b = w_xq[l].astype(BF16)
    w_xo_b = w_xo[l].astype(BF16)
    w_pq_b = w_pq[l].astype(BF16)
    k1_b = peer_k1[l].astype(BF16)
    k2_b = peer_k2[l].astype(BF16)
    u_b = peer_u[l].astype(BF16)
    vt_b = peer_v[l].T.astype(BF16)
    wk_rows = _cmp_rows(w_cmp_k[l])
    wv_rows = _cmp_rows(w_cmp_v[l])
    gm, gx, gf, gfin = _row(g_mix[l]), _row(g_x[l]), _row(g_ffn[l]), _row(g_final)
    ga, gs, gme = _row(g_attn_out[l]), _row(g_ssm_out[l]), _row(g_mem[l])
    s5w = _s5_weights(lam_re[l], lam_im[l], log_dt[l], b_re[l], b_im[l], c_re[l], c_im[l])
    dsk = _row(d_skip[l])

    xp = x_prompt.reshape(mp, D_MODEL)
    q_hm, kv, k_hm, gates, v_t = _proj_attn(xp, gm, w_attn, tm=256)
    u_tm = _proj_u(xp, gm, w_u, nb, tm=256).reshape(seq * nb, D_SSM)
    ck, cv_t = _compress(kv[0], kv[1], wk_rows, wv_rows, nb)
    o_a = _nsa_prompt_t(q_hm, ck, cv_t, k_hm[0], v_t[0], k_hm[1], v_t[1], gates, nb)
    y_tm, p_sr, p_si = _s5_prompt(u_tm, s5w, dsk, nb, tc=256)
    o_s = _glu(y_tm.reshape(seq, nb * D_SSM), w_glu_b, nb, tm=256)
    h1 = _merge(o_a, o_s, ga, gs, w_out_a, w_out_s, xp, tm=256)
    mk, mv = _mem_kv(mem_prompt.reshape(nb * MEM_LEN, D_MODEL), gme, w_mem, tm=256)
    xq = _mm(h1, w_xq_b, 256, BF16, g=gx, name="xattn_q")
    xo = _xattn_prompt(xq, mk, mv, nb, tm=256)
    h2 = _mm(xo, w_xo_b, 256, F32, res=h1, name="xattn_o")
    routed = _peer_route(h2, gf, w_pq_b, k1_b, k2_b, tm=256)
    y_p = _final_norm(h2, _peer(routed[0], u_b, vt_b, *routed[1:], tt=512, et=1024), gfin, tm=256)

    xs = x_sample.reshape(db, D_MODEL)
    z = _mm(xs, w_full, db, F32, g=gm, gate_cols=(kv_end, kv_end + n_gate), name="proj_sample")
    new_rows = [z[:, D_ATTN + D_KV * i:D_ATTN + D_KV * (i + 1)] for i in range(6)]
    n_phys = cache_k_cmp.shape[1]
    pools = [c[l].transpose(0, 2, 3, 1).reshape(n_phys, D_KV, PAGE)
             for c in (cache_k_cmp, cache_v_cmp, cache_k_sel, cache_v_sel)]
    wl = cache_k_win.shape[2]
    buf_kw = cache_k_win[l].transpose(0, 2, 3, 1).reshape(db, D_KV, wl)
    buf_vw = cache_v_win[l].transpose(0, 2, 3, 1).reshape(db, D_KV, wl)
    q_s = (z[:, :D_ATTN] * (HEAD_DIM ** -0.5)).reshape(db, N_HEADS, HEAD_DIM)
    gates_t = z[:, kv_end:kv_end + n_gate].reshape(db, 3, N_HEADS).transpose(0, 2, 1)
    slopes = jnp.asarray(np.array([[_alibi_slope(i)] for i in range(N_HEADS)], np.float32))
    o_a_s = _nsa_sample(page_table, pools, [r.reshape(db, 1, D_KV) for r in new_rows], buf_kw, buf_vw,
                        q_s, gates_t, slopes, wk_rows, wv_rows).reshape(db, D_ATTN)
    y_s, s_sr, s_si = _s5_sample(z[:, kv_end + GATE_PAD:], state_s5_re[l].reshape(db, -1), state_s5_im[l].reshape(db, -1), s5w, dsk)
    o_s_s = _glu(y_s, w_glu_b, 1, tm=db)
    h1s = _merge(o_a_s, o_s_s, ga, gs, w_out_a, w_out_s, xs, tm=db)
    xq_s = _mm(h1s, w_xq_b, db, F32, g=gx, name="xattn_q_sample")
    nt = X_HEAD_DIM // 128

    def mem_rows(c):
        return c.reshape(db, MEM_LEN, X_HEADS, nt, 128).transpose(0, 1, 3, 2, 4).reshape(db, MEM_LEN * nt * X_HEADS, 128)

    xo_s = _xattn_sample(xq_s.reshape(db, 1, D_MODEL), mem_rows(cache_mem_k[l]), mem_rows(cache_mem_v[l])).reshape(db, D_MODEL)
    h2s = _mm(xo_s, w_xo_b, db, F32, res=h1s, name="xattn_o_sample")
    routed_s = _peer_route(h2s, gf, w_pq_b, k1_b, k2_b, tm=db)
    y_s_out = _final_norm(h2s, _peer(routed_s[0], u_b, vt_b, *routed_s[1:], tt=db, et=1024), gfin, tm=db)

    kvshape = (1, nb, seq, N_KV, HEAD_DIM)
    wlp = min(WINDOW, seq)
    p_kv = [a.reshape(kvshape) for a in kv]
    p_win = [a[:, :, seq - wlp:] for a in p_kv[4:6]]
    s_new = [r.reshape(1, db, 1, N_KV, HEAD_DIM) for r in new_rows]
    s_win = [jnp.concatenate([c[l], n[0]], axis=1)[None, :, -min(WINDOW, wl + 1):] for c, n in ((cache_k_win, s_new[4]), (cache_v_win, s_new[5]))]
    g64 = (1, -1, N_SSM_GROUPS, SSM_STATE)
    return (y_p.reshape(nb, seq, D_MODEL), y_s_out.reshape(db, 1, D_MODEL),
            p_kv[0], p_kv[1], p_kv[2], p_kv[3], p_win[0], p_win[1],
            p_sr.reshape(g64), p_si.reshape(g64),
            mk.reshape(1, nb, MEM_LEN, X_HEADS, X_HEAD_DIM), mv.reshape(1, nb, MEM_LEN, X_HEADS, X_HEAD_DIM),
            s_new[0], s_new[1], s_new[2], s_new[3], s_win[0], s_win[1],
            s_sr.reshape(g64), s_si.reshape(g64))
```

```python
import functools
import math

import jax
import jax.numpy as jnp
import numpy as np
from jax import lax
from jax.experimental import pallas as pl
from jax.experimental.pallas import tpu as pltpu

F32 = jnp.float32
BF16 = jnp.bfloat16

D_MODEL = 2048
N_HEADS = 16
HEAD_DIM = 64
N_KV = 4
GQ = 4
D_ATTN = 1024
D_SSM = 1024
D_KV = 256
CMP_BLOCK = 32
CMP_STRIDE = 16
SEL_BLOCK = 64
N_SEL = 8
WINDOW = 512
Q_BLOCK = 128
VT_ROWS = 80
HEADS_PER_LOOP = 4
FORCE = 1e4
N_SSM_GROUPS = 64
SSM_GROUP = 16
SSM_STATE = 64
SSM_BANDS = 4
MEM_LEN = 256
X_HEADS = 4
X_HEAD_DIM = 512
PEER_KEYS = 128
PEER_HEADS = 8
PEER_TOPK = 16
GATE_ROWS = 32
SCORE_ROWS = 128
EMIT_ROWS = 256
PAGE = 128
EPS = 1e-6
NEG = -1e30
TINY = 1e-30
LOWEST = -3.0e38
HUGE = 3.0e38
GATE_PAD = 128
VMEM_LIMIT = 56 * 2**20


def _params(sem, flags=None):
    return pltpu.CompilerParams(dimension_semantics=sem, vmem_limit_bytes=VMEM_LIMIT, flags=flags)


def _full(a):
    nd = a.ndim
    return pl.BlockSpec(a.shape, lambda *_: (0,) * nd)


def _rms(x, g):
    return x * lax.rsqrt(jnp.mean(x * x, axis=-1, keepdims=True) + EPS) * g


def _dot(a, b):
    return jnp.dot(a, b, preferred_element_type=F32)


def _dot_nt(a, b):
    return lax.dot_general(a, b, (((1,), (1,)), ((), ())), preferred_element_type=F32)


def _dot3(a, b_exact):
    hi = a.astype(BF16)
    r1 = a - hi.astype(F32)
    mid = r1.astype(BF16)
    lo = (r1 - mid.astype(F32)).astype(BF16)
    return _dot(hi, b_exact) + _dot(mid, b_exact) + _dot(lo, b_exact)


def _iota(shape, dim):
    return lax.broadcasted_iota(jnp.int32, shape, dim)


def _proj_attn_kernel(x_ref, g_ref, w_ref, q_ref, *rest):
    kv_refs, kh_refs, gate_ref, vt_refs = rest[:6], rest[6:8], rest[8], rest[9:11]
    xn = _rms(x_ref[...], g_ref[...]).astype(BF16)
    z = _dot(xn, w_ref[...])
    for hd in range(N_HEADS):
        q_ref[hd] = (z[:, hd * HEAD_DIM:(hd + 1) * HEAD_DIM] * (HEAD_DIM ** -0.5)).astype(BF16)
    for k in range(6):
        kv_refs[k][...] = z[:, D_ATTN + D_KV * k:D_ATTN + D_KV * (k + 1)]
    for kh_ref, k in zip(kh_refs, (2, 4)):
        for h in range(N_KV):
            c0 = D_ATTN + D_KV * k + h * HEAD_DIM
            kh_ref[h] = z[:, c0:c0 + HEAD_DIM].astype(BF16)
    for vt_ref, k in zip(vt_refs, (3, 5)):
        zt = z[:, D_ATTN + D_KV * k:D_ATTN + D_KV * (k + 1)].T.astype(BF16)
        for h in range(N_KV):
            vt_ref[h, 0, 0:HEAD_DIM, :] = zt[h * HEAD_DIM:(h + 1) * HEAD_DIM]
            vt_ref[h, 0, HEAD_DIM:, :] = jnp.ones((VT_ROWS - HEAD_DIM, zt.shape[1]), BF16)
    gate_ref[...] = jax.nn.sigmoid(z[:, D_ATTN + 6 * D_KV:])


def _proj_attn(x, g, w, tm):
    m = x.shape[0]
    row = lambda n: pl.BlockSpec((tm, n), lambda i: (i, 0))
    hm = lambda n: pl.BlockSpec((n, tm, HEAD_DIM), lambda i: (0, i, 0))
    out_shape = ([jax.ShapeDtypeStruct((N_HEADS, m, HEAD_DIM), BF16)]
                 + [jax.ShapeDtypeStruct((m, D_KV), F32)] * 6
                 + [jax.ShapeDtypeStruct((N_KV, m, HEAD_DIM), BF16)] * 2
                 + [jax.ShapeDtypeStruct((m, GATE_PAD), F32)]
                 + [jax.ShapeDtypeStruct((N_KV, m // tm, VT_ROWS, tm), BF16)] * 2)
    vt = pl.BlockSpec((N_KV, 1, VT_ROWS, tm), lambda i: (0, i, 0, 0))
    out_specs = [hm(N_HEADS)] + [row(D_KV)] * 6 + [hm(N_KV)] * 2 + [row(GATE_PAD)] + [vt] * 2
    outs = pl.pallas_call(
        _proj_attn_kernel, grid=(m // tm,),
        in_specs=[row(D_MODEL), _full(g), _full(w)],
        out_specs=out_specs, out_shape=out_shape,
        compiler_params=_params(("parallel",)), name="proj_attn")(x, g, w)
    return outs[0], outs[1:7], outs[7:9], outs[9], outs[10:12]


def _proj_u_kernel(x_ref, g_ref, w_ref, u_ref):
    xn = _rms(x_ref[...], g_ref[...]).astype(BF16)
    u_ref[...] = _dot(xn, w_ref[...])


def _proj_u(x, g, w, nb, tm):
    m = x.shape[0]
    t = m // nb
    nt = t // tm
    return pl.pallas_call(
        _proj_u_kernel, grid=(nb, nt),
        in_specs=[pl.BlockSpec((tm, D_MODEL), lambda b, i: (b * nt + i, 0)), _full(g), _full(w)],
        out_specs=pl.BlockSpec((tm, D_SSM), lambda b, i: (i, b)),
        out_shape=jax.ShapeDtypeStruct((t, nb * D_SSM), F32),
        compiler_params=_params(("parallel", "parallel")), name="proj_u")(x, g, w)


def _mm_kernel(*refs, norm, res, gate_cols):
    it = iter(refs)
    x_ref = next(it)
    g_ref = next(it) if norm else None
    w_ref = next(it)
    r_ref = next(it) if res else None
    o_ref = next(it)
    x = x_ref[...]
    if norm:
        x = _rms(x.astype(F32), g_ref[...])
    z = _dot(x.astype(BF16), w_ref[...])
    if res:
        z = z + r_ref[...]
    if gate_cols is not None:
        col = _iota(z.shape, 1)
        z = jnp.where((col >= gate_cols[0]) & (col < gate_cols[1]), jax.nn.sigmoid(z), z)
    o_ref[...] = z.astype(o_ref.dtype)


def _mm(x, w, tm, out_dtype, g=None, res=None, gate_cols=None, name="mm"):
    m, k = x.shape
    n = w.shape[1]
    row = lambda c: pl.BlockSpec((tm, c), lambda i: (i, 0))
    args, specs = [x], [row(k)]
    if g is not None:
        args.append(g)
        specs.append(_full(g))
    args.append(w)
    specs.append(_full(w))
    if res is not None:
        args.append(res)
        specs.append(row(n))
    return pl.pallas_call(
        functools.partial(_mm_kernel, norm=g is not None, res=res is not None, gate_cols=gate_cols),
        grid=(m // tm,), in_specs=specs, out_specs=row(n),
        out_shape=jax.ShapeDtypeStruct((m, n), out_dtype),
        compiler_params=_params(("parallel",)), name=name)(*args)


def _mm2_kernel(x_ref, g_ref, w_ref, o0_ref, o1_ref):
    xn = _rms(x_ref[...], g_ref[...]).astype(BF16)
    z = _dot(xn, w_ref[...])
    n = o0_ref.shape[1]
    o0_ref[...] = z[:, :n]
    o1_ref[...] = z[:, n:]


def _mem_kv(mem, g, w, tm):
    m = mem.shape[0]
    row = pl.BlockSpec((tm, D_MODEL), lambda i: (i, 0))
    return pl.pallas_call(
        _mm2_kernel, grid=(m // tm,), in_specs=[row, _full(g), _full(w)], out_specs=[row, row],
        out_shape=[jax.ShapeDtypeStruct((m, D_MODEL), F32)] * 2,
        compiler_params=_params(("parallel",)), name="mem_kv")(mem, g, w)


def _glu_kernel(y_ref, w_ref, o_ref):
    y = y_ref[...]
    o_ref[...] = y * jax.nn.sigmoid(_dot(y.astype(BF16), w_ref[...]))


def _glu(y_tm, w, nb, tm):
    t = y_tm.shape[0]
    nt = t // tm
    return pl.pallas_call(
        _glu_kernel, grid=(nb, nt),
        in_specs=[pl.BlockSpec((tm, D_SSM), lambda b, i: (i, b)), _full(w)],
        out_specs=pl.BlockSpec((tm, D_SSM), lambda b, i: (b * nt + i, 0)),
        out_shape=jax.ShapeDtypeStruct((nb * t, D_SSM), F32),
        compiler_params=_params(("parallel", "parallel")), name="glu")(y_tm, w)


def _merge_kernel(oa_ref, os_ref, ga_ref, gs_ref, wa_ref, ws_ref, x_ref, o_ref):
    a = _rms(oa_ref[...], ga_ref[...]).astype(BF16)
    s = _rms(os_ref[...], gs_ref[...]).astype(BF16)
    o_ref[...] = x_ref[...] + (_dot(a, wa_ref[...]) + _dot(s, ws_ref[...]))


def _merge(o_a, o_s, g_a, g_s, w_a, w_s, x, tm):
    m = x.shape[0]
    half = pl.BlockSpec((tm, D_ATTN), lambda i: (i, 0))
    row = pl.BlockSpec((tm, D_MODEL), lambda i: (i, 0))
    return pl.pallas_call(
        _merge_kernel, grid=(m // tm,),
        in_specs=[half, half, _full(g_a), _full(g_s), _full(w_a), _full(w_s), row],
        out_specs=row, out_shape=jax.ShapeDtypeStruct((m, D_MODEL), F32),
        compiler_params=_params(("parallel",)), name="merge_heads")(o_a, o_s, g_a, g_s, w_a, w_s, x)


def _pool16(x_refs, w_ref, tokens):
    halves = []
    for hf, x_ref in enumerate(x_refs):
        lanes = slice(hf * 128, (hf + 1) * 128)
        a = b = None
        for j in range(CMP_STRIDE):
            xj = x_ref[pl.ds(j, tokens // CMP_STRIDE, stride=CMP_STRIDE), :]
            ta = xj * w_ref[j:j + 1, lanes]
            tb = xj * w_ref[CMP_STRIDE + j:CMP_STRIDE + j + 1, lanes]
            a = ta if a is None else a + ta
            b = tb if b is None else b + tb
        halves.append((a, b))
    return (jnp.concatenate([halves[0][0], halves[1][0]], axis=1),
            jnp.concatenate([halves[0][1], halves[1][1]], axis=1))


def _shift_up(b, last_row):
    n = b.shape[0]
    rolled = pltpu.roll(b, n - 1, 0)
    return jnp.where(_iota(b.shape, 0) == n - 1, last_row, rolled)


def _compress_kernel(k0_ref, k1_ref, v0_ref, v1_ref, wk_ref, wv_ref, ck_ref, cv_ref):
    tokens = k0_ref.shape[0]
    a, b = _pool16((k0_ref, k1_ref), wk_ref, tokens)
    c = (a + _shift_up(b, 0.0)).astype(BF16)
    for h in range(N_KV):
        ck_ref[0, h] = c[:, h * HEAD_DIM:(h + 1) * HEAD_DIM]
    a, b = _pool16((v0_ref, v1_ref), wv_ref, tokens)
    ct = (a + _shift_up(b, 0.0)).T.astype(BF16)
    for h in range(N_KV):
        cv_ref[0, h] = ct[h * HEAD_DIM:(h + 1) * HEAD_DIM]


def _compress(kc, vc, wk, wv, nb):
    s = kc.shape[0] // nb
    nc = s // CMP_STRIDE
    lo = pl.BlockSpec((s, D_KV // 2), lambda b: (b, 0))
    hi = pl.BlockSpec((s, D_KV // 2), lambda b: (b, 1))
    out = pl.BlockSpec((1, N_KV, nc, HEAD_DIM), lambda b: (b, 0, 0, 0))
    out_t = pl.BlockSpec((1, N_KV, HEAD_DIM, nc), lambda b: (b, 0, 0, 0))
    return pl.pallas_call(
        _compress_kernel, grid=(nb,), in_specs=[lo, hi, lo, hi, _full(wk), _full(wv)], out_specs=[out, out_t],
        out_shape=[jax.ShapeDtypeStruct((nb, N_KV, nc, HEAD_DIM), BF16),
                   jax.ShapeDtypeStruct((nb, N_KV, HEAD_DIM, nc), BF16)],
        compiler_params=_params(("parallel",)), name="nsa_compress")(kc, kc, vc, vc, wk, wv)


def _overlap_matrix(nc, width, n_cmp, n_sel):
    i = _iota((nc, width), 0)
    j = _iota((nc, width), 1)
    lo = jnp.maximum(i * CMP_STRIDE, j * SEL_BLOCK)
    hi = jnp.minimum(i * CMP_STRIDE + CMP_BLOCK, (j + 1) * SEL_BLOCK)
    ov = jnp.maximum(hi - lo, 0).astype(F32) * (1.0 / CMP_BLOCK)
    return jnp.where((i < n_cmp) & (j < n_sel), ov, 0.0).astype(BF16)


def _select_blocks(imp, qpos, n_sel, axis=1):
    blk = _iota(imp.shape, axis)
    valid = blk * SEL_BLOCK <= qpos
    forced = (blk == qpos // SEL_BLOCK) | (blk == 0)
    x = jnp.where(valid, imp + jnp.where(forced, FORCE, 0.0), NEG)
    x = jnp.where(blk < n_sel, x, LOWEST)
    sel = jnp.zeros(imp.shape, F32)
    blk_f = blk.astype(F32)
    for _ in range(N_SEL):
        m = jnp.max(x, axis=axis, keepdims=True)
        first = jnp.min(jnp.where(x == m, blk_f, 4.0 * imp.shape[axis]), axis=axis, keepdims=True)
        pick = blk_f == first
        sel = jnp.where(pick & (m > 0.5 * NEG), 1.0, sel)
        x = jnp.where(pick, LOWEST, x)
    return sel


M_INIT = 0.1 * NEG


def _alibi_slope(head):
    return float(2.0 ** (-8.0 * (head + 1) / N_HEADS))


def _nsa_prompt_t_kernel(q_ref, ck_ref, cvt_ref, ks_ref, vst_ref, kw_ref, vwt_ref, gate_ref, o_ref, *, kt, nc):
    qi = pl.program_id(1)
    cols = GQ * Q_BLOCK
    q0 = qi * Q_BLOCK
    qpos = q0 + _iota((1, Q_BLOCK), 1)
    n_sel = ks_ref.shape[1] // SEL_BLOCK
    r_sel = -(-n_sel // 8) * 8
    bi = _iota((r_sel, nc), 0)
    ci = _iota((r_sel, nc), 1)
    ov = jnp.maximum(jnp.minimum(ci * CMP_STRIDE + CMP_BLOCK, (bi + 1) * SEL_BLOCK)
                     - jnp.maximum(ci * CMP_STRIDE, bi * SEL_BLOCK), 0).astype(F32) * (1.0 / CMP_BLOCK)
    ov = jnp.where((ci < nc - 1) & (bi < n_sel), ov, 0.0).astype(BF16)
    cend = _iota((nc, Q_BLOCK), 0) * CMP_STRIDE + (CMP_BLOCK - 1)
    cend_f = cend.astype(F32)
    mask_c = qpos >= cend
    n_hi = (q0 + Q_BLOCK + kt - 1) // kt
    w_lo = jnp.maximum(q0 - (WINDOW - 1), 0) // kt
    gates_t = gate_ref[...].T
    lanes = lambda g: slice(g * Q_BLOCK, (g + 1) * Q_BLOCK)

    def prepare(h):
        q = q_ref[h * GQ:(h + 1) * GQ].reshape(cols, HEAD_DIM)
        s = _dot_nt(ck_ref[0, h], q)
        ps, p_grp = [], None
        for g in range(GQ):
            sg = jnp.where(mask_c, s[:, lanes(g)] + _alibi_slope(h * GQ + g) * cend_f, NEG)
            p = jnp.where(mask_c, jnp.exp(sg - jnp.max(sg, axis=0, keepdims=True)), 0.0)
            p = p / jnp.maximum(jnp.sum(p, axis=0, keepdims=True), TINY)
            ps.append(p.astype(BF16))
            p_grp = p if p_grp is None else p_grp + p
        o_c = _dot(cvt_ref[0, h], jnp.concatenate(ps, axis=1))
        hi = p_grp.astype(BF16)
        r1 = p_grp - hi.astype(F32)
        mid = r1.astype(BF16)
        lo = (r1 - mid.astype(F32)).astype(BF16)
        imp = _dot(ov, hi) + _dot(ov, mid) + _dot(ov, lo)
        sel = _select_blocks(imp, qpos, n_sel, axis=0).astype(BF16)
        return q, sel, o_c

    def tile(t, carry, h, q, sel, k_ref, vt_ref):
        m, acc = carry
        k0 = pl.multiple_of(t * kt, kt)
        kpos = k0 + _iota((kt, Q_BLOCK), 0)
        kpos_f = kpos.astype(F32)
        s = _dot_nt(k_ref[h, pl.ds(k0, kt), :], q)
        ok = kpos <= qpos
        if sel is not None:
            expand = (k0 + _iota((kt, r_sel), 0)) // SEL_BLOCK == _iota((kt, r_sel), 1)
            ok = ok & (_dot(jnp.where(expand, 1.0, 0.0).astype(BF16), sel) > 0.5)
        else:
            ok = ok & (qpos - kpos < WINDOW)
        s = jnp.concatenate([jnp.where(ok, s[:, lanes(g)] + _alibi_slope(h * GQ + g) * kpos_f, NEG)
                             for g in range(GQ)], axis=1)
        m_new = jnp.maximum(m, jnp.max(s, axis=0, keepdims=True))
        p = jnp.exp(s - m_new).astype(BF16)
        acc = jnp.exp(m - m_new) * acc + _dot(vt_ref[h, t], p)
        return m_new, acc

    outs = [None] * N_HEADS
    for h0 in range(0, N_KV, HEADS_PER_LOOP):
        heads = range(h0, h0 + HEADS_PER_LOOP)
        prep = [prepare(h) for h in heads]

        def sel_tiles(t, carry):
            return [tile(t, c, h, q, sel, ks_ref, vst_ref) for c, h, (q, sel, _) in zip(carry, heads, prep)]

        def win_tiles(t, carry):
            return [tile(t, c, h, q, None, kw_ref, vwt_ref) for c, h, (q, _, _) in zip(carry, heads, prep)]

        init = [(jnp.full((1, cols), M_INIT, F32), jnp.zeros((VT_ROWS, cols), F32)) for _ in heads]
        sel_c = lax.fori_loop(0, w_lo, sel_tiles, init)
        sel_c, win_c = lax.fori_loop(w_lo, n_hi, lambda t, c: (sel_tiles(t, c[0]), win_tiles(t, c[1])), (sel_c, init))
        for h, (_, _, o_c), (_, a_s), (_, a_w) in zip(heads, prep, sel_c, win_c):
            o_s = a_s[0:HEAD_DIM] / jnp.maximum(a_s[HEAD_DIM:HEAD_DIM + 1], TINY)
            o_w = a_w[0:HEAD_DIM] / jnp.maximum(a_w[HEAD_DIM:HEAD_DIM + 1], TINY)
            for g in range(GQ):
                hd = h * GQ + g
                outs[hd] = (gates_t[hd:hd + 1] * o_c[:, lanes(g)] + gates_t[N_HEADS + hd:N_HEADS + hd + 1] * o_s[:, lanes(g)]
                            + gates_t[2 * N_HEADS + hd:2 * N_HEADS + hd + 1] * o_w[:, lanes(g)])
    o_ref[...] = jnp.concatenate(outs, axis=0).T


def _nsa_prompt_t(q_hm, ck, cvt, ks, vst, kw, vwt, gates, nb):
    m = gates.shape[0]
    s = m // nb
    nq = s // Q_BLOCK
    nc = ck.shape[2]
    kt = vst.shape[3]
    kv = pl.BlockSpec((N_KV, s, HEAD_DIM), lambda b, i: (0, b, 0))
    vt = pl.BlockSpec((N_KV, s // kt, VT_ROWS, kt), lambda b, i: (0, b, 0, 0))
    return pl.pallas_call(
        functools.partial(_nsa_prompt_t_kernel, kt=kt, nc=nc), grid=(nb, nq),
        in_specs=[pl.BlockSpec((N_HEADS, Q_BLOCK, HEAD_DIM), lambda b, i: (0, b * nq + i, 0)),
                  pl.BlockSpec((1, N_KV, nc, HEAD_DIM), lambda b, i: (b, 0, 0, 0)),
                  pl.BlockSpec((1, N_KV, HEAD_DIM, nc), lambda b, i: (b, 0, 0, 0)),
                  kv, vt, kv, vt,
                  pl.BlockSpec((Q_BLOCK, GATE_PAD), lambda b, i: (b * nq + i, 0))],
        out_specs=pl.BlockSpec((Q_BLOCK, D_ATTN), lambda b, i: (b * nq + i, 0)),
        out_shape=jax.ShapeDtypeStruct((m, D_ATTN), F32),
        compiler_params=_params(("parallel", "parallel")), name="nsa_prompt")(q_hm, ck, cvt, ks, vst, kw, vwt, gates)


def _nsa_sample_kernel(pt_ref, *refs, n_pages, past):
    del pt_ref
    it = iter(refs)
    pools = [[next(it) for _ in range(n_pages)] for _ in range(4)]
    new = [next(it) for _ in range(6)]
    bkw_ref, bvw_ref, q_ref, gate_ref, slope_ref, wk_ref, wv_ref, o_ref = [next(it) for _ in range(8)]
    kbuf, vbuf = next(it), next(it)
    kwbuf, vwbuf = next(it), next(it)
    tok = next(it), next(it)
    nc = past // CMP_STRIDE
    n_sel = past // SEL_BLOCK + 1
    rows = N_HEADS
    slope = slope_ref[...]
    own = _iota((rows, D_KV), 1) // HEAD_DIM == _iota((rows, D_KV), 0) // GQ
    q = jnp.where(own, jnp.concatenate([q_ref[0]] * N_KV, axis=1), 0.0)
    qb = q.astype(BF16)

    def new_row(i):
        return new[i][0].astype(BF16).astype(F32)

    def own_heads(o):
        o = jnp.where(own, o, 0.0)
        return o[:, 0:64] + o[:, 64:128] + o[:, 128:192] + o[:, 192:256]

    def pooled(pages, w_ref, x_new):
        for i, p_ref in enumerate(pages):
            pt = p_ref[0].T
            tok[0][i * PAGE:(i + 1) * PAGE, :] = pt[:, 0:128]
            tok[1][i * PAGE:(i + 1) * PAGE, :] = pt[:, 128:256]
        a, b = _pool16(tok, w_ref, past)
        return (a + _shift_up(b, w_ref[CMP_STRIDE:CMP_STRIDE + 1, :] * x_new[0])).astype(BF16)

    def fill(pages, kb, vb):
        for i, (kp, vp) in enumerate(pages):
            kb[:, i * kp.shape[1]:(i + 1) * kp.shape[1]] = kp[...].astype(BF16)
            vb[:, i * vp.shape[1]:(i + 1) * vp.shape[1]] = vp[...].astype(BF16)

    def attend(kb, vb, k_new, v_new, mask, new_ok, dist):
        s = jnp.where(mask, _dot(qb, kb[...]) - slope * dist.astype(F32), NEG)
        s_new = jnp.where(new_ok, jnp.sum(qb.astype(F32) * k_new, axis=-1, keepdims=True), NEG)
        m = jnp.maximum(jnp.max(s, axis=-1, keepdims=True), s_new)
        p = jnp.where(mask, jnp.exp(s - m), 0.0)
        p_new = jnp.where(new_ok, jnp.exp(s_new - m), 0.0)
        l = jnp.sum(p, axis=-1, keepdims=True) + p_new
        o = _dot_nt(p.astype(BF16), vb[...]) + p_new.astype(BF16).astype(F32) * v_new
        return own_heads(o) / jnp.maximum(l, TINY)

    ck = pooled(pools[0], wk_ref, new[0])
    dist_c = past - (_iota((1, nc), 1) * CMP_STRIDE + (CMP_BLOCK - 1))
    mask_c = dist_c >= 0
    s = jnp.where(mask_c, _dot_nt(qb, ck) - slope * dist_c.astype(F32), NEG)
    p = jnp.where(mask_c, jnp.exp(s - jnp.max(s, axis=-1, keepdims=True)), 0.0)
    p = p / jnp.maximum(jnp.sum(p, axis=-1, keepdims=True), TINY)
    p_grp = jnp.concatenate(
        [jnp.broadcast_to(jnp.sum(p[h * GQ:(h + 1) * GQ], axis=0, keepdims=True), (GQ, nc)) for h in range(N_KV)], axis=0)
    imp = _dot3(p_grp, _overlap_matrix(nc, 128, nc, n_sel))
    sel = _select_blocks(imp, jnp.full((rows, 1), past, jnp.int32), n_sel)
    o_c = own_heads(_dot(p.astype(BF16), pooled(pools[1], wv_ref, new[1])))
    wl = bkw_ref.shape[2]
    dist_w = wl - _iota((1, wl), 1)
    fill([(bkw_ref.at[0], bvw_ref.at[0])], kwbuf, vwbuf)
    o_w = attend(kwbuf, vwbuf, new_row(4), new_row(5), (dist_w >= 0) & (dist_w < WINDOW), jnp.full((rows, 1), True), dist_w)
    fill([(k.at[0], v.at[0]) for k, v in zip(pools[2], pools[3])], kbuf, vbuf)
    dist_s = past - _iota((1, past), 1)
    expand = _iota((128, past), 0) == _iota((128, past), 1) // SEL_BLOCK
    hit = _dot(sel.astype(BF16), jnp.where(expand, 1.0, 0.0).astype(BF16))
    o_s = attend(kbuf, vbuf, new_row(2), new_row(3), (hit > 0.5) & (dist_s >= 0), sel[:, n_sel - 1:n_sel] > 0.5, dist_s)
    gates = gate_ref[0]
    o_ref[0] = gates[:, 0:1] * o_c + gates[:, 1:2] * o_s + gates[:, 2:3] * o_w


def _nsa_sample(page_table, pools, new_rows, buf_kw, buf_vw, q, gates_t, slopes, wk, wv):
    nb, n_pages = page_table.shape
    past = n_pages * PAGE
    page_specs = []
    for pool in pools:
        for p in range(n_pages):
            page_specs.append(pl.BlockSpec((1,) + pool.shape[1:], lambda b, pt, p=p: (pt[b, p], 0, 0)))
    per_b = lambda shape: pl.BlockSpec((1,) + shape, lambda b, pt: (b,) + (0,) * len(shape))
    in_specs = (page_specs + [per_b((1, D_KV))] * 6 + [per_b(buf_kw.shape[1:])] * 2
                + [per_b((N_HEADS, HEAD_DIM)), per_b((N_HEADS, 3)),
                   pl.BlockSpec(slopes.shape, lambda b, pt: (0, 0)),
                   pl.BlockSpec(wk.shape, lambda b, pt: (0, 0)), pl.BlockSpec(wv.shape, lambda b, pt: (0, 0))])
    args = [pool for pool in pools for _ in range(n_pages)] + list(new_rows) + [buf_kw, buf_vw, q, gates_t, slopes, wk, wv]
    return pl.pallas_call(
        functools.partial(_nsa_sample_kernel, n_pages=n_pages, past=past),
        grid_spec=pltpu.PrefetchScalarGridSpec(
            num_scalar_prefetch=1, grid=(nb,), in_specs=in_specs,
            out_specs=per_b((N_HEADS, HEAD_DIM)),
            scratch_shapes=([pltpu.VMEM((D_KV, past), BF16)] * 2 + [pltpu.VMEM((D_KV, buf_kw.shape[2]), BF16)] * 2
                            + [pltpu.VMEM((past, D_KV // 2), F32)] * 2)),
        out_shape=jax.ShapeDtypeStruct((nb, N_HEADS, HEAD_DIM), F32),
        compiler_params=_params(("arbitrary",)), name="nsa_sample")(page_table, *args)


def _s5_disc_kernel(lr_ref, li_ref, ldt_ref, ar_ref, ai_ref, fr_ref, fi_ref):
    lr, li = lr_ref[...], li_ref[...]
    dt = jnp.exp(ldt_ref[...])
    mag = jnp.exp(lr * dt)
    ar = mag * jnp.cos(li * dt)
    ai = mag * jnp.sin(li * dt)
    den = lr * lr + li * li
    ar_ref[...] = ar
    ai_ref[...] = ai
    fr_ref[...] = ((ar - 1.0) * lr + ai * li) / den
    fi_ref[...] = (ai * lr - (ar - 1.0) * li) / den


def _s5_bbar_kernel(fr_ref, fi_ref, br_ref, bi_ref, or_ref, oi_ref):
    fr, fi, br, bi = fr_ref[...], fi_ref[...], br_ref[...], bi_ref[...]
    or_ref[...] = fr * br - fi * bi
    oi_ref[...] = fr * bi + fi * br


def _s5_weights(lam_re, lam_im, log_dt, b_re, b_im, c_re, c_im):
    g, n = lam_re.shape
    sd = jax.ShapeDtypeStruct((g, n), F32)
    ar, ai, fr, fi = pl.pallas_call(_s5_disc_kernel, out_shape=[sd] * 4, name="s5_discretise")(
        lam_re, lam_im, log_dt.reshape(g, 1))
    sb = jax.ShapeDtypeStruct((g * n, SSM_GROUP), F32)
    bbr, bbi = pl.pallas_call(_s5_bbar_kernel, out_shape=[sb] * 2, name="s5_bbar")(
        fr.reshape(g * n, 1), fi.reshape(g * n, 1), b_re.reshape(g * n, SSM_GROUP), b_im.reshape(g * n, SSM_GROUP))
    eye = jnp.eye(g // SSM_BANDS, dtype=F32)
    gl = g // SSM_BANDS

    def band_in(bb):
        x = bb.reshape(SSM_BANDS, gl, n, SSM_GROUP).transpose(0, 1, 3, 2)
        return jnp.einsum("jgpn,gh->jgphn", x, eye).reshape(SSM_BANDS, gl * SSM_GROUP, gl * n).astype(BF16)

    def band_out(c):
        x = c.reshape(SSM_BANDS, gl, SSM_GROUP, n).transpose(0, 1, 3, 2)
        return jnp.einsum("jgnp,gh->jgnhp", x, eye).reshape(SSM_BANDS, gl * n, gl * SSM_GROUP).astype(BF16)

    return (ar.reshape(1, g * n), ai.reshape(1, g * n), band_in(bbr), band_in(bbi), band_out(c_re), band_out(-c_im))


def _s5_prompt_kernel(u_ref, wbr_ref, wbi_ref, ar_ref, ai_ref, wcr_ref, wci_ref, d_ref,
                      y_ref, sr_ref, si_ref, hr_s, hi_s, cr_s, ci_s, *, nb):
    c = pl.program_id(1)
    rows, width = hr_s.shape
    rep = 8 // nb

    @pl.when(c == 0)
    def _():
        cr_s[...] = jnp.zeros_like(cr_s)
        ci_s[...] = jnp.zeros_like(ci_s)

    u = u_ref[...]
    ub = u.astype(BF16)
    hr_s[...] = _dot(ub, wbr_ref[0])
    hi_s[...] = _dot(ub, wbi_ref[0])
    ar = jnp.broadcast_to(ar_ref[...], (8, width))
    ai = jnp.broadcast_to(ai_ref[...], (8, width))
    sub = _iota((8, width), 0) // nb

    def step(i, carry):
        sr, si = carry
        base = pl.multiple_of(i * 8, 8)
        xr = hr_s[pl.ds(base, 8), :]
        xi = hi_s[pl.ds(base, 8), :]
        outr = outi = None
        for k in range(rep):
            yr = ar * sr - ai * si + xr
            yi = ar * si + ai * sr + xi
            outr = yr if k == 0 else jnp.where(sub == k, yr, outr)
            outi = yi if k == 0 else jnp.where(sub == k, yi, outi)
            sr, si = pltpu.roll(yr, nb, 0), pltpu.roll(yi, nb, 0)
        hr_s[pl.ds(base, 8), :] = outr
        hi_s[pl.ds(base, 8), :] = outi
        return sr, si

    sr, si = lax.fori_loop(0, rows // 8, step, (cr_s[...], ci_s[...]))
    cr_s[...] = sr
    ci_s[...] = si
    y = _dot(hr_s[...].astype(BF16), wcr_ref[0]) + _dot(hi_s[...].astype(BF16), wci_ref[0]) + d_ref[...] * u
    y_ref[...] = jax.nn.gelu(y)

    @pl.when(c == pl.num_programs(1) - 1)
    def _():
        sr_ref[...] = sr[0:nb]
        si_ref[...] = si[0:nb]


def _s5_prompt(u_tm, weights, d_skip, nb, tc):
    ar, ai, wbr, wbi, wcr, wci = weights
    rows = u_tm.shape[0]
    cw = D_SSM // SSM_BANDS
    sw = ar.shape[1] // SSM_BANDS
    blk = tc * nb
    tile = pl.BlockSpec((blk, cw), lambda j, c: (c, j))
    band = lambda a: pl.BlockSpec((1,) + a.shape[1:], lambda j, c: (j, 0, 0))
    vec = lambda w: pl.BlockSpec((1, w), lambda j, c: (0, j))
    state = pl.BlockSpec((nb, sw), lambda j, c: (0, j))
    return pl.pallas_call(
        functools.partial(_s5_prompt_kernel, nb=nb), grid=(SSM_BANDS, rows // blk),
        in_specs=[tile, band(wbr), band(wbi), vec(sw), vec(sw), band(wcr), band(wci), vec(cw)],
        out_specs=[tile, state, state],
        out_shape=[jax.ShapeDtypeStruct((rows, D_SSM), F32)] + [jax.ShapeDtypeStruct((nb, ar.shape[1]), F32)] * 2,
        scratch_shapes=[pltpu.VMEM((blk, sw), F32)] * 2 + [pltpu.VMEM((8, sw), F32)] * 2,
        compiler_params=_params(("parallel", "arbitrary")), name="s5_prompt")(
            u_tm, wbr, wbi, ar, ai, wcr, wci, d_skip)


def _s5_sample_kernel(u_ref, h0r_ref, h0i_ref, wbr_ref, wbi_ref, ar_ref, ai_ref, wcr_ref, wci_ref, d_ref,
                      y_ref, sr_ref, si_ref):
    cw = D_SSM // SSM_BANDS
    sw = ar_ref.shape[1] // SSM_BANDS
    for j in range(SSM_BANDS):
        cs = slice(j * cw, (j + 1) * cw)
        ss = slice(j * sw, (j + 1) * sw)
        u = u_ref[:, cs]
        ub = u.astype(BF16)
        ar, ai = ar_ref[:, ss], ai_ref[:, ss]
        h0r, h0i = h0r_ref[:, ss], h0i_ref[:, ss]
        hr = _dot(ub, wbr_ref[j]) + (ar * h0r - ai * h0i)
        hi = _dot(ub, wbi_ref[j]) + (ar * h0i + ai * h0r)
        sr_ref[:, ss] = hr
        si_ref[:, ss] = hi
        y = _dot(hr.astype(BF16), wcr_ref[j]) + _dot(hi.astype(BF16), wci_ref[j]) + d_ref[:, cs] * u
        y_ref[:, cs] = jax.nn.gelu(y)


def _s5_sample(u, h0r, h0i, weights, d_skip):
    ar, ai, wbr, wbi, wcr, wci = weights
    nb = u.shape[0]
    st = jax.ShapeDtypeStruct(h0r.shape, F32)
    return pl.pallas_call(
        _s5_sample_kernel, out_shape=[jax.ShapeDtypeStruct((nb, D_SSM), F32), st, st],
        compiler_params=pltpu.CompilerParams(vmem_limit_bytes=VMEM_LIMIT), name="s5_sample")(
            u, h0r, h0i, wbr, wbi, ar, ai, wcr, wci, d_skip)


def _xattn_prompt_kernel(q_ref, k_ref, v_ref, o_ref):
    scale = X_HEAD_DIM ** -0.5
    for h in range(X_HEADS):
        cs = slice(h * X_HEAD_DIM, (h + 1) * X_HEAD_DIM)
        s = _dot_nt(q_ref[:, cs], k_ref[:, cs].astype(BF16)) * scale
        p = jnp.exp(s - jnp.max(s, axis=-1, keepdims=True))
        p = p / jnp.sum(p, axis=-1, keepdims=True)
        o_ref[:, cs] = _dot(p.astype(BF16), v_ref[:, cs].astype(BF16)).astype(o_ref.dtype)


def _xattn_prompt(q, mk, mv, nb, tm):
    m = q.shape[0]
    nt = m // nb // tm
    mem = pl.BlockSpec((MEM_LEN, D_MODEL), lambda b, i: (b, 0))
    row = pl.BlockSpec((tm, D_MODEL), lambda b, i: (b * nt + i, 0))
    return pl.pallas_call(
        _xattn_prompt_kernel, grid=(nb, nt), in_specs=[row, mem, mem], out_specs=row,
        out_shape=jax.ShapeDtypeStruct((m, D_MODEL), BF16),
        compiler_params=_params(("parallel", "parallel")), name="xattn_prompt")(q, mk, mv)


def _xattn_sample_kernel(q_ref, k_ref, v_ref, o_ref):
    scale = X_HEAD_DIM ** -0.5
    nt = X_HEAD_DIM // 128
    rows = MEM_LEN * nt * X_HEADS

    def head(ref, h):
        parts = [ref[0, pl.ds(t * X_HEADS + h, MEM_LEN, stride=nt * X_HEADS), :] for t in range(nt)]
        return jnp.concatenate(parts, axis=1).astype(BF16)

    assert k_ref.shape[1] == rows
    for h in range(X_HEADS):
        cs = slice(h * X_HEAD_DIM, (h + 1) * X_HEAD_DIM)
        q = jnp.broadcast_to(q_ref[0, :, cs], (8, X_HEAD_DIM)).astype(BF16)
        s = _dot_nt(q, head(k_ref, h)) * scale
        p = jnp.exp(s - jnp.max(s, axis=-1, keepdims=True))
        p = p / jnp.sum(p, axis=-1, keepdims=True)
        o_ref[0, :, cs] = _dot(p.astype(BF16), head(v_ref, h))[0:1].astype(o_ref.dtype)


def _xattn_sample(q, mk, mv):
    nb = q.shape[0]
    mem = pl.BlockSpec((1,) + mk.shape[1:], lambda b: (b, 0, 0))
    row = pl.BlockSpec((1, 1, D_MODEL), lambda b: (b, 0, 0))
    return pl.pallas_call(
        _xattn_sample_kernel, grid=(nb,), in_specs=[row, mem, mem], out_specs=row,
        out_shape=jax.ShapeDtypeStruct((nb, 1, D_MODEL), BF16),
        compiler_params=_params(("parallel",)), name="xattn_sample")(q, mk, mv)


def _top_distinct(x, n):
    rows = []
    for _ in range(n):
        m = jnp.max(x, axis=0, keepdims=True)
        rows.append(m)
        x = jnp.where(x == m, LOWEST, x)
    return jnp.concatenate(rows, axis=0)


def _peer_route_kernel(x_ref, g_ref, w_ref, k1_ref, k2_ref, xt_ref, thr_ref, s2_ref, e2_ref, c1_ref):
    xn = _rms(x_ref[...], g_ref[...])
    xt_ref[...] = xn.T.astype(BF16)
    q = _dot(xn.astype(BF16), w_ref[...]).astype(BF16)
    half = PEER_KEYS
    for h in range(PEER_HEADS):
        s1 = _dot_nt(k1_ref[...], q[:, 2 * half * h:2 * half * h + half])
        s2 = _dot_nt(k2_ref[...], q[:, 2 * half * h + half:2 * half * (h + 1)])
        d1 = _top_distinct(s1, PEER_TOPK)
        d2 = _top_distinct(s2, PEER_TOPK)
        cand = jnp.concatenate([d1[a:a + 1] + d2[0:PEER_TOPK // (a + 1)] for a in range(PEER_TOPK)], axis=0)
        tau = _top_distinct(cand, PEER_TOPK)[PEER_TOPK - 1:PEER_TOPK]
        top = d1[0:1] + d2[0:1]
        z = jnp.sum(jnp.where(cand >= tau, jnp.exp(cand - top), 0.0), axis=0, keepdims=True)
        thr = jnp.full(s1.shape, HUGE, F32)
        for a in range(PEER_TOPK):
            ok = (d1[a:a + 1] + d2) >= tau
            thr_a = jnp.min(jnp.where(ok, d2, HUGE), axis=0, keepdims=True)
            thr = jnp.where(s1 == d1[a:a + 1], thr_a, thr)
        thr_ref[h] = thr
        s2_ref[h] = s2
        e2_ref[h] = jnp.exp(s2 - d2[0:1])
        c1_ref[h] = jnp.exp(s1 - d1[0:1]) / z


def _peer_route(x, g, w_pq, k1, k2, tm):
    m = x.shape[0]
    hk = pl.BlockSpec((PEER_HEADS, PEER_KEYS, tm), lambda i: (0, 0, i))
    hks = jax.ShapeDtypeStruct((PEER_HEADS, PEER_KEYS, m), F32)
    return pl.pallas_call(
        _peer_route_kernel, grid=(m // tm,),
        in_specs=[pl.BlockSpec((tm, D_MODEL), lambda i: (i, 0)), _full(g), _full(w_pq), _full(k1), _full(k2)],
        out_specs=[pl.BlockSpec((D_MODEL, tm), lambda i: (0, i)), hk, hk, hk, hk],
        out_shape=[jax.ShapeDtypeStruct((D_MODEL, m), BF16), hks, hks, hks, hks],
        compiler_params=_params(("parallel",)), name="peer_route")(x, g, w_pq, k1, k2)


def _peer_kernel(xt_ref, u_ref, vt_ref, thr_ref, s2_ref, e2_ref, c1_ref, o_ref, wa_s, wb_s, act_s):
    e = pl.program_id(1)
    et, tt = act_s.shape
    n_sub = et // PEER_KEYS
    group = 2
    oc = o_ref.shape[0] // n_sub

    @pl.when(e == 0)
    def _():
        o_ref[...] = jnp.zeros_like(o_ref)
        wb_s[...] = jnp.zeros_like(wb_s)

    def step(cur_s, prev_s):
        def score(c):
            rs = slice(c * SCORE_ROWS, (c + 1) * SCORE_ROWS)
            act_s[rs, :] = _dot(u_ref[rs, :], xt_ref[...])

        def emit(c):
            rs = slice(c * EMIT_ROWS, (c + 1) * EMIT_ROWS)
            o_ref[rs, :] += _dot(vt_ref[rs, :], prev_s[...])

        def gate_group(j0, lt):
            ls = slice(lt * 128, (lt + 1) * 128)
            for part in range(PEER_KEYS // GATE_ROWS):
                ks = slice(part * GATE_ROWS, (part + 1) * GATE_ROWS)
                gates = [None] * group
                for h in range(PEER_HEADS):
                    s2 = s2_ref[h, ks, ls]
                    e2 = e2_ref[h, ks, ls]
                    for k in range(group):
                        jj = j0 + k
                        g = jnp.where(s2 >= thr_ref[h, jj:jj + 1, ls], e2, 0.0) * c1_ref[h, jj:jj + 1, ls]
                        gates[k] = g if gates[k] is None else gates[k] + g
                for k in range(group):
                    r0 = (j0 + k) * PEER_KEYS + part * GATE_ROWS
                    rs = slice(r0, r0 + GATE_ROWS)
                    cur_s[rs, ls] = (gates[k] * jax.nn.gelu(act_s[rs, ls])).astype(BF16)

        scores = list(range(et // SCORE_ROWS))
        emits = list(range(o_ref.shape[0] // EMIT_ROWS))
        for j0 in range(0, n_sub, group):
            while scores and scores[0] * SCORE_ROWS < (j0 + group) * PEER_KEYS:
                score(scores.pop(0))
            for lt in range(tt // 128):
                gate_group(j0, lt)
                if scores:
                    score(scores.pop(0))
                elif emits:
                    emit(emits.pop(0))
        for c in emits:
            emit(c)

    @pl.when(e % 2 == 0)
    def _():
        step(wa_s, wb_s)

    @pl.when(e % 2 == 1)
    def _():
        step(wb_s, wa_s)


def _peer(xt, u, vt, thr, s2, e2, c1, tt, et):
    m = xt.shape[1]
    n_e = u.shape[0] // et
    tok3 = lambda r: pl.BlockSpec((PEER_HEADS, r, tt), lambda i, e: (0, 0, i))
    tile3 = pl.BlockSpec((PEER_HEADS, et // PEER_KEYS, tt), lambda i, e: (0, jnp.minimum(e, n_e - 1), i))
    return pl.pallas_call(
        _peer_kernel, grid=(m // tt, n_e + 1),
        in_specs=[pl.BlockSpec((D_MODEL, tt), lambda i, e: (0, i)),
                  pl.BlockSpec((et, D_MODEL), lambda i, e: (jnp.minimum(e, n_e - 1), 0)),
                  pl.BlockSpec((D_MODEL, et), lambda i, e: (0, jnp.maximum(e - 1, 0))),
                  tile3, tok3(PEER_KEYS), tok3(PEER_KEYS), tile3],
        out_specs=pl.BlockSpec((D_MODEL, tt), lambda i, e: (0, i)),
        out_shape=jax.ShapeDtypeStruct((D_MODEL, m), F32),
        scratch_shapes=[pltpu.VMEM((et, tt), BF16)] * 2 + [pltpu.VMEM((et, tt), F32)],
        compiler_params=_params(("parallel", "arbitrary")), name="peer_dense")(
            xt, u, vt, thr, s2, e2, c1)


def _final_kernel(h_ref, ot_ref, g_ref, y_ref):
    y_ref[...] = _rms(h_ref[...] + ot_ref[...].T, g_ref[...])


def _final_norm(h, out_t, g, tm):
    m = h.shape[0]
    row = pl.BlockSpec((tm, D_MODEL), lambda i: (i, 0))
    return pl.pallas_call(
        _final_kernel, grid=(m // tm,),
        in_specs=[row, pl.BlockSpec((D_MODEL, tm), lambda i: (0, i)), _full(g)], out_specs=row,
        out_shape=jax.ShapeDtypeStruct((m, D_MODEL), F32),
        compiler_params=_params(("parallel",)), name="final_norm")(h, out_t, g)


def _row(v):
    return v.reshape(1, -1).astype(F32)


def _cmp_rows(w):
    return jnp.repeat(w.T.astype(F32), HEAD_DIM, axis=1)


def kernel(x_prompt, x_sample, mem_prompt, cache_k_cmp, cache_v_cmp, cache_k_sel, cache_v_sel, cache_k_win, cache_v_win, state_s5_re, state_s5_im, cache_mem_k, cache_mem_v, page_table, g_mix, w_in, w_cmp_k, w_cmp_v, lam_re, lam_im, log_dt, b_re, b_im, c_re, c_im, d_skip, w_glu, g_attn_out, g_ssm_out, w_out, g_x, g_mem, w_xq, w_xk, w_xv, w_xo, g_ffn, w_pq, peer_k1, peer_k2, peer_u, peer_v, g_final):
    nb, seq, _ = x_prompt.shape
    db = x_sample.shape[0]
    depth = g_mix.shape[0]
    assert depth == 1 and x_sample.shape[1] == 1
    l = 0
    mp = nb * seq
    n_gate = 3 * N_HEADS
    kv_end = D_ATTN + 6 * D_KV

    w_attn = jnp.concatenate([w_in[l][:, :kv_end + n_gate], jnp.zeros((D_MODEL, GATE_PAD - n_gate), F32)], axis=1).astype(BF16)
    w_u = w_in[l][:, kv_end + n_gate:].astype(BF16)
    w_full = jnp.concatenate([w_attn, w_u], axis=1)
    w_glu_b = w_glu[l].astype(BF16)
    w_out_a = w_out[l][:D_ATTN].astype(BF16)
    w_out_s = w_out[l][D_ATTN:].astype(BF16)
    w_mem = jnp.concatenate([w_xk[l], w_xv[l]], axis=1).astype(BF16)
    w_xq_b = w_xq[l].astype(BF16)
    w_xo_b = w_xo[l].astype(BF16)
    w_pq_b = w_pq[l].astype(BF16)
    k1_b = peer_k1[l].astype(BF16)
    k2_b = peer_k2[l].astype(BF16)
    u_b = peer_u[l].astype(BF16)
    vt_b = peer_v[l].T.astype(BF16)
    wk_rows = _cmp_rows(w_cmp_k[l])
    wv_rows = _cmp_rows(w_cmp_v[l])
    gm, gx, gf, gfin = _row(g_mix[l]), _row(g_x[l]), _row(g_ffn[l]), _row(g_final)
    ga, gs, gme = _row(g_attn_out[l]), _row(g_ssm_out[l]), _row(g_mem[l])
    s5w = _s5_weights(lam_re[l], lam_im[l], log_dt[l], b_re[l], b_im[l], c_re[l], c_im[l])
    dsk = _row(d_skip[l])

    xp = x_prompt.reshape(mp, D_MODEL)
    q_hm, kv, k_hm, gates, v_t = _proj_attn(xp, gm, w_attn, tm=256)
    u_tm = _proj_u(xp, gm, w_u, nb, tm=256).reshape(seq * nb, D_SSM)
    ck, cv_t = _compress(kv[0], kv[1], wk_rows, wv_rows, nb)
    o_a = _nsa_prompt_t(q_hm, ck, cv_t, k_hm[0], v_t[0], k_hm[1], v_t[1], gates, nb)
    y_tm, p_sr, p_si = _s5_prompt(u_tm, s5w, dsk, nb, tc=256)
    o_s = _glu(y_tm.reshape(seq, nb * D_SSM), w_glu_b, nb, tm=256)
    h1 = _merge(o_a, o_s, ga, gs, w_out_a, w_out_s, xp, tm=256)
    mk, mv = _mem_kv(mem_prompt.reshape(nb * MEM_LEN, D_MODEL), gme, w_mem, tm=256)
    xq = _mm(h1, w_xq_b, 256, BF16, g=gx, name="xattn_q")
    xo = _xattn_prompt(xq, mk, mv, nb, tm=256)
    h2 = _mm(xo, w_xo_b, 256, F32, res=h1, name="xattn_o")
    routed = _peer_route(h2, gf, w_pq_b, k1_b, k2_b, tm=256)
    y_p = _final_norm(h2, _peer(routed[0], u_b, vt_b, *routed[1:], tt=512, et=1024), gfin, tm=256)

    xs = x_sample.reshape(db, D_MODEL)
    z = _mm(xs, w_full, db, F32, g=gm, gate_cols=(kv_end, kv_end + n_gate), name="proj_sample")
    new_rows = [z[:, D_ATTN + D_KV * i:D_ATTN + D_KV * (i + 1)] for i in range(6)]
    n_phys = cache_k_cmp.shape[1]
    pools = [c[l].transpose(0, 2, 3, 1).reshape(n_phys, D_KV, PAGE)
             for c in (cache_k_cmp, cache_v_cmp, cache_k_sel, cache_v_sel)]
    wl = cache_k_win.shape[2]
    buf_kw = cache_k_win[l].transpose(0, 2, 3, 1).reshape(db, D_KV, wl)
    buf_vw = cache_v_win[l].transpose(0, 2, 3, 1).reshape(db, D_KV, wl)
    q_s = (z[:, :D_ATTN] * (HEAD_DIM ** -0.5)).reshape(db, N_HEADS, HEAD_DIM)
    gates_t = z[:, kv_end:kv_end + n_gate].reshape(db, 3, N_HEADS).transpose(0, 2, 1)
    slopes = jnp.asarray(np.array([[_alibi_slope(i)] for i in range(N_HEADS)], np.float32))
    o_a_s = _nsa_sample(page_table, pools, [r.reshape(db, 1, D_KV) for r in new_rows], buf_kw, buf_vw,
                        q_s, gates_t, slopes, wk_rows, wv_rows).reshape(db, D_ATTN)
    y_s, s_sr, s_si = _s5_sample(z[:, kv_end + GATE_PAD:], state_s5_re[l].reshape(db, -1), state_s5_im[l].reshape(db, -1), s5w, dsk)
    o_s_s = _glu(y_s, w_glu_b, 1, tm=db)
    h1s = _merge(o_a_s, o_s_s, ga, gs, w_out_a, w_out_s, xs, tm=db)
    xq_s = _mm(h1s, w_xq_b, db, F32, g=gx, name="xattn_q_sample")
    nt = X_HEAD_DIM // 128

    def mem_rows(c):
        return c.reshape(db, MEM_LEN, X_HEADS, nt, 128).transpose(0, 1, 3, 2, 4).reshape(db, MEM_LEN * nt * X_HEADS, 128)

    xo_s = _xattn_sample(xq_s.reshape(db, 1, D_MODEL), mem_rows(cache_mem_k[l]), mem_rows(cache_mem_v[l])).reshape(db, D_MODEL)
    h2s = _mm(xo_s, w_xo_b, db, F32, res=h1s, name="xattn_o_sample")
    routed_s = _peer_route(h2s, gf, w_pq_b, k1_b, k2_b, tm=db)
    y_s_out = _final_norm(h2s, _peer(routed_s[0], u_b, vt_b, *routed_s[1:], tt=db, et=1024), gfin, tm=db)

    kvshape = (1, nb, seq, N_KV, HEAD_DIM)
    wlp = min(WINDOW, seq)
    p_kv = [a.reshape(kvshape) for a in kv]
    p_win = [a[:, :, seq - wlp:] for a in p_kv[4:6]]
    s_new = [r.reshape(1, db, 1, N_KV, HEAD_DIM) for r in new_rows]
    s_win = [jnp.concatenate([c[l], n[0]], axis=1)[None, :, -min(WINDOW, wl + 1):] for c, n in ((cache_k_win, s_new[4]), (cache_v_win, s_new[5]))]
    g64 = (1, -1, N_SSM_GROUPS, SSM_STATE)
    return (y_p.reshape(nb, seq, D_MODEL), y_s_out.reshape(db, 1, D_MODEL),
            p_kv[0], p_kv[1], p_kv[2], p_kv[3], p_win[0], p_win[1],
            p_sr.reshape(g64), p_si.reshape(g64),
            mk.reshape(1, nb, MEM_LEN, X_HEADS, X_HEAD_DIM), mv.reshape(1, nb, MEM_LEN, X_HEADS, X_HEAD_DIM),
            s_new[0], s_new[1], s_new[2], s_new[3], s_win[0], s_win[1],
            s_sr.reshape(g64), s_si.reshape(g64))
```

```python
import functools
import math

import jax
import jax.numpy as jnp
import numpy as np
from jax import lax
from jax.experimental import pallas as pl
from jax.experimental.pallas import tpu as pltpu

F32 = jnp.float32
BF16 = jnp.bfloat16

D_MODEL = 2048
N_HEADS = 16
HEAD_DIM = 64
N_KV = 4
GQ = 4
D_ATTN = 1024
D_SSM = 1024
D_KV = 256
CMP_BLOCK = 32
CMP_STRIDE = 16
SEL_BLOCK = 64
N_SEL = 8
WINDOW = 512
Q_BLOCK = 128
VT_ROWS = 80
HEADS_PER_LOOP = 4
FORCE = 1e4
N_SSM_GROUPS = 64
SSM_GROUP = 16
SSM_STATE = 64
SSM_BANDS = 4
MEM_LEN = 256
X_HEADS = 4
X_HEAD_DIM = 512
PEER_KEYS = 128
PEER_HEADS = 8
PEER_TOPK = 16
GATE_ROWS = 32
SCORE_ROWS = 128
EMIT_ROWS = 256
PAGE = 128
EPS = 1e-6
NEG = -1e30
TINY = 1e-30
LOWEST = -3.0e38
HUGE = 3.0e38
GATE_PAD = 128
VMEM_LIMIT = 56 * 2**20


def _params(sem, flags=None):
    return pltpu.CompilerParams(dimension_semantics=sem, vmem_limit_bytes=VMEM_LIMIT, flags=flags)


def _full(a):
    nd = a.ndim
    return pl.BlockSpec(a.shape, lambda *_: (0,) * nd)


def _rms(x, g):
    return x * lax.rsqrt(jnp.mean(x * x, axis=-1, keepdims=True) + EPS) * g


def _dot(a, b):
    return jnp.dot(a, b, preferred_element_type=F32)


def _dot_nt(a, b):
    return lax.dot_general(a, b, (((1,), (1,)), ((), ())), preferred_element_type=F32)


def _dot3(a, b_exact):
    hi = a.astype(BF16)
    r1 = a - hi.astype(F32)
    mid = r1.astype(BF16)
    lo = (r1 - mid.astype(F32)).astype(BF16)
    return _dot(hi, b_exact) + _dot(mid, b_exact) + _dot(lo, b_exact)


def _iota(shape, dim):
    return lax.broadcasted_iota(jnp.int32, shape, dim)


def _proj_attn_kernel(x_ref, g_ref, w_ref, q_ref, *rest):
    kv_refs, kh_refs, gate_ref, vt_refs = rest[:6], rest[6:8], rest[8], rest[9:11]
    xn = _rms(x_ref[...], g_ref[...]).astype(BF16)
    z = _dot(xn, w_ref[...])
    for hd in range(N_HEADS):
        q_ref[hd] = (z[:, hd * HEAD_DIM:(hd + 1) * HEAD_DIM] * (HEAD_DIM ** -0.5)).astype(BF16)
    zts = {}
    for k in range(6):
        zk = z[:, D_ATTN + D_KV * k:D_ATTN + D_KV * (k + 1)]
        if k < 2:
            kv_refs[k][...] = zk
        else:
            zts[k] = zk.T
            kv_refs[k][0] = zts[k]
    for kh_ref, k in zip(kh_refs, (2, 4)):
        for h in range(N_KV):
            c0 = D_ATTN + D_KV * k + h * HEAD_DIM
            kh_ref[h] = z[:, c0:c0 + HEAD_DIM].astype(BF16)
    for vt_ref, k in zip(vt_refs, (3, 5)):
        zt = zts[k].astype(BF16)
        for h in range(N_KV):
            vt_ref[h, 0, 0:HEAD_DIM, :] = zt[h * HEAD_DIM:(h + 1) * HEAD_DIM]
            vt_ref[h, 0, HEAD_DIM:, :] = jnp.ones((VT_ROWS - HEAD_DIM, zt.shape[1]), BF16)
    gate_ref[...] = jax.nn.sigmoid(z[:, D_ATTN + 6 * D_KV:])


def _proj_attn(x, g, w, nb, tm):
    m = x.shape[0]
    nt = m // nb // tm
    row = lambda n: pl.BlockSpec((tm, n), lambda i: (i, 0))
    hm = lambda n: pl.BlockSpec((n, tm, HEAD_DIM), lambda i: (0, i, 0))
    ft = pl.BlockSpec((1, D_KV, tm), lambda i: (i // nt, 0, i % nt))
    out_shape = ([jax.ShapeDtypeStruct((N_HEADS, m, HEAD_DIM), BF16)]
                 + [jax.ShapeDtypeStruct((m, D_KV), F32)] * 2
                 + [jax.ShapeDtypeStruct((nb, D_KV, m // nb), F32)] * 4
                 + [jax.ShapeDtypeStruct((N_KV, m, HEAD_DIM), BF16)] * 2
                 + [jax.ShapeDtypeStruct((m, GATE_PAD), F32)]
                 + [jax.ShapeDtypeStruct((N_KV, m // tm, VT_ROWS, tm), BF16)] * 2)
    vt = pl.BlockSpec((N_KV, 1, VT_ROWS, tm), lambda i: (0, i, 0, 0))
    out_specs = [hm(N_HEADS)] + [row(D_KV)] * 2 + [ft] * 4 + [hm(N_KV)] * 2 + [row(GATE_PAD)] + [vt] * 2
    outs = pl.pallas_call(
        _proj_attn_kernel, grid=(m // tm,),
        in_specs=[row(D_MODEL), _full(g), _full(w)],
        out_specs=out_specs, out_shape=out_shape,
        compiler_params=_params(("parallel",)), name="proj_attn")(x, g, w)
    return outs[0], outs[1:7], outs[7:9], outs[9], outs[10:12]


def _proj_u_kernel(x_ref, g_ref, w_ref, u_ref):
    xn = _rms(x_ref[...], g_ref[...]).astype(BF16)
    u_ref[...] = _dot(xn, w_ref[...])


def _proj_u(x, g, w, nb, tm):
    m = x.shape[0]
    t = m // nb
    nt = t // tm
    return pl.pallas_call(
        _proj_u_kernel, grid=(nb, nt),
        in_specs=[pl.BlockSpec((tm, D_MODEL), lambda b, i: (b * nt + i, 0)), _full(g), _full(w)],
        out_specs=pl.BlockSpec((tm, D_SSM), lambda b, i: (i, b)),
        out_shape=jax.ShapeDtypeStruct((t, nb * D_SSM), F32),
        compiler_params=_params(("parallel", "parallel")), name="proj_u")(x, g, w)


def _mm_kernel(*refs, norm, res, gate_cols):
    it = iter(refs)
    x_ref = next(it)
    g_ref = next(it) if norm else None
    w_ref = next(it)
    r_ref = next(it) if res else None
    o_ref = next(it)
    x = x_ref[...]
    if norm:
        x = _rms(x.astype(F32), g_ref[...])
    z = _dot(x.astype(BF16), w_ref[...])
    if res:
        z = z + r_ref[...]
    if gate_cols is not None:
        col = _iota(z.shape, 1)
        z = jnp.where((col >= gate_cols[0]) & (col < gate_cols[1]), jax.nn.sigmoid(z), z)
    o_ref[...] = z.astype(o_ref.dtype)


def _mm(x, w, tm, out_dtype, g=None, res=None, gate_cols=None, name="mm"):
    m, k = x.shape
    n = w.shape[1]
    row = lambda c: pl.BlockSpec((tm, c), lambda i: (i, 0))
    args, specs = [x], [row(k)]
    if g is not None:
        args.append(g)
        specs.append(_full(g))
    args.append(w)
    specs.append(_full(w))
    if res is not None:
        args.append(res)
        specs.append(row(n))
    return pl.pallas_call(
        functools.partial(_mm_kernel, norm=g is not None, res=res is not None, gate_cols=gate_cols),
        grid=(m // tm,), in_specs=specs, out_specs=row(n),
        out_shape=jax.ShapeDtypeStruct((m, n), out_dtype),
        compiler_params=_params(("parallel",)), name=name)(*args)


def _mm2_kernel(x_ref, g_ref, w_ref, o0_ref, o1_ref):
    xn = _rms(x_ref[...], g_ref[...]).astype(BF16)
    z = _dot(xn, w_ref[...])
    n = o0_ref.shape[1]
    o0_ref[...] = z[:, :n]
    o1_ref[...] = z[:, n:]


def _mem_kv(mem, g, w, tm):
    m = mem.shape[0]
    row = pl.BlockSpec((tm, D_MODEL), lambda i: (i, 0))
    return pl.pallas_call(
        _mm2_kernel, grid=(m // tm,), in_specs=[row, _full(g), _full(w)], out_specs=[row, row],
        out_shape=[jax.ShapeDtypeStruct((m, D_MODEL), F32)] * 2,
        compiler_params=_params(("parallel",)), name="mem_kv")(mem, g, w)


def _glu_kernel(y_ref, w_ref, o_ref):
    y = y_ref[...]
    o_ref[...] = y * jax.nn.sigmoid(_dot(y.astype(BF16), w_ref[...]))


def _glu(y_tm, w, nb, tm):
    t = y_tm.shape[0]
    nt = t // tm
    return pl.pallas_call(
        _glu_kernel, grid=(nb, nt),
        in_specs=[pl.BlockSpec((tm, D_SSM), lambda b, i: (i, b)), _full(w)],
        out_specs=pl.BlockSpec((tm, D_SSM), lambda b, i: (b * nt + i, 0)),
        out_shape=jax.ShapeDtypeStruct((nb * t, D_SSM), F32),
        compiler_params=_params(("parallel", "parallel")), name="glu")(y_tm, w)


def _merge_kernel(oa_ref, os_ref, ga_ref, gs_ref, wa_ref, ws_ref, x_ref, o_ref):
    a = _rms(oa_ref[...], ga_ref[...]).astype(BF16)
    s = _rms(os_ref[...], gs_ref[...]).astype(BF16)
    o_ref[...] = x_ref[...] + (_dot(a, wa_ref[...]) + _dot(s, ws_ref[...]))


def _merge(o_a, o_s, g_a, g_s, w_a, w_s, x, tm):
    m = x.shape[0]
    half = pl.BlockSpec((tm, D_ATTN), lambda i: (i, 0))
    row = pl.BlockSpec((tm, D_MODEL), lambda i: (i, 0))
    return pl.pallas_call(
        _merge_kernel, grid=(m // tm,),
        in_specs=[half, half, _full(g_a), _full(g_s), _full(w_a), _full(w_s), row],
        out_specs=row, out_shape=jax.ShapeDtypeStruct((m, D_MODEL), F32),
        compiler_params=_params(("parallel",)), name="merge_heads")(o_a, o_s, g_a, g_s, w_a, w_s, x)


def _pool16(x_refs, w_ref, tokens):
    halves = []
    for hf, x_ref in enumerate(x_refs):
        lanes = slice(hf * 128, (hf + 1) * 128)
        a = b = None
        for j in range(CMP_STRIDE):
            xj = x_ref[pl.ds(j, tokens // CMP_STRIDE, stride=CMP_STRIDE), :]
            ta = xj * w_ref[j:j + 1, lanes]
            tb = xj * w_ref[CMP_STRIDE + j:CMP_STRIDE + j + 1, lanes]
            a = ta if a is None else a + ta
            b = tb if b is None else b + tb
        halves.append((a, b))
    return (jnp.concatenate([halves[0][0], halves[1][0]], axis=1),
            jnp.concatenate([halves[0][1], halves[1][1]], axis=1))


def _shift_up(b, last_row):
    n = b.shape[0]
    rolled = pltpu.roll(b, n - 1, 0)
    return jnp.where(_iota(b.shape, 0) == n - 1, last_row, rolled)


def _compress_kernel(k0_ref, k1_ref, v0_ref, v1_ref, wk_ref, wv_ref, ck_ref, cv_ref):
    tokens = k0_ref.shape[0]
    a, b = _pool16((k0_ref, k1_ref), wk_ref, tokens)
    c = (a + _shift_up(b, 0.0)).astype(BF16)
    for h in range(N_KV):
        ck_ref[0, h] = c[:, h * HEAD_DIM:(h + 1) * HEAD_DIM]
    a, b = _pool16((v0_ref, v1_ref), wv_ref, tokens)
    ct = (a + _shift_up(b, 0.0)).T.astype(BF16)
    for h in range(N_KV):
        cv_ref[0, h] = ct[h * HEAD_DIM:(h + 1) * HEAD_DIM]


def _compress(kc, vc, wk, wv, nb):
    s = kc.shape[0] // nb
    nc = s // CMP_STRIDE
    lo = pl.BlockSpec((s, D_KV // 2), lambda b: (b, 0))
    hi = pl.BlockSpec((s, D_KV // 2), lambda b: (b, 1))
    out = pl.BlockSpec((1, N_KV, nc, HEAD_DIM), lambda b: (b, 0, 0, 0))
    out_t = pl.BlockSpec((1, N_KV, HEAD_DIM, nc), lambda b: (b, 0, 0, 0))
    return pl.pallas_call(
        _compress_kernel, grid=(nb,), in_specs=[lo, hi, lo, hi, _full(wk), _full(wv)], out_specs=[out, out_t],
        out_shape=[jax.ShapeDtypeStruct((nb, N_KV, nc, HEAD_DIM), BF16),
                   jax.ShapeDtypeStruct((nb, N_KV, HEAD_DIM, nc), BF16)],
        compiler_params=_params(("parallel",)), name="nsa_compress")(kc, kc, vc, vc, wk, wv)


def _overlap_matrix(nc, width, n_cmp, n_sel):
    i = _iota((nc, width), 0)
    j = _iota((nc, width), 1)
    lo = jnp.maximum(i * CMP_STRIDE, j * SEL_BLOCK)
    hi = jnp.minimum(i * CMP_STRIDE + CMP_BLOCK, (j + 1) * SEL_BLOCK)
    ov = jnp.maximum(hi - lo, 0).astype(F32) * (1.0 / CMP_BLOCK)
    return jnp.where((i < n_cmp) & (j < n_sel), ov, 0.0).astype(BF16)


def _select_blocks(imp, qpos, n_sel, axis=1):
    blk = _iota(imp.shape, axis)
    valid = blk * SEL_BLOCK <= qpos
    forced = (blk == qpos // SEL_BLOCK) | (blk == 0)
    x = jnp.where(valid, imp + jnp.where(forced, FORCE, 0.0), NEG)
    x = jnp.where(blk < n_sel, x, LOWEST)
    sel = jnp.zeros(imp.shape, F32)
    blk_f = blk.astype(F32)
    for _ in range(N_SEL):
        m = jnp.max(x, axis=axis, keepdims=True)
        first = jnp.min(jnp.where(x == m, blk_f, 4.0 * imp.shape[axis]), axis=axis, keepdims=True)
        pick = blk_f == first
        sel = jnp.where(pick & (m > 0.5 * NEG), 1.0, sel)
        x = jnp.where(pick, LOWEST, x)
    return sel


M_INIT = 0.1 * NEG


def _alibi_slope(head):
    return float(2.0 ** (-8.0 * (head + 1) / N_HEADS))


def _nsa_prompt_t_kernel(q_ref, ck_ref, cvt_ref, ks_ref, vst_ref, kw_ref, vwt_ref, gate_ref, o_ref, *, kt, nc):
    qi = pl.program_id(1)
    cols = GQ * Q_BLOCK
    q0 = qi * Q_BLOCK
    qpos = q0 + _iota((1, Q_BLOCK), 1)
    n_sel = ks_ref.shape[1] // SEL_BLOCK
    r_sel = -(-n_sel // 8) * 8
    bi = _iota((r_sel, nc), 0)
    ci = _iota((r_sel, nc), 1)
    ov = jnp.maximum(jnp.minimum(ci * CMP_STRIDE + CMP_BLOCK, (bi + 1) * SEL_BLOCK)
                     - jnp.maximum(ci * CMP_STRIDE, bi * SEL_BLOCK), 0).astype(F32) * (1.0 / CMP_BLOCK)
    ov = jnp.where((ci < nc - 1) & (bi < n_sel), ov, 0.0).astype(BF16)
    cend = _iota((nc, Q_BLOCK), 0) * CMP_STRIDE + (CMP_BLOCK - 1)
    cend_f = cend.astype(F32)
    mask_c = qpos >= cend
    n_hi = (q0 + Q_BLOCK + kt - 1) // kt
    w_lo = jnp.maximum(q0 - (WINDOW - 1), 0) // kt
    gates_t = gate_ref[...].T
    lanes = lambda g: slice(g * Q_BLOCK, (g + 1) * Q_BLOCK)

    def prepare(h):
        q = q_ref[h * GQ:(h + 1) * GQ].reshape(cols, HEAD_DIM)
        s = _dot_nt(ck_ref[0, h], q)
        ps, p_grp = [], None
        for g in range(GQ):
            sg = jnp.where(mask_c, s[:, lanes(g)] + _alibi_slope(h * GQ + g) * cend_f, NEG)
            p = jnp.where(mask_c, jnp.exp(sg - jnp.max(sg, axis=0, keepdims=True)), 0.0)
            p = p / jnp.maximum(jnp.sum(p, axis=0, keepdims=True), TINY)
            ps.append(p.astype(BF16))
            p_grp = p if p_grp is None else p_grp + p
        o_c = _dot(cvt_ref[0, h], jnp.concatenate(ps, axis=1))
        hi = p_grp.astype(BF16)
        r1 = p_grp - hi.astype(F32)
        mid = r1.astype(BF16)
        lo = (r1 - mid.astype(F32)).astype(BF16)
        imp = _dot(ov, hi) + _dot(ov, mid) + _dot(ov, lo)
        sel = _select_blocks(imp, qpos, n_sel, axis=0).astype(BF16)
        return q, sel, o_c

    def tile(t, carry, h, q, sel, k_ref, vt_ref):
        m, acc = carry
        k0 = pl.multiple_of(t * kt, kt)
        kpos = k0 + _iota((kt, Q_BLOCK), 0)
        kpos_f = kpos.astype(F32)
        s = _dot_nt(k_ref[h, pl.ds(k0, kt), :], q)
        ok = kpos <= qpos
        if sel is not None:
            expand = (k0 + _iota((kt, r_sel), 0)) // SEL_BLOCK == _iota((kt, r_sel), 1)
            ok = ok & (_dot(jnp.where(expand, 1.0, 0.0).astype(BF16), sel) > 0.5)
        else:
            ok = ok & (qpos - kpos < WINDOW)
        s = jnp.concatenate([jnp.where(ok, s[:, lanes(g)] + _alibi_slope(h * GQ + g) * kpos_f, NEG)
                             for g in range(GQ)], axis=1)
        m_new = jnp.maximum(m, jnp.max(s, axis=0, keepdims=True))
        p = jnp.exp(s - m_new).astype(BF16)
        acc = jnp.exp(m - m_new) * acc + _dot(vt_ref[h, t], p)
        return m_new, acc

    outs = [None] * N_HEADS
    for h0 in range(0, N_KV, HEADS_PER_LOOP):
        heads = range(h0, h0 + HEADS_PER_LOOP)
        prep = [prepare(h) for h in heads]

        def sel_tiles(t, carry):
            return [tile(t, c, h, q, sel, ks_ref, vst_ref) for c, h, (q, sel, _) in zip(carry, heads, prep)]

        def win_tiles(t, carry):
            return [tile(t, c, h, q, None, kw_ref, vwt_ref) for c, h, (q, _, _) in zip(carry, heads, prep)]

        init = [(jnp.full((1, cols), M_INIT, F32), jnp.zeros((VT_ROWS, cols), F32)) for _ in heads]
        sel_c = lax.fori_loop(0, w_lo, sel_tiles, init)
        sel_c, win_c = lax.fori_loop(w_lo, n_hi, lambda t, c: (sel_tiles(t, c[0]), win_tiles(t, c[1])), (sel_c, init))
        for h, (_, _, o_c), (_, a_s), (_, a_w) in zip(heads, prep, sel_c, win_c):
            o_s = a_s[0:HEAD_DIM] / jnp.maximum(a_s[HEAD_DIM:HEAD_DIM + 1], TINY)
            o_w = a_w[0:HEAD_DIM] / jnp.maximum(a_w[HEAD_DIM:HEAD_DIM + 1], TINY)
            for g in range(GQ):
                hd = h * GQ + g
                outs[hd] = (gates_t[hd:hd + 1] * o_c[:, lanes(g)] + gates_t[N_HEADS + hd:N_HEADS + hd + 1] * o_s[:, lanes(g)]
                            + gates_t[2 * N_HEADS + hd:2 * N_HEADS + hd + 1] * o_w[:, lanes(g)])
    o_ref[...] = jnp.concatenate(outs, axis=0).T


def _nsa_prompt_t(q_hm, ck, cvt, ks, vst, kw, vwt, gates, nb):
    m = gates.shape[0]
    s = m // nb
    nq = s // Q_BLOCK
    nc = ck.shape[2]
    kt = vst.shape[3]
    kv = pl.BlockSpec((N_KV, s, HEAD_DIM), lambda b, i: (0, b, 0))
    vt = pl.BlockSpec((N_KV, s // kt, VT_ROWS, kt), lambda b, i: (0, b, 0, 0))
    return pl.pallas_call(
        functools.partial(_nsa_prompt_t_kernel, kt=kt, nc=nc), grid=(nb, nq),
        in_specs=[pl.BlockSpec((N_HEADS, Q_BLOCK, HEAD_DIM), lambda b, i: (0, b * nq + i, 0)),
                  pl.BlockSpec((1, N_KV, nc, HEAD_DIM), lambda b, i: (b, 0, 0, 0)),
                  pl.BlockSpec((1, N_KV, HEAD_DIM, nc), lambda b, i: (b, 0, 0, 0)),
                  kv, vt, kv, vt,
                  pl.BlockSpec((Q_BLOCK, GATE_PAD), lambda b, i: (b * nq + i, 0))],
        out_specs=pl.BlockSpec((Q_BLOCK, D_ATTN), lambda b, i: (b * nq + i, 0)),
        out_shape=jax.ShapeDtypeStruct((m, D_ATTN), F32),
        compiler_params=_params(("parallel", "parallel")), name="nsa_prompt")(q_hm, ck, cvt, ks, vst, kw, vwt, gates)


def _nsa_sample_kernel(pt_ref, *refs, n_pages, past):
    del pt_ref
    it = iter(refs)
    pools = [[next(it) for _ in range(n_pages)] for _ in range(4)]
    new_ref = next(it)
    new = [new_ref.at[0, pl.ds(i, 1)] for i in range(6)]
    bkw_ref, bvw_ref, q_ref, gate_ref, slope_ref, wk_ref, wv_ref, o_ref = [next(it) for _ in range(8)]
    kbuf, vbuf = next(it), next(it)
    kwbuf, vwbuf = next(it), next(it)
    tok = next(it), next(it)
    nc = past // CMP_STRIDE
    n_sel = past // SEL_BLOCK + 1
    rows = N_HEADS
    slope = slope_ref[...]
    own = _iota((rows, D_KV), 1) // HEAD_DIM == _iota((rows, D_KV), 0) // GQ
    q = jnp.where(own, jnp.concatenate([q_ref[0]] * N_KV, axis=1), 0.0)
    qb = q.astype(BF16)

    def new_row(i):
        return new[i][...].astype(BF16).astype(F32)

    def own_heads(o):
        o = jnp.where(own, o, 0.0)
        return o[:, 0:64] + o[:, 64:128] + o[:, 128:192] + o[:, 192:256]

    def pooled(pages, w_ref, x_new):
        for i, p_ref in enumerate(pages):
            pt = p_ref[0].T
            tok[0][i * PAGE:(i + 1) * PAGE, :] = pt[:, 0:128]
            tok[1][i * PAGE:(i + 1) * PAGE, :] = pt[:, 128:256]
        a, b = _pool16(tok, w_ref, past)
        return (a + _shift_up(b, w_ref[CMP_STRIDE:CMP_STRIDE + 1, :] * x_new[...])).astype(BF16)

    def fill(pages, kb, vb):
        for i, (kp, vp) in enumerate(pages):
            kb[:, i * kp.shape[1]:(i + 1) * kp.shape[1]] = kp[...].astype(BF16)
            vb[:, i * vp.shape[1]:(i + 1) * vp.shape[1]] = vp[...].astype(BF16)

    def attend(kb, vb, k_new, v_new, mask, new_ok, dist):
        s = jnp.where(mask, _dot(qb, kb[...]) - slope * dist.astype(F32), NEG)
        s_new = jnp.where(new_ok, jnp.sum(qb.astype(F32) * k_new, axis=-1, keepdims=True), NEG)
        m = jnp.maximum(jnp.max(s, axis=-1, keepdims=True), s_new)
        p = jnp.where(mask, jnp.exp(s - m), 0.0)
        p_new = jnp.where(new_ok, jnp.exp(s_new - m), 0.0)
        l = jnp.sum(p, axis=-1, keepdims=True) + p_new
        o = _dot_nt(p.astype(BF16), vb[...]) + p_new.astype(BF16).astype(F32) * v_new
        return own_heads(o) / jnp.maximum(l, TINY)

    ck = pooled(pools[0], wk_ref, new[0])
    dist_c = past - (_iota((1, nc), 1) * CMP_STRIDE + (CMP_BLOCK - 1))
    mask_c = dist_c >= 0
    s = jnp.where(mask_c, _dot_nt(qb, ck) - slope * dist_c.astype(F32), NEG)
    p = jnp.where(mask_c, jnp.exp(s - jnp.max(s, axis=-1, keepdims=True)), 0.0)
    p = p / jnp.maximum(jnp.sum(p, axis=-1, keepdims=True), TINY)
    p_grp = jnp.concatenate(
        [jnp.broadcast_to(jnp.sum(p[h * GQ:(h + 1) * GQ], axis=0, keepdims=True), (GQ, nc)) for h in range(N_KV)], axis=0)
    imp = _dot3(p_grp, _overlap_matrix(nc, 128, nc, n_sel))
    sel = _select_blocks(imp, jnp.full((rows, 1), past, jnp.int32), n_sel)
    o_c = own_heads(_dot(p.astype(BF16), pooled(pools[1], wv_ref, new[1])))
    wl = bkw_ref.shape[2]
    dist_w = wl - _iota((1, wl), 1)
    fill([(bkw_ref.at[0], bvw_ref.at[0])], kwbuf, vwbuf)
    o_w = attend(kwbuf, vwbuf, new_row(4), new_row(5), (dist_w >= 0) & (dist_w < WINDOW), jnp.full((rows, 1), True), dist_w)
    fill([(k.at[0], v.at[0]) for k, v in zip(pools[2], pools[3])], kbuf, vbuf)
    dist_s = past - _iota((1, past), 1)
    expand = _iota((128, past), 0) == _iota((128, past), 1) // SEL_BLOCK
    hit = _dot(sel.astype(BF16), jnp.where(expand, 1.0, 0.0).astype(BF16))
    o_s = attend(kbuf, vbuf, new_row(2), new_row(3), (hit > 0.5) & (dist_s >= 0), sel[:, n_sel - 1:n_sel] > 0.5, dist_s)
    gates = gate_ref[0]
    o_ref[0] = gates[:, 0:1] * o_c + gates[:, 1:2] * o_s + gates[:, 2:3] * o_w


def _nsa_sample(page_table, pools, new_rows, buf_kw, buf_vw, q, gates_t, slopes, wk, wv):
    nb, n_pages = page_table.shape
    past = n_pages * PAGE
    page_specs = []
    for pool in pools:
        for p in range(n_pages):
            page_specs.append(pl.BlockSpec((1,) + pool.shape[1:], lambda b, pt, p=p: (pt[b, p], 0, 0)))
    per_b = lambda shape: pl.BlockSpec((1,) + shape, lambda b, pt: (b,) + (0,) * len(shape))
    in_specs = (page_specs + [per_b(new_rows.shape[1:])] + [per_b(buf_kw.shape[1:])] * 2
                + [per_b((N_HEADS, HEAD_DIM)), per_b((N_HEADS, 3)),
                   pl.BlockSpec(slopes.shape, lambda b, pt: (0, 0)),
                   pl.BlockSpec(wk.shape, lambda b, pt: (0, 0)), pl.BlockSpec(wv.shape, lambda b, pt: (0, 0))])
    args = [pool for pool in pools for _ in range(n_pages)] + [new_rows, buf_kw, buf_vw, q, gates_t, slopes, wk, wv]
    return pl.pallas_call(
        functools.partial(_nsa_sample_kernel, n_pages=n_pages, past=past),
        grid_spec=pltpu.PrefetchScalarGridSpec(
            num_scalar_prefetch=1, grid=(nb,), in_specs=in_specs,
            out_specs=per_b((N_HEADS, HEAD_DIM)),
            scratch_shapes=([pltpu.VMEM((D_KV, past), BF16)] * 2 + [pltpu.VMEM((D_KV, buf_kw.shape[2]), BF16)] * 2
                            + [pltpu.VMEM((past, D_KV // 2), F32)] * 2)),
        out_shape=jax.ShapeDtypeStruct((nb, N_HEADS, HEAD_DIM), F32),
        compiler_params=_params(("arbitrary",)), name="nsa_sample")(page_table, *args)


def _s5_disc_kernel(lr_ref, li_ref, ldt_ref, ar_ref, ai_ref, fr_ref, fi_ref):
    lr, li = lr_ref[...], li_ref[...]
    dt = jnp.exp(ldt_ref[...])
    mag = jnp.exp(lr * dt)
    ar = mag * jnp.cos(li * dt)
    ai = mag * jnp.sin(li * dt)
    den = lr * lr + li * li
    ar_ref[...] = ar
    ai_ref[...] = ai
    fr_ref[...] = ((ar - 1.0) * lr + ai * li) / den
    fi_ref[...] = (ai * lr - (ar - 1.0) * li) / den


def _s5_bbar_kernel(fr_ref, fi_ref, br_ref, bi_ref, or_ref, oi_ref):
    fr, fi, br, bi = fr_ref[...], fi_ref[...], br_ref[...], bi_ref[...]
    or_ref[...] = fr * br - fi * bi
    oi_ref[...] = fr * bi + fi * br


def _s5_weights(lam_re, lam_im, log_dt, b_re, b_im, c_re, c_im):
    g, n = lam_re.shape
    sd = jax.ShapeDtypeStruct((g, n), F32)
    ar, ai, fr, fi = pl.pallas_call(_s5_disc_kernel, out_shape=[sd] * 4, name="s5_discretise")(
        lam_re, lam_im, log_dt.reshape(g, 1))
    sb = jax.ShapeDtypeStruct((g * n, SSM_GROUP), F32)
    bbr, bbi = pl.pallas_call(_s5_bbar_kernel, out_shape=[sb] * 2, name="s5_bbar")(
        fr.reshape(g * n, 1), fi.reshape(g * n, 1), b_re.reshape(g * n, SSM_GROUP), b_im.reshape(g * n, SSM_GROUP))
    eye = jnp.eye(g // SSM_BANDS, dtype=F32)
    gl = g // SSM_BANDS

    def band_in(bb):
        x = bb.reshape(SSM_BANDS, gl, n, SSM_GROUP).transpose(0, 1, 3, 2)
        return jnp.einsum("jgpn,gh->jgphn", x, eye).reshape(SSM_BANDS, gl * SSM_GROUP, gl * n).astype(BF16)

    def band_out(c):
        x = c.reshape(SSM_BANDS, gl, SSM_GROUP, n).transpose(0, 1, 3, 2)
        return jnp.einsum("jgnp,gh->jgnhp", x, eye).reshape(SSM_BANDS, gl * n, gl * SSM_GROUP).astype(BF16)

    return (ar.reshape(1, g * n), ai.reshape(1, g * n), band_in(bbr), band_in(bbi), band_out(c_re), band_out(-c_im))


def _s5_prompt_kernel(u_ref, wbr_ref, wbi_ref, ar_ref, ai_ref, wcr_ref, wci_ref, d_ref,
                      y_ref, sr_ref, si_ref, hr_s, hi_s, cr_s, ci_s, *, nb):
    c = pl.program_id(1)
    rows, width = hr_s.shape
    rep = 8 // nb

    @pl.when(c == 0)
    def _():
        cr_s[...] = jnp.zeros_like(cr_s)
        ci_s[...] = jnp.zeros_like(ci_s)

    u = u_ref[...]
    ub = u.astype(BF16)
    hr_s[...] = _dot(ub, wbr_ref[0])
    hi_s[...] = _dot(ub, wbi_ref[0])
    ar = jnp.broadcast_to(ar_ref[...], (8, width))
    ai = jnp.broadcast_to(ai_ref[...], (8, width))
    sub = _iota((8, width), 0) // nb

    def step(i, carry):
        sr, si = carry
        base = pl.multiple_of(i * 8, 8)
        xr = hr_s[pl.ds(base, 8), :]
        xi = hi_s[pl.ds(base, 8), :]
        outr = outi = None
        for k in range(rep):
            yr = ar * sr - ai * si + xr
            yi = ar * si + ai * sr + xi
            outr = yr if k == 0 else jnp.where(sub == k, yr, outr)
            outi = yi if k == 0 else jnp.where(sub == k, yi, outi)
            sr, si = pltpu.roll(yr, nb, 0), pltpu.roll(yi, nb, 0)
        hr_s[pl.ds(base, 8), :] = outr
        hi_s[pl.ds(base, 8), :] = outi
        return sr, si

    sr, si = lax.fori_loop(0, rows // 8, step, (cr_s[...], ci_s[...]))
    cr_s[...] = sr
    ci_s[...] = si
    y = _dot(hr_s[...].astype(BF16), wcr_ref[0]) + _dot(hi_s[...].astype(BF16), wci_ref[0]) + d_ref[...] * u
    y_ref[...] = jax.nn.gelu(y)

    @pl.when(c == pl.num_programs(1) - 1)
    def _():
        sr_ref[...] = sr[0:nb]
        si_ref[...] = si[0:nb]


def _s5_prompt(u_tm, weights, d_skip, nb, tc):
    ar, ai, wbr, wbi, wcr, wci = weights
    rows = u_tm.shape[0]
    cw = D_SSM // SSM_BANDS
    sw = ar.shape[1] // SSM_BANDS
    blk = tc * nb
    tile = pl.BlockSpec((blk, cw), lambda j, c: (c, j))
    band = lambda a: pl.BlockSpec((1,) + a.shape[1:], lambda j, c: (j, 0, 0))
    vec = lambda w: pl.BlockSpec((1, w), lambda j, c: (0, j))
    state = pl.BlockSpec((nb, sw), lambda j, c: (0, j))
    return pl.pallas_call(
        functools.partial(_s5_prompt_kernel, nb=nb), grid=(SSM_BANDS, rows // blk),
        in_specs=[tile, band(wbr), band(wbi), vec(sw), vec(sw), band(wcr), band(wci), vec(cw)],
        out_specs=[tile, state, state],
        out_shape=[jax.ShapeDtypeStruct((rows, D_SSM), F32)] + [jax.ShapeDtypeStruct((nb, ar.shape[1]), F32)] * 2,
        scratch_shapes=[pltpu.VMEM((blk, sw), F32)] * 2 + [pltpu.VMEM((8, sw), F32)] * 2,
        compiler_params=_params(("parallel", "arbitrary")), name="s5_prompt")(
            u_tm, wbr, wbi, ar, ai, wcr, wci, d_skip)


def _s5_sample_kernel(u_ref, h0r_ref, h0i_ref, wbr_ref, wbi_ref, ar_ref, ai_ref, wcr_ref, wci_ref, d_ref,
                      y_ref, sr_ref, si_ref):
    cw = D_SSM // SSM_BANDS
    sw = ar_ref.shape[1] // SSM_BANDS
    for j in range(SSM_BANDS):
        cs = slice(j * cw, (j + 1) * cw)
        ss = slice(j * sw, (j + 1) * sw)
        u = u_ref[:, cs]
        ub = u.astype(BF16)
        ar, ai = ar_ref[:, ss], ai_ref[:, ss]
        h0r, h0i = h0r_ref[:, ss], h0i_ref[:, ss]
        hr = _dot(ub, wbr_ref[j]) + (ar * h0r - ai * h0i)
        hi = _dot(ub, wbi_ref[j]) + (ar * h0i + ai * h0r)
        sr_ref[:, ss] = hr
        si_ref[:, ss] = hi
        y = _dot(hr.astype(BF16), wcr_ref[j]) + _dot(hi.astype(BF16), wci_ref[j]) + d_ref[:, cs] * u
        y_ref[:, cs] = jax.nn.gelu(y)


def _s5_sample(u, h0r, h0i, weights, d_skip):
    ar, ai, wbr, wbi, wcr, wci = weights
    nb = u.shape[0]
    st = jax.ShapeDtypeStruct(h0r.shape, F32)
    return pl.pallas_call(
        _s5_sample_kernel, out_shape=[jax.ShapeDtypeStruct((nb, D_SSM), F32), st, st],
        compiler_params=pltpu.CompilerParams(vmem_limit_bytes=VMEM_LIMIT), name="s5_sample")(
            u, h0r, h0i, wbr, wbi, ar, ai, wcr, wci, d_skip)


def _xattn_prompt_kernel(q_ref, k_ref, v_ref, o_ref):
    scale = X_HEAD_DIM ** -0.5
    for h in range(X_HEADS):
        cs = slice(h * X_HEAD_DIM, (h + 1) * X_HEAD_DIM)
        s = _dot_nt(q_ref[:, cs], k_ref[:, cs].astype(BF16)) * scale
        p = jnp.exp(s - jnp.max(s, axis=-1, keepdims=True))
        p = p / jnp.sum(p, axis=-1, keepdims=True)
        o_ref[:, cs] = _dot(p.astype(BF16), v_ref[:, cs].astype(BF16)).astype(o_ref.dtype)


def _xattn_prompt(q, mk, mv, nb, tm):
    m = q.shape[0]
    nt = m // nb // tm
    mem = pl.BlockSpec((MEM_LEN, D_MODEL), lambda b, i: (b, 0))
    row = pl.BlockSpec((tm, D_MODEL), lambda b, i: (b * nt + i, 0))
    return pl.pallas_call(
        _xattn_prompt_kernel, grid=(nb, nt), in_specs=[row, mem, mem], out_specs=row,
        out_shape=jax.ShapeDtypeStruct((m, D_MODEL), BF16),
        compiler_params=_params(("parallel", "parallel")), name="xattn_prompt")(q, mk, mv)


def _xattn_sample_kernel(q_ref, k_ref, v_ref, o_ref):
    scale = X_HEAD_DIM ** -0.5
    nt = X_HEAD_DIM // 128
    rows = MEM_LEN * nt * X_HEADS

    def head(ref, h):
        parts = [ref[0, pl.ds(t * X_HEADS + h, MEM_LEN, stride=nt * X_HEADS), :] for t in range(nt)]
        return jnp.concatenate(parts, axis=1).astype(BF16)

    assert k_ref.shape[1] == rows
    for h in range(X_HEADS):
        cs = slice(h * X_HEAD_DIM, (h + 1) * X_HEAD_DIM)
        q = jnp.broadcast_to(q_ref[0, :, cs], (8, X_HEAD_DIM)).astype(BF16)
        s = _dot_nt(q, head(k_ref, h)) * scale
        p = jnp.exp(s - jnp.max(s, axis=-1, keepdims=True))
        p = p / jnp.sum(p, axis=-1, keepdims=True)
        o_ref[0, :, cs] = _dot(p.astype(BF16), head(v_ref, h))[0:1].astype(o_ref.dtype)


def _xattn_sample(q, mk, mv):
    nb = q.shape[0]
    mem = pl.BlockSpec((1,) + mk.shape[1:], lambda b: (b, 0, 0))
    row = pl.BlockSpec((1, 1, D_MODEL), lambda b: (b, 0, 0))
    return pl.pallas_call(
        _xattn_sample_kernel, grid=(nb,), in_specs=[row, mem, mem], out_specs=row,
        out_shape=jax.ShapeDtypeStruct((nb, 1, D_MODEL), BF16),
        compiler_params=_params(("parallel",)), name="xattn_sample")(q, mk, mv)


def _top_distinct(x, n):
    rows = []
    for _ in range(n):
        m = jnp.max(x, axis=0, keepdims=True)
        rows.append(m)
        x = jnp.where(x == m, LOWEST, x)
    return jnp.concatenate(rows, axis=0)


def _peer_route_kernel(x_ref, g_ref, w_ref, k1_ref, k2_ref, xt_ref, thr_ref, s2_ref, e2_ref, c1_ref):
    xn = _rms(x_ref[...], g_ref[...])
    xt_ref[...] = xn.T.astype(BF16)
    q = _dot(xn.astype(BF16), w_ref[...]).astype(BF16)
    half = PEER_KEYS
    for h in range(PEER_HEADS):
        s1 = _dot_nt(k1_ref[...], q[:, 2 * half * h:2 * half * h + half])
        s2 = _dot_nt(k2_ref[...], q[:, 2 * half * h + half:2 * half * (h + 1)])
        d1 = _top_distinct(s1, PEER_TOPK)
        d2 = _top_distinct(s2, PEER_TOPK)
        cand = jnp.concatenate([d1[a:a + 1] + d2[0:PEER_TOPK // (a + 1)] for a in range(PEER_TOPK)], axis=0)
        tau = _top_distinct(cand, PEER_TOPK)[PEER_TOPK - 1:PEER_TOPK]
        top = d1[0:1] + d2[0:1]
        z = jnp.sum(jnp.where(cand >= tau, jnp.exp(cand - top), 0.0), axis=0, keepdims=True)
        thr = jnp.full(s1.shape, HUGE, F32)
        for a in range(PEER_TOPK):
            ok = (d1[a:a + 1] + d2) >= tau
            thr_a = jnp.min(jnp.where(ok, d2, HUGE), axis=0, keepdims=True)
            thr = jnp.where(s1 == d1[a:a + 1], thr_a, thr)
        thr_ref[h] = thr
        s2_ref[h] = s2
        e2_ref[h] = jnp.exp(s2 - d2[0:1])
        c1_ref[h] = jnp.exp(s1 - d1[0:1]) / z


def _peer_route(x, g, w_pq, k1, k2, tm):
    m = x.shape[0]
    hk = pl.BlockSpec((PEER_HEADS, PEER_KEYS, tm), lambda i: (0, 0, i))
    hks = jax.ShapeDtypeStruct((PEER_HEADS, PEER_KEYS, m), F32)
    return pl.pallas_call(
        _peer_route_kernel, grid=(m // tm,),
        in_specs=[pl.BlockSpec((tm, D_MODEL), lambda i: (i, 0)), _full(g), _full(w_pq), _full(k1), _full(k2)],
        out_specs=[pl.BlockSpec((D_MODEL, tm), lambda i: (0, i)), hk, hk, hk, hk],
        out_shape=[jax.ShapeDtypeStruct((D_MODEL, m), BF16), hks, hks, hks, hks],
        compiler_params=_params(("parallel",)), name="peer_route")(x, g, w_pq, k1, k2)


def _peer_kernel(xt_ref, u_ref, vt_ref, thr_ref, s2_ref, e2_ref, c1_ref, o_ref, wa_s, wb_s, act_s):
    e = pl.program_id(1)
    et, tt = act_s.shape
    n_sub = et // PEER_KEYS
    group = 2
    oc = o_ref.shape[0] // n_sub

    @pl.when(e == 0)
    def _():
        o_ref[...] = jnp.zeros_like(o_ref)
        wb_s[...] = jnp.zeros_like(wb_s)

    def step(cur_s, prev_s):
        def score(c):
            rs = slice(c * SCORE_ROWS, (c + 1) * SCORE_ROWS)
            act_s[rs, :] = _dot(u_ref[rs, :], xt_ref[...])

        def emit(c):
            rs = slice(c * EMIT_ROWS, (c + 1) * EMIT_ROWS)
            o_ref[rs, :] += _dot(vt_ref[rs, :], prev_s[...])

        def gate_group(j0, lt):
            ls = slice(lt * 128, (lt + 1) * 128)
            for part in range(PEER_KEYS // GATE_ROWS):
                ks = slice(part * GATE_ROWS, (part + 1) * GATE_ROWS)
                gates = [None] * group
                for h in range(PEER_HEADS):
                    s2 = s2_ref[h, ks, ls]
                    e2 = e2_ref[h, ks, ls]
                    for k in range(group):
                        jj = j0 + k
                        g = jnp.where(s2 >= thr_ref[h, jj:jj + 1, ls], e2, 0.0) * c1_ref[h, jj:jj + 1, ls]
                        gates[k] = g if gates[k] is None else gates[k] + g
                for k in range(group):
                    r0 = (j0 + k) * PEER_KEYS + part * GATE_ROWS
                    rs = slice(r0, r0 + GATE_ROWS)
                    cur_s[rs, ls] = (gates[k] * jax.nn.gelu(act_s[rs, ls])).astype(BF16)

        scores = list(range(et // SCORE_ROWS))
        emits = list(range(o_ref.shape[0] // EMIT_ROWS))
        for j0 in range(0, n_sub, group):
            while scores and scores[0] * SCORE_ROWS < (j0 + group) * PEER_KEYS:
                score(scores.pop(0))
            for lt in range(tt // 128):
                gate_group(j0, lt)
                if scores:
                    score(scores.pop(0))
                elif emits:
                    emit(emits.pop(0))
        for c in emits:
            emit(c)

    @pl.when(e % 2 == 0)
    def _():
        step(wa_s, wb_s)

    @pl.when(e % 2 == 1)
    def _():
        step(wb_s, wa_s)


def _peer(xt, u, vt, thr, s2, e2, c1, tt, et):
    m = xt.shape[1]
    n_e = u.shape[0] // et
    tok3 = lambda r: pl.BlockSpec((PEER_HEADS, r, tt), lambda i, e: (0, 0, i))
    tile3 = pl.BlockSpec((PEER_HEADS, et // PEER_KEYS, tt), lambda i, e: (0, jnp.minimum(e, n_e - 1), i))
    return pl.pallas_call(
        _peer_kernel, grid=(m // tt, n_e + 1),
        in_specs=[pl.BlockSpec((D_MODEL, tt), lambda i, e: (0, i)),
                  pl.BlockSpec((et, D_MODEL), lambda i, e: (jnp.minimum(e, n_e - 1), 0)),
                  pl.BlockSpec((D_MODEL, et), lambda i, e: (0, jnp.maximum(e - 1, 0))),
                  tile3, tok3(PEER_KEYS), tok3(PEER_KEYS), tile3],
        out_specs=pl.BlockSpec((D_MODEL, tt), lambda i, e: (0, i)),
        out_shape=jax.ShapeDtypeStruct((D_MODEL, m), F32),
        scratch_shapes=[pltpu.VMEM((et, tt), BF16)] * 2 + [pltpu.VMEM((et, tt), F32)],
        compiler_params=_params(("parallel", "arbitrary")), name="peer_dense")(
            xt, u, vt, thr, s2, e2, c1)


def _final_kernel(h_ref, ot_ref, g_ref, y_ref):
    y_ref[...] = _rms(h_ref[...] + ot_ref[...].T, g_ref[...])


def _final_norm(h, out_t, g, tm):
    m = h.shape[0]
    row = pl.BlockSpec((tm, D_MODEL), lambda i: (i, 0))
    return pl.pallas_call(
        _final_kernel, grid=(m // tm,),
        in_specs=[row, pl.BlockSpec((D_MODEL, tm), lambda i: (0, i)), _full(g)], out_specs=row,
        out_shape=jax.ShapeDtypeStruct((m, D_MODEL), F32),
        compiler_params=_params(("parallel",)), name="final_norm")(h, out_t, g)


def _row(v):
    return v.reshape(1, -1).astype(F32)


def _cmp_rows(w):
    return jnp.repeat(w.T.astype(F32), HEAD_DIM, axis=1)


def kernel(x_prompt, x_sample, mem_prompt, cache_k_cmp, cache_v_cmp, cache_k_sel, cache_v_sel, cache_k_win, cache_v_win, state_s5_re, state_s5_im, cache_mem_k, cache_mem_v, page_table, g_mix, w_in, w_cmp_k, w_cmp_v, lam_re, lam_im, log_dt, b_re, b_im, c_re, c_im, d_skip, w_glu, g_attn_out, g_ssm_out, w_out, g_x, g_mem, w_xq, w_xk, w_xv, w_xo, g_ffn, w_pq, peer_k1, peer_k2, peer_u, peer_v, g_final):
    nb, seq, _ = x_prompt.shape
    db = x_sample.shape[0]
    depth = g_mix.shape[0]
    assert depth == 1 and x_sample.shape[1] == 1
    l = 0
    mp = nb * seq
    n_gate = 3 * N_HEADS
    kv_end = D_ATTN + 6 * D_KV

    w_attn = jnp.concatenate([w_in[l][:, :kv_end + n_gate], jnp.zeros((D_MODEL, GATE_PAD - n_gate), F32)], axis=1).astype(BF16)
    w_u = w_in[l][:, kv_end + n_gate:].astype(BF16)
    w_full = jnp.concatenate([w_attn, w_u], axis=1)
    w_glu_b = w_glu[l].astype(BF16)
    w_out_a = w_out[l][:D_ATTN].astype(BF16)
    w_out_s = w_out[l][D_ATTN:].astype(BF16)
    w_mem = jnp.concatenate([w_xk[l], w_xv[l]], axis=1).astype(BF16)
    w_xq_b = w_xq[l].astype(BF16)
    w_xo_b = w_xo[l].astype(BF16)
    w_pq_b = w_pq[l].astype(BF16)
    k1_b = peer_k1[l].astype(BF16)
    k2_b = peer_k2[l].astype(BF16)
    u_b = peer_u[l].astype(BF16)
    vt_b = peer_v[l].T.astype(BF16)
    wk_rows = _cmp_rows(w_cmp_k[l])
    wv_rows = _cmp_rows(w_cmp_v[l])
    gm, gx, gf, gfin = _row(g_mix[l]), _row(g_x[l]), _row(g_ffn[l]), _row(g_final)
    ga, gs, gme = _row(g_attn_out[l]), _row(g_ssm_out[l]), _row(g_mem[l])
    s5w = _s5_weights(lam_re[l], lam_im[l], log_dt[l], b_re[l], b_im[l], c_re[l], c_im[l])
    dsk = _row(d_skip[l])

    xp = x_prompt.reshape(mp, D_MODEL)
    q_hm, kv, k_hm, gates, v_t = _proj_attn(xp, gm, w_attn, nb, tm=256)
    u_tm = _proj_u(xp, gm, w_u, nb, tm=256).reshape(seq * nb, D_SSM)
    ck, cv_t = _compress(kv[0], kv[1], wk_rows, wv_rows, nb)
    o_a = _nsa_prompt_t(q_hm, ck, cv_t, k_hm[0], v_t[0], k_hm[1], v_t[1], gates, nb)
    y_tm, p_sr, p_si = _s5_prompt(u_tm, s5w, dsk, nb, tc=256)
    o_s = _glu(y_tm.reshape(seq, nb * D_SSM), w_glu_b, nb, tm=256)
    h1 = _merge(o_a, o_s, ga, gs, w_out_a, w_out_s, xp, tm=256)
    mk, mv = _mem_kv(mem_prompt.reshape(nb * MEM_LEN, D_MODEL), gme, w_mem, tm=256)
    xq = _mm(h1, w_xq_b, 256, BF16, g=gx, name="xattn_q")
    xo = _xattn_prompt(xq, mk, mv, nb, tm=256)
    h2 = _mm(xo, w_xo_b, 256, F32, res=h1, name="xattn_o")
    routed = _peer_route(h2, gf, w_pq_b, k1_b, k2_b, tm=256)
    y_p = _final_norm(h2, _peer(routed[0], u_b, vt_b, *routed[1:], tt=512, et=1024), gfin, tm=256)

    xs = x_sample.reshape(db, D_MODEL)
    z = _mm(xs, w_full, db, F32, g=gm, gate_cols=(kv_end, kv_end + n_gate), name="proj_sample")
    new_rows = [z[:, D_ATTN + D_KV * i:D_ATTN + D_KV * (i + 1)] for i in range(6)]
    n_phys = cache_k_cmp.shape[1]
    pools = [c[l].transpose(0, 2, 3, 1).reshape(n_phys, D_KV, PAGE)
             for c in (cache_k_cmp, cache_v_cmp, cache_k_sel, cache_v_sel)]
    wl = cache_k_win.shape[2]
    buf_kw = cache_k_win[l].transpose(0, 2, 3, 1).reshape(db, D_KV, wl)
    buf_vw = cache_v_win[l].transpose(0, 2, 3, 1).reshape(db, D_KV, wl)
    q_s = (z[:, :D_ATTN] * (HEAD_DIM ** -0.5)).reshape(db, N_HEADS, HEAD_DIM)
    gates_t = z[:, kv_end:kv_end + n_gate].reshape(db, 3, N_HEADS).transpose(0, 2, 1)
    slopes = jnp.asarray(np.array([[_alibi_slope(i)] for i in range(N_HEADS)], np.float32))
    o_a_s = _nsa_sample(page_table, pools, z[:, D_ATTN:kv_end].reshape(db, 6, D_KV), buf_kw, buf_vw,
                        q_s, gates_t, slopes, wk_rows, wv_rows).reshape(db, D_ATTN)
    y_s, s_sr, s_si = _s5_sample(z[:, kv_end + GATE_PAD:], state_s5_re[l].reshape(db, -1), state_s5_im[l].reshape(db, -1), s5w, dsk)
    o_s_s = _glu(y_s, w_glu_b, 1, tm=db)
    h1s = _merge(o_a_s, o_s_s, ga, gs, w_out_a, w_out_s, xs, tm=db)
    xq_s = _mm(h1s, w_xq_b, db, F32, g=gx, name="xattn_q_sample")
    nt = X_HEAD_DIM // 128

    def mem_rows(c):
        return c.reshape(db, MEM_LEN, X_HEADS, nt, 128).transpose(0, 1, 3, 2, 4).reshape(db, MEM_LEN * nt * X_HEADS, 128)

    xo_s = _xattn_sample(xq_s.reshape(db, 1, D_MODEL), mem_rows(cache_mem_k[l]), mem_rows(cache_mem_v[l])).reshape(db, D_MODEL)
    h2s = _mm(xo_s, w_xo_b, db, F32, res=h1s, name="xattn_o_sample")
    routed_s = _peer_route(h2s, gf, w_pq_b, k1_b, k2_b, tm=db)
    y_s_out = _final_norm(h2s, _peer(routed_s[0], u_b, vt_b, *routed_s[1:], tt=db, et=1024), gfin, tm=db)

    kvshape = (1, nb, seq, N_KV, HEAD_DIM)
    wlp = min(WINDOW, seq)
    p_kv = ([a.reshape(kvshape) for a in kv[:2]]
            + [a.reshape(1, nb, N_KV, HEAD_DIM, seq).transpose(0, 1, 4, 2, 3) for a in kv[2:]])
    p_win = [a[:, :, seq - wlp:] for a in p_kv[4:6]]
    s_new = [r.reshape(1, db, 1, N_KV, HEAD_DIM) for r in new_rows]
    s_win = [jnp.concatenate([c[l], n[0]], axis=1)[None, :, -min(WINDOW, wl + 1):] for c, n in ((cache_k_win, s_new[4]), (cache_v_win, s_new[5]))]
    g64 = (1, -1, N_SSM_GROUPS, SSM_STATE)
    return (y_p.reshape(nb, seq, D_MODEL), y_s_out.reshape(db, 1, D_MODEL),
            p_kv[0], p_kv[1], p_kv[2], p_kv[3], p_win[0], p_win[1],
            p_sr.reshape(g64), p_si.reshape(g64),
            mk.reshape(1, nb, MEM_LEN, X_HEADS, X_HEAD_DIM), mv.reshape(1, nb, MEM_LEN, X_HEADS, X_HEAD_DIM),
            s_new[0], s_new[1], s_new[2], s_new[3], s_win[0], s_win[1],
            s_sr.reshape(g64), s_si.reshape(g64))
```

```python
import functools
import math

import jax
import jax.numpy as jnp
import numpy as np
from jax import lax
from jax.experimental import pallas as pl
from jax.experimental.pallas import tpu as pltpu

F32 = jnp.float32
BF16 = jnp.bfloat16

D_MODEL = 2048
N_HEADS = 16
HEAD_DIM = 64
N_KV = 4
GQ = 4
D_ATTN = 1024
D_SSM = 1024
D_KV = 256
CMP_BLOCK = 32
CMP_STRIDE = 16
SEL_BLOCK = 64
N_SEL = 8
WINDOW = 512
Q_BLOCK = 128
VT_ROWS = 80
HEADS_PER_LOOP = 4
FORCE = 1e4
N_SSM_GROUPS = 64
SSM_GROUP = 16
SSM_STATE = 64
SSM_BANDS = 4
MEM_LEN = 256
X_HEADS = 4
X_HEAD_DIM = 512
PEER_KEYS = 128
PEER_HEADS = 8
PEER_TOPK = 16
GATE_ROWS = 32
SCORE_ROWS = 128
EMIT_ROWS = 256
PAGE = 128
EPS = 1e-6
NEG = -1e30
TINY = 1e-30
LOWEST = -3.0e38
HUGE = 3.0e38
GATE_PAD = 128
VMEM_LIMIT = 56 * 2**20


def _params(sem, flags=None):
    return pltpu.CompilerParams(dimension_semantics=sem, vmem_limit_bytes=VMEM_LIMIT, flags=flags)


def _full(a):
    nd = a.ndim
    return pl.BlockSpec(a.shape, lambda *_: (0,) * nd)


def _rms(x, g):
    return x * lax.rsqrt(jnp.mean(x * x, axis=-1, keepdims=True) + EPS) * g


def _dot(a, b):
    return jnp.dot(a, b, preferred_element_type=F32)


def _dot_nt(a, b):
    return lax.dot_general(a, b, (((1,), (1,)), ((), ())), preferred_element_type=F32)


def _dot3(a, b_exact):
    hi = a.astype(BF16)
    r1 = a - hi.astype(F32)
    mid = r1.astype(BF16)
    lo = (r1 - mid.astype(F32)).astype(BF16)
    return _dot(hi, b_exact) + _dot(mid, b_exact) + _dot(lo, b_exact)


def _iota(shape, dim):
    return lax.broadcasted_iota(jnp.int32, shape, dim)


def _proj_attn_kernel(x_ref, g_ref, w_ref, q_ref, *rest):
    kv_refs, kh_refs, gate_ref, vt_refs, cmp_ft_refs = rest[:6], rest[6:8], rest[8], rest[9:11], rest[11:13]
    xn = _rms(x_ref[...], g_ref[...]).astype(BF16)
    z = _dot(xn, w_ref[...])
    for hd in range(N_HEADS):
        q_ref[hd] = (z[:, hd * HEAD_DIM:(hd + 1) * HEAD_DIM] * (HEAD_DIM ** -0.5)).astype(BF16)
    zts = {}
    for k in range(6):
        zk = z[:, D_ATTN + D_KV * k:D_ATTN + D_KV * (k + 1)]
        if k < 2:
            kv_refs[k][...] = zk
            cmp_ft_refs[k][0] = zk.T
        else:
            zts[k] = zk.T
            kv_refs[k][0] = zts[k]
    for kh_ref, k in zip(kh_refs, (2, 4)):
        for h in range(N_KV):
            c0 = D_ATTN + D_KV * k + h * HEAD_DIM
            kh_ref[h] = z[:, c0:c0 + HEAD_DIM].astype(BF16)
    for vt_ref, k in zip(vt_refs, (3, 5)):
        zt = zts[k].astype(BF16)
        for h in range(N_KV):
            vt_ref[h, 0, 0:HEAD_DIM, :] = zt[h * HEAD_DIM:(h + 1) * HEAD_DIM]
            vt_ref[h, 0, HEAD_DIM:, :] = jnp.ones((VT_ROWS - HEAD_DIM, zt.shape[1]), BF16)
    gate_ref[...] = jax.nn.sigmoid(z[:, D_ATTN + 6 * D_KV:])


def _proj_attn(x, g, w, nb, tm):
    m = x.shape[0]
    nt = m // nb // tm
    row = lambda n: pl.BlockSpec((tm, n), lambda i: (i, 0))
    hm = lambda n: pl.BlockSpec((n, tm, HEAD_DIM), lambda i: (0, i, 0))
    ft = pl.BlockSpec((1, D_KV, tm), lambda i: (i // nt, 0, i % nt))
    out_shape = ([jax.ShapeDtypeStruct((N_HEADS, m, HEAD_DIM), BF16)]
                 + [jax.ShapeDtypeStruct((m, D_KV), F32)] * 2
                 + [jax.ShapeDtypeStruct((nb, D_KV, m // nb), F32)] * 4
                 + [jax.ShapeDtypeStruct((N_KV, m, HEAD_DIM), BF16)] * 2
                 + [jax.ShapeDtypeStruct((m, GATE_PAD), F32)]
                 + [jax.ShapeDtypeStruct((N_KV, m // tm, VT_ROWS, tm), BF16)] * 2
                 + [jax.ShapeDtypeStruct((nb, D_KV, m // nb), F32)] * 2)
    vt = pl.BlockSpec((N_KV, 1, VT_ROWS, tm), lambda i: (0, i, 0, 0))
    out_specs = [hm(N_HEADS)] + [row(D_KV)] * 2 + [ft] * 4 + [hm(N_KV)] * 2 + [row(GATE_PAD)] + [vt] * 2 + [ft] * 2
    outs = pl.pallas_call(
        _proj_attn_kernel, grid=(m // tm,),
        in_specs=[row(D_MODEL), _full(g), _full(w)],
        out_specs=out_specs, out_shape=out_shape,
        compiler_params=_params(("parallel",)), name="proj_attn")(x, g, w)
    return outs[0], outs[1:7], outs[7:9], outs[9], outs[10:12], outs[12:14]


def _proj_u_kernel(x_ref, g_ref, w_ref, u_ref):
    xn = _rms(x_ref[...], g_ref[...]).astype(BF16)
    u_ref[...] = _dot(xn, w_ref[...])


def _proj_u(x, g, w, nb, tm):
    m = x.shape[0]
    t = m // nb
    nt = t // tm
    return pl.pallas_call(
        _proj_u_kernel, grid=(nb, nt),
        in_specs=[pl.BlockSpec((tm, D_MODEL), lambda b, i: (b * nt + i, 0)), _full(g), _full(w)],
        out_specs=pl.BlockSpec((tm, D_SSM), lambda b, i: (i, b)),
        out_shape=jax.ShapeDtypeStruct((t, nb * D_SSM), F32),
        compiler_params=_params(("parallel", "parallel")), name="proj_u")(x, g, w)


def _mm_kernel(*refs, norm, res, gate_cols):
    it = iter(refs)
    x_ref = next(it)
    g_ref = next(it) if norm else None
    w_ref = next(it)
    r_ref = next(it) if res else None
    o_ref = next(it)
    x = x_ref[...]
    if norm:
        x = _rms(x.astype(F32), g_ref[...])
    z = _dot(x.astype(BF16), w_ref[...])
    if res:
        z = z + r_ref[...]
    if gate_cols is not None:
        col = _iota(z.shape, 1)
        z = jnp.where((col >= gate_cols[0]) & (col < gate_cols[1]), jax.nn.sigmoid(z), z)
    o_ref[...] = z.astype(o_ref.dtype)


def _mm(x, w, tm, out_dtype, g=None, res=None, gate_cols=None, name="mm"):
    m, k = x.shape
    n = w.shape[1]
    row = lambda c: pl.BlockSpec((tm, c), lambda i: (i, 0))
    args, specs = [x], [row(k)]
    if g is not None:
        args.append(g)
        specs.append(_full(g))
    args.append(w)
    specs.append(_full(w))
    if res is not None:
        args.append(res)
        specs.append(row(n))
    return pl.pallas_call(
        functools.partial(_mm_kernel, norm=g is not None, res=res is not None, gate_cols=gate_cols),
        grid=(m // tm,), in_specs=specs, out_specs=row(n),
        out_shape=jax.ShapeDtypeStruct((m, n), out_dtype),
        compiler_params=_params(("parallel",)), name=name)(*args)


def _mm2_kernel(x_ref, g_ref, w_ref, o0_ref, o1_ref):
    xn = _rms(x_ref[...], g_ref[...]).astype(BF16)
    z = _dot(xn, w_ref[...])
    n = o0_ref.shape[1]
    o0_ref[...] = z[:, :n]
    o1_ref[...] = z[:, n:]


def _mem_kv(mem, g, w, tm):
    m = mem.shape[0]
    row = pl.BlockSpec((tm, D_MODEL), lambda i: (i, 0))
    return pl.pallas_call(
        _mm2_kernel, grid=(m // tm,), in_specs=[row, _full(g), _full(w)], out_specs=[row, row],
        out_shape=[jax.ShapeDtypeStruct((m, D_MODEL), F32)] * 2,
        compiler_params=_params(("parallel",)), name="mem_kv")(mem, g, w)


def _glu_kernel(y_ref, w_ref, o_ref):
    y = y_ref[...]
    o_ref[...] = y * jax.nn.sigmoid(_dot(y.astype(BF16), w_ref[...]))


def _glu(y_tm, w, nb, tm):
    t = y_tm.shape[0]
    nt = t // tm
    return pl.pallas_call(
        _glu_kernel, grid=(nb, nt),
        in_specs=[pl.BlockSpec((tm, D_SSM), lambda b, i: (i, b)), _full(w)],
        out_specs=pl.BlockSpec((tm, D_SSM), lambda b, i: (b * nt + i, 0)),
        out_shape=jax.ShapeDtypeStruct((nb * t, D_SSM), F32),
        compiler_params=_params(("parallel", "parallel")), name="glu")(y_tm, w)


def _merge_kernel(oa_ref, os_ref, ga_ref, gs_ref, wa_ref, ws_ref, x_ref, o_ref):
    a = _rms(oa_ref[...], ga_ref[...]).astype(BF16)
    s = _rms(os_ref[...], gs_ref[...]).astype(BF16)
    o_ref[...] = x_ref[...] + (_dot(a, wa_ref[...]) + _dot(s, ws_ref[...]))


def _merge(o_a, o_s, g_a, g_s, w_a, w_s, x, tm):
    m = x.shape[0]
    half = pl.BlockSpec((tm, D_ATTN), lambda i: (i, 0))
    row = pl.BlockSpec((tm, D_MODEL), lambda i: (i, 0))
    return pl.pallas_call(
        _merge_kernel, grid=(m // tm,),
        in_specs=[half, half, _full(g_a), _full(g_s), _full(w_a), _full(w_s), row],
        out_specs=row, out_shape=jax.ShapeDtypeStruct((m, D_MODEL), F32),
        compiler_params=_params(("parallel",)), name="merge_heads")(o_a, o_s, g_a, g_s, w_a, w_s, x)


def _pool16(x_refs, w_ref, tokens):
    halves = []
    for hf, x_ref in enumerate(x_refs):
        lanes = slice(hf * 128, (hf + 1) * 128)
        a = b = None
        for j in range(CMP_STRIDE):
            xj = x_ref[pl.ds(j, tokens // CMP_STRIDE, stride=CMP_STRIDE), :]
            ta = xj * w_ref[j:j + 1, lanes]
            tb = xj * w_ref[CMP_STRIDE + j:CMP_STRIDE + j + 1, lanes]
            a = ta if a is None else a + ta
            b = tb if b is None else b + tb
        halves.append((a, b))
    return (jnp.concatenate([halves[0][0], halves[1][0]], axis=1),
            jnp.concatenate([halves[0][1], halves[1][1]], axis=1))


def _shift_up(b, last_row):
    n = b.shape[0]
    rolled = pltpu.roll(b, n - 1, 0)
    return jnp.where(_iota(b.shape, 0) == n - 1, last_row, rolled)


def _compress_kernel(k0_ref, k1_ref, v0_ref, v1_ref, wk_ref, wv_ref, ck_ref, cv_ref):
    tokens = k0_ref.shape[0]
    a, b = _pool16((k0_ref, k1_ref), wk_ref, tokens)
    c = (a + _shift_up(b, 0.0)).astype(BF16)
    for h in range(N_KV):
        ck_ref[0, h] = c[:, h * HEAD_DIM:(h + 1) * HEAD_DIM]
    a, b = _pool16((v0_ref, v1_ref), wv_ref, tokens)
    ct = (a + _shift_up(b, 0.0)).T.astype(BF16)
    for h in range(N_KV):
        cv_ref[0, h] = ct[h * HEAD_DIM:(h + 1) * HEAD_DIM]


def _compress(kc, vc, wk, wv, nb):
    s = kc.shape[0] // nb
    nc = s // CMP_STRIDE
    lo = pl.BlockSpec((s, D_KV // 2), lambda b: (b, 0))
    hi = pl.BlockSpec((s, D_KV // 2), lambda b: (b, 1))
    out = pl.BlockSpec((1, N_KV, nc, HEAD_DIM), lambda b: (b, 0, 0, 0))
    out_t = pl.BlockSpec((1, N_KV, HEAD_DIM, nc), lambda b: (b, 0, 0, 0))
    return pl.pallas_call(
        _compress_kernel, grid=(nb,), in_specs=[lo, hi, lo, hi, _full(wk), _full(wv)], out_specs=[out, out_t],
        out_shape=[jax.ShapeDtypeStruct((nb, N_KV, nc, HEAD_DIM), BF16),
                   jax.ShapeDtypeStruct((nb, N_KV, HEAD_DIM, nc), BF16)],
        compiler_params=_params(("parallel",)), name="nsa_compress")(kc, kc, vc, vc, wk, wv)


def _overlap_matrix(nc, width, n_cmp, n_sel):
    i = _iota((nc, width), 0)
    j = _iota((nc, width), 1)
    lo = jnp.maximum(i * CMP_STRIDE, j * SEL_BLOCK)
    hi = jnp.minimum(i * CMP_STRIDE + CMP_BLOCK, (j + 1) * SEL_BLOCK)
    ov = jnp.maximum(hi - lo, 0).astype(F32) * (1.0 / CMP_BLOCK)
    return jnp.where((i < n_cmp) & (j < n_sel), ov, 0.0).astype(BF16)


def _select_blocks(imp, qpos, n_sel, axis=1):
    blk = _iota(imp.shape, axis)
    valid = blk * SEL_BLOCK <= qpos
    forced = (blk == qpos // SEL_BLOCK) | (blk == 0)
    x = jnp.where(valid, imp + jnp.where(forced, FORCE, 0.0), NEG)
    x = jnp.where(blk < n_sel, x, LOWEST)
    sel = jnp.zeros(imp.shape, F32)
    blk_f = blk.astype(F32)
    for _ in range(N_SEL):
        m = jnp.max(x, axis=axis, keepdims=True)
        first = jnp.min(jnp.where(x == m, blk_f, 4.0 * imp.shape[axis]), axis=axis, keepdims=True)
        pick = blk_f == first
        sel = jnp.where(pick & (m > 0.5 * NEG), 1.0, sel)
        x = jnp.where(pick, LOWEST, x)
    return sel


M_INIT = 0.1 * NEG


def _alibi_slope(head):
    return float(2.0 ** (-8.0 * (head + 1) / N_HEADS))


def _nsa_prompt_t_kernel(q_ref, ck_ref, cvt_ref, ks_ref, vst_ref, kw_ref, vwt_ref, gate_ref, o_ref, *, kt, nc):
    qi = pl.program_id(1)
    cols = GQ * Q_BLOCK
    q0 = qi * Q_BLOCK
    qpos = q0 + _iota((1, Q_BLOCK), 1)
    n_sel = ks_ref.shape[1] // SEL_BLOCK
    r_sel = -(-n_sel // 8) * 8
    bi = _iota((r_sel, nc), 0)
    ci = _iota((r_sel, nc), 1)
    ov = jnp.maximum(jnp.minimum(ci * CMP_STRIDE + CMP_BLOCK, (bi + 1) * SEL_BLOCK)
                     - jnp.maximum(ci * CMP_STRIDE, bi * SEL_BLOCK), 0).astype(F32) * (1.0 / CMP_BLOCK)
    ov = jnp.where((ci < nc - 1) & (bi < n_sel), ov, 0.0).astype(BF16)
    cend = _iota((nc, Q_BLOCK), 0) * CMP_STRIDE + (CMP_BLOCK - 1)
    cend_f = cend.astype(F32)
    mask_c = qpos >= cend
    n_hi = (q0 + Q_BLOCK + kt - 1) // kt
    w_lo = jnp.maximum(q0 - (WINDOW - 1), 0) // kt
    gates_t = gate_ref[...].T
    lanes = lambda g: slice(g * Q_BLOCK, (g + 1) * Q_BLOCK)

    def prepare(h):
        q = q_ref[h * GQ:(h + 1) * GQ].reshape(cols, HEAD_DIM)
        s = _dot_nt(ck_ref[0, h], q)
        ps, p_grp = [], None
        for g in range(GQ):
            sg = jnp.where(mask_c, s[:, lanes(g)] + _alibi_slope(h * GQ + g) * cend_f, NEG)
            p = jnp.where(mask_c, jnp.exp(sg - jnp.max(sg, axis=0, keepdims=True)), 0.0)
            p = p / jnp.maximum(jnp.sum(p, axis=0, keepdims=True), TINY)
            ps.append(p.astype(BF16))
            p_grp = p if p_grp is None else p_grp + p
        o_c = _dot(cvt_ref[0, h], jnp.concatenate(ps, axis=1))
        hi = p_grp.astype(BF16)
        r1 = p_grp - hi.astype(F32)
        mid = r1.astype(BF16)
        lo = (r1 - mid.astype(F32)).astype(BF16)
        imp = _dot(ov, hi) + _dot(ov, mid) + _dot(ov, lo)
        sel = _select_blocks(imp, qpos, n_sel, axis=0).astype(BF16)
        return q, sel, o_c

    def tile(t, carry, h, q, sel, k_ref, vt_ref):
        m, acc = carry
        k0 = pl.multiple_of(t * kt, kt)
        kpos = k0 + _iota((kt, Q_BLOCK), 0)
        kpos_f = kpos.astype(F32)
        s = _dot_nt(k_ref[h, pl.ds(k0, kt), :], q)
        ok = kpos <= qpos
        if sel is not None:
            expand = (k0 + _iota((kt, r_sel), 0)) // SEL_BLOCK == _iota((kt, r_sel), 1)
            ok = ok & (_dot(jnp.where(expand, 1.0, 0.0).astype(BF16), sel) > 0.5)
        else:
            ok = ok & (qpos - kpos < WINDOW)
        s = jnp.concatenate([jnp.where(ok, s[:, lanes(g)] + _alibi_slope(h * GQ + g) * kpos_f, NEG)
                             for g in range(GQ)], axis=1)
        m_new = jnp.maximum(m, jnp.max(s, axis=0, keepdims=True))
        p = jnp.exp(s - m_new).astype(BF16)
        acc = jnp.exp(m - m_new) * acc + _dot(vt_ref[h, t], p)
        return m_new, acc

    outs = [None] * N_HEADS
    for h0 in range(0, N_KV, HEADS_PER_LOOP):
        heads = range(h0, h0 + HEADS_PER_LOOP)
        prep = [prepare(h) for h in heads]

        def sel_tiles(t, carry):
            return [tile(t, c, h, q, sel, ks_ref, vst_ref) for c, h, (q, sel, _) in zip(carry, heads, prep)]

        def win_tiles(t, carry):
            return [tile(t, c, h, q, None, kw_ref, vwt_ref) for c, h, (q, _, _) in zip(carry, heads, prep)]

        init = [(jnp.full((1, cols), M_INIT, F32), jnp.zeros((VT_ROWS, cols), F32)) for _ in heads]
        sel_c = lax.fori_loop(0, w_lo, sel_tiles, init)
        sel_c, win_c = lax.fori_loop(w_lo, n_hi, lambda t, c: (sel_tiles(t, c[0]), win_tiles(t, c[1])), (sel_c, init))
        for h, (_, _, o_c), (_, a_s), (_, a_w) in zip(heads, prep, sel_c, win_c):
            o_s = a_s[0:HEAD_DIM] / jnp.maximum(a_s[HEAD_DIM:HEAD_DIM + 1], TINY)
            o_w = a_w[0:HEAD_DIM] / jnp.maximum(a_w[HEAD_DIM:HEAD_DIM + 1], TINY)
            for g in range(GQ):
                hd = h * GQ + g
                outs[hd] = (gates_t[hd:hd + 1] * o_c[:, lanes(g)] + gates_t[N_HEADS + hd:N_HEADS + hd + 1] * o_s[:, lanes(g)]
                            + gates_t[2 * N_HEADS + hd:2 * N_HEADS + hd + 1] * o_w[:, lanes(g)])
    o_ref[...] = jnp.concatenate(outs, axis=0).T


def _nsa_prompt_t(q_hm, ck, cvt, ks, vst, kw, vwt, gates, nb):
    m = gates.shape[0]
    s = m // nb
    nq = s // Q_BLOCK
    nc = ck.shape[2]
    kt = vst.shape[3]
    kv = pl.BlockSpec((N_KV, s, HEAD_DIM), lambda b, i: (0, b, 0))
    vt = pl.BlockSpec((N_KV, s // kt, VT_ROWS, kt), lambda b, i: (0, b, 0, 0))
    return pl.pallas_call(
        functools.partial(_nsa_prompt_t_kernel, kt=kt, nc=nc), grid=(nb, nq),
        in_specs=[pl.BlockSpec((N_HEADS, Q_BLOCK, HEAD_DIM), lambda b, i: (0, b * nq + i, 0)),
                  pl.BlockSpec((1, N_KV, nc, HEAD_DIM), lambda b, i: (b, 0, 0, 0)),
                  pl.BlockSpec((1, N_KV, HEAD_DIM, nc), lambda b, i: (b, 0, 0, 0)),
                  kv, vt, kv, vt,
                  pl.BlockSpec((Q_BLOCK, GATE_PAD), lambda b, i: (b * nq + i, 0))],
        out_specs=pl.BlockSpec((Q_BLOCK, D_ATTN), lambda b, i: (b * nq + i, 0)),
        out_shape=jax.ShapeDtypeStruct((m, D_ATTN), F32),
        compiler_params=_params(("parallel", "parallel")), name="nsa_prompt")(q_hm, ck, cvt, ks, vst, kw, vwt, gates)


def _nsa_sample_kernel(pt_ref, *refs, n_pages, past):
    del pt_ref
    it = iter(refs)
    pools = [[next(it) for _ in range(n_pages)] for _ in range(4)]
    new_ref = next(it)
    new = [new_ref.at[0, pl.ds(i, 1)] for i in range(6)]
    bkw_ref, bvw_ref, q_ref, gate_ref, slope_ref, wk_ref, wv_ref, o_ref = [next(it) for _ in range(8)]
    kbuf, vbuf = next(it), next(it)
    kwbuf, vwbuf = next(it), next(it)
    tok = next(it), next(it)
    nc = past // CMP_STRIDE
    n_sel = past // SEL_BLOCK + 1
    rows = N_HEADS
    slope = slope_ref[...]
    own = _iota((rows, D_KV), 1) // HEAD_DIM == _iota((rows, D_KV), 0) // GQ
    q = jnp.where(own, jnp.concatenate([q_ref[0]] * N_KV, axis=1), 0.0)
    qb = q.astype(BF16)

    def new_row(i):
        return new[i][...].astype(BF16).astype(F32)

    def own_heads(o):
        o = jnp.where(own, o, 0.0)
        return o[:, 0:64] + o[:, 64:128] + o[:, 128:192] + o[:, 192:256]

    def pooled(pages, w_ref, x_new):
        for i, p_ref in enumerate(pages):
            pt = p_ref[0].T
            tok[0][i * PAGE:(i + 1) * PAGE, :] = pt[:, 0:128]
            tok[1][i * PAGE:(i + 1) * PAGE, :] = pt[:, 128:256]
        a, b = _pool16(tok, w_ref, past)
        return (a + _shift_up(b, w_ref[CMP_STRIDE:CMP_STRIDE + 1, :] * x_new[...])).astype(BF16)

    def fill(pages, kb, vb):
        for i, (kp, vp) in enumerate(pages):
            kb[:, i * kp.shape[1]:(i + 1) * kp.shape[1]] = kp[...].astype(BF16)
            vb[:, i * vp.shape[1]:(i + 1) * vp.shape[1]] = vp[...].astype(BF16)

    def attend(kb, vb, k_new, v_new, mask, new_ok, dist):
        s = jnp.where(mask, _dot(qb, kb[...]) - slope * dist.astype(F32), NEG)
        s_new = jnp.where(new_ok, jnp.sum(qb.astype(F32) * k_new, axis=-1, keepdims=True), NEG)
        m = jnp.maximum(jnp.max(s, axis=-1, keepdims=True), s_new)
        p = jnp.where(mask, jnp.exp(s - m), 0.0)
        p_new = jnp.where(new_ok, jnp.exp(s_new - m), 0.0)
        l = jnp.sum(p, axis=-1, keepdims=True) + p_new
        o = _dot_nt(p.astype(BF16), vb[...]) + p_new.astype(BF16).astype(F32) * v_new
        return own_heads(o) / jnp.maximum(l, TINY)

    ck = pooled(pools[0], wk_ref, new[0])
    dist_c = past - (_iota((1, nc), 1) * CMP_STRIDE + (CMP_BLOCK - 1))
    mask_c = dist_c >= 0
    s = jnp.where(mask_c, _dot_nt(qb, ck) - slope * dist_c.astype(F32), NEG)
    p = jnp.where(mask_c, jnp.exp(s - jnp.max(s, axis=-1, keepdims=True)), 0.0)
    p = p / jnp.maximum(jnp.sum(p, axis=-1, keepdims=True), TINY)
    p_grp = jnp.concatenate(
        [jnp.broadcast_to(jnp.sum(p[h * GQ:(h + 1) * GQ], axis=0, keepdims=True), (GQ, nc)) for h in range(N_KV)], axis=0)
    imp = _dot3(p_grp, _overlap_matrix(nc, 128, nc, n_sel))
    sel = _select_blocks(imp, jnp.full((rows, 1), past, jnp.int32), n_sel)
    o_c = own_heads(_dot(p.astype(BF16), pooled(pools[1], wv_ref, new[1])))
    wl = bkw_ref.shape[2]
    dist_w = wl - _iota((1, wl), 1)
    fill([(bkw_ref.at[0], bvw_ref.at[0])], kwbuf, vwbuf)
    o_w = attend(kwbuf, vwbuf, new_row(4), new_row(5), (dist_w >= 0) & (dist_w < WINDOW), jnp.full((rows, 1), True), dist_w)
    fill([(k.at[0], v.at[0]) for k, v in zip(pools[2], pools[3])], kbuf, vbuf)
    dist_s = past - _iota((1, past), 1)
    expand = _iota((128, past), 0) == _iota((128, past), 1) // SEL_BLOCK
    hit = _dot(sel.astype(BF16), jnp.where(expand, 1.0, 0.0).astype(BF16))
    o_s = attend(kbuf, vbuf, new_row(2), new_row(3), (hit > 0.5) & (dist_s >= 0), sel[:, n_sel - 1:n_sel] > 0.5, dist_s)
    gates = gate_ref[0]
    o_ref[0] = gates[:, 0:1] * o_c + gates[:, 1:2] * o_s + gates[:, 2:3] * o_w


def _nsa_sample(page_table, pools, new_rows, buf_kw, buf_vw, q, gates_t, slopes, wk, wv):
    nb, n_pages = page_table.shape
    past = n_pages * PAGE
    page_specs = []
    for pool in pools:
        for p in range(n_pages):
            page_specs.append(pl.BlockSpec((1,) + pool.shape[1:], lambda b, pt, p=p: (pt[b, p], 0, 0)))
    per_b = lambda shape: pl.BlockSpec((1,) + shape, lambda b, pt: (b,) + (0,) * len(shape))
    in_specs = (page_specs + [per_b(new_rows.shape[1:])] + [per_b(buf_kw.shape[1:])] * 2
                + [per_b((N_HEADS, HEAD_DIM)), per_b((N_HEADS, 3)),
                   pl.BlockSpec(slopes.shape, lambda b, pt: (0, 0)),
                   pl.BlockSpec(wk.shape, lambda b, pt: (0, 0)), pl.BlockSpec(wv.shape, lambda b, pt: (0, 0))])
    args = [pool for pool in pools for _ in range(n_pages)] + [new_rows, buf_kw, buf_vw, q, gates_t, slopes, wk, wv]
    return pl.pallas_call(
        functools.partial(_nsa_sample_kernel, n_pages=n_pages, past=past),
        grid_spec=pltpu.PrefetchScalarGridSpec(
            num_scalar_prefetch=1, grid=(nb,), in_specs=in_specs,
            out_specs=per_b((N_HEADS, HEAD_DIM)),
            scratch_shapes=([pltpu.VMEM((D_KV, past), BF16)] * 2 + [pltpu.VMEM((D_KV, buf_kw.shape[2]), BF16)] * 2
                            + [pltpu.VMEM((past, D_KV // 2), F32)] * 2)),
        out_shape=jax.ShapeDtypeStruct((nb, N_HEADS, HEAD_DIM), F32),
        compiler_params=_params(("arbitrary",)), name="nsa_sample")(page_table, *args)


def _s5_disc_kernel(lr_ref, li_ref, ldt_ref, ar_ref, ai_ref, fr_ref, fi_ref):
    lr, li = lr_ref[...], li_ref[...]
    dt = jnp.exp(ldt_ref[...])
    mag = jnp.exp(lr * dt)
    ar = mag * jnp.cos(li * dt)
    ai = mag * jnp.sin(li * dt)
    den = lr * lr + li * li
    ar_ref[...] = ar
    ai_ref[...] = ai
    fr_ref[...] = ((ar - 1.0) * lr + ai * li) / den
    fi_ref[...] = (ai * lr - (ar - 1.0) * li) / den


def _s5_bbar_kernel(fr_ref, fi_ref, br_ref, bi_ref, or_ref, oi_ref):
    fr, fi, br, bi = fr_ref[...], fi_ref[...], br_ref[...], bi_ref[...]
    or_ref[...] = fr * br - fi * bi
    oi_ref[...] = fr * bi + fi * br


def _s5_weights(lam_re, lam_im, log_dt, b_re, b_im, c_re, c_im):
    g, n = lam_re.shape
    sd = jax.ShapeDtypeStruct((g, n), F32)
    ar, ai, fr, fi = pl.pallas_call(_s5_disc_kernel, out_shape=[sd] * 4, name="s5_discretise")(
        lam_re, lam_im, log_dt.reshape(g, 1))
    sb = jax.ShapeDtypeStruct((g * n, SSM_GROUP), F32)
    bbr, bbi = pl.pallas_call(_s5_bbar_kernel, out_shape=[sb] * 2, name="s5_bbar")(
        fr.reshape(g * n, 1), fi.reshape(g * n, 1), b_re.reshape(g * n, SSM_GROUP), b_im.reshape(g * n, SSM_GROUP))
    eye = jnp.eye(g // SSM_BANDS, dtype=F32)
    gl = g // SSM_BANDS

    def band_in(bb):
        x = bb.reshape(SSM_BANDS, gl, n, SSM_GROUP).transpose(0, 1, 3, 2)
        return jnp.einsum("jgpn,gh->jgphn", x, eye).reshape(SSM_BANDS, gl * SSM_GROUP, gl * n).astype(BF16)

    def band_out(c):
        x = c.reshape(SSM_BANDS, gl, SSM_GROUP, n).transpose(0, 1, 3, 2)
        return jnp.einsum("jgnp,gh->jgnhp", x, eye).reshape(SSM_BANDS, gl * n, gl * SSM_GROUP).astype(BF16)

    return (ar.reshape(1, g * n), ai.reshape(1, g * n), band_in(bbr), band_in(bbi), band_out(c_re), band_out(-c_im))


def _s5_prompt_kernel(u_ref, wbr_ref, wbi_ref, ar_ref, ai_ref, wcr_ref, wci_ref, d_ref,
                      y_ref, sr_ref, si_ref, hr_s, hi_s, cr_s, ci_s, *, nb):
    c = pl.program_id(1)
    rows, width = hr_s.shape
    rep = 8 // nb

    @pl.when(c == 0)
    def _():
        cr_s[...] = jnp.zeros_like(cr_s)
        ci_s[...] = jnp.zeros_like(ci_s)

    u = u_ref[...]
    ub = u.astype(BF16)
    hr_s[...] = _dot(ub, wbr_ref[0])
    hi_s[...] = _dot(ub, wbi_ref[0])
    ar = jnp.broadcast_to(ar_ref[...], (8, width))
    ai = jnp.broadcast_to(ai_ref[...], (8, width))
    sub = _iota((8, width), 0) // nb

    def step(i, carry):
        sr, si = carry
        base = pl.multiple_of(i * 8, 8)
        xr = hr_s[pl.ds(base, 8), :]
        xi = hi_s[pl.ds(base, 8), :]
        outr = outi = None
        for k in range(rep):
            yr = ar * sr - ai * si + xr
            yi = ar * si + ai * sr + xi
            outr = yr if k == 0 else jnp.where(sub == k, yr, outr)
            outi = yi if k == 0 else jnp.where(sub == k, yi, outi)
            sr, si = pltpu.roll(yr, nb, 0), pltpu.roll(yi, nb, 0)
        hr_s[pl.ds(base, 8), :] = outr
        hi_s[pl.ds(base, 8), :] = outi
        return sr, si

    sr, si = lax.fori_loop(0, rows // 8, step, (cr_s[...], ci_s[...]))
    cr_s[...] = sr
    ci_s[...] = si
    y = _dot(hr_s[...].astype(BF16), wcr_ref[0]) + _dot(hi_s[...].astype(BF16), wci_ref[0]) + d_ref[...] * u
    y_ref[...] = jax.nn.gelu(y)

    @pl.when(c == pl.num_programs(1) - 1)
    def _():
        sr_ref[...] = sr[0:nb]
        si_ref[...] = si[0:nb]


def _s5_prompt(u_tm, weights, d_skip, nb, tc):
    ar, ai, wbr, wbi, wcr, wci = weights
    rows = u_tm.shape[0]
    cw = D_SSM // SSM_BANDS
    sw = ar.shape[1] // SSM_BANDS
    blk = tc * nb
    tile = pl.BlockSpec((blk, cw), lambda j, c: (c, j))
    band = lambda a: pl.BlockSpec((1,) + a.shape[1:], lambda j, c: (j, 0, 0))
    vec = lambda w: pl.BlockSpec((1, w), lambda j, c: (0, j))
    state = pl.BlockSpec((nb, sw), lambda j, c: (0, j))
    return pl.pallas_call(
        functools.partial(_s5_prompt_kernel, nb=nb), grid=(SSM_BANDS, rows // blk),
        in_specs=[tile, band(wbr), band(wbi), vec(sw), vec(sw), band(wcr), band(wci), vec(cw)],
        out_specs=[tile, state, state],
        out_shape=[jax.ShapeDtypeStruct((rows, D_SSM), F32)] + [jax.ShapeDtypeStruct((nb, ar.shape[1]), F32)] * 2,
        scratch_shapes=[pltpu.VMEM((blk, sw), F32)] * 2 + [pltpu.VMEM((8, sw), F32)] * 2,
        compiler_params=_params(("parallel", "arbitrary")), name="s5_prompt")(
            u_tm, wbr, wbi, ar, ai, wcr, wci, d_skip)


def _s5_sample_kernel(u_ref, h0r_ref, h0i_ref, wbr_ref, wbi_ref, ar_ref, ai_ref, wcr_ref, wci_ref, d_ref,
                      y_ref, sr_ref, si_ref):
    cw = D_SSM // SSM_BANDS
    sw = ar_ref.shape[1] // SSM_BANDS
    for j in range(SSM_BANDS):
        cs = slice(j * cw, (j + 1) * cw)
        ss = slice(j * sw, (j + 1) * sw)
        u = u_ref[:, cs]
        ub = u.astype(BF16)
        ar, ai = ar_ref[:, ss], ai_ref[:, ss]
        h0r, h0i = h0r_ref[:, ss], h0i_ref[:, ss]
        hr = _dot(ub, wbr_ref[j]) + (ar * h0r - ai * h0i)
        hi = _dot(ub, wbi_ref[j]) + (ar * h0i + ai * h0r)
        sr_ref[:, ss] = hr
        si_ref[:, ss] = hi
        y = _dot(hr.astype(BF16), wcr_ref[j]) + _dot(hi.astype(BF16), wci_ref[j]) + d_ref[:, cs] * u
        y_ref[:, cs] = jax.nn.gelu(y)


def _s5_sample(u, h0r, h0i, weights, d_skip):
    ar, ai, wbr, wbi, wcr, wci = weights
    nb = u.shape[0]
    st = jax.ShapeDtypeStruct(h0r.shape, F32)
    return pl.pallas_call(
        _s5_sample_kernel, out_shape=[jax.ShapeDtypeStruct((nb, D_SSM), F32), st, st],
        compiler_params=pltpu.CompilerParams(vmem_limit_bytes=VMEM_LIMIT), name="s5_sample")(
            u, h0r, h0i, wbr, wbi, ar, ai, wcr, wci, d_skip)


def _xattn_prompt_kernel(q_ref, k_ref, v_ref, o_ref):
    scale = X_HEAD_DIM ** -0.5
    for h in range(X_HEADS):
        cs = slice(h * X_HEAD_DIM, (h + 1) * X_HEAD_DIM)
        s = _dot_nt(q_ref[:, cs], k_ref[:, cs].astype(BF16)) * scale
        p = jnp.exp(s - jnp.max(s, axis=-1, keepdims=True))
        p = p / jnp.sum(p, axis=-1, keepdims=True)
        o_ref[:, cs] = _dot(p.astype(BF16), v_ref[:, cs].astype(BF16)).astype(o_ref.dtype)


def _xattn_prompt(q, mk, mv, nb, tm):
    m = q.shape[0]
    nt = m // nb // tm
    mem = pl.BlockSpec((MEM_LEN, D_MODEL), lambda b, i: (b, 0))
    row = pl.BlockSpec((tm, D_MODEL), lambda b, i: (b * nt + i, 0))
    return pl.pallas_call(
        _xattn_prompt_kernel, grid=(nb, nt), in_specs=[row, mem, mem], out_specs=row,
        out_shape=jax.ShapeDtypeStruct((m, D_MODEL), BF16),
        compiler_params=_params(("parallel", "parallel")), name="xattn_prompt")(q, mk, mv)


def _xattn_sample_kernel(q_ref, k_ref, v_ref, o_ref):
    scale = X_HEAD_DIM ** -0.5
    nt = X_HEAD_DIM // 128
    rows = MEM_LEN * nt * X_HEADS

    def head(ref, h):
        parts = [ref[0, pl.ds(t * X_HEADS + h, MEM_LEN, stride=nt * X_HEADS), :] for t in range(nt)]
        return jnp.concatenate(parts, axis=1).astype(BF16)

    assert k_ref.shape[1] == rows
    for h in range(X_HEADS):
        cs = slice(h * X_HEAD_DIM, (h + 1) * X_HEAD_DIM)
        q = jnp.broadcast_to(q_ref[0, :, cs], (8, X_HEAD_DIM)).astype(BF16)
        s = _dot_nt(q, head(k_ref, h)) * scale
        p = jnp.exp(s - jnp.max(s, axis=-1, keepdims=True))
        p = p / jnp.sum(p, axis=-1, keepdims=True)
        o_ref[0, :, cs] = _dot(p.astype(BF16), head(v_ref, h))[0:1].astype(o_ref.dtype)


def _xattn_sample(q, mk, mv):
    nb = q.shape[0]
    mem = pl.BlockSpec((1,) + mk.shape[1:], lambda b: (b, 0, 0))
    row = pl.BlockSpec((1, 1, D_MODEL), lambda b: (b, 0, 0))
    return pl.pallas_call(
        _xattn_sample_kernel, grid=(nb,), in_specs=[row, mem, mem], out_specs=row,
        out_shape=jax.ShapeDtypeStruct((nb, 1, D_MODEL), BF16),
        compiler_params=_params(("parallel",)), name="xattn_sample")(q, mk, mv)


def _top_distinct(x, n):
    rows = []
    for _ in range(n):
        m = jnp.max(x, axis=0, keepdims=True)
        rows.append(m)
        x = jnp.where(x == m, LOWEST, x)
    return jnp.concatenate(rows, axis=0)


def _peer_route_kernel(x_ref, g_ref, w_ref, k1_ref, k2_ref, xt_ref, thr_ref, s2_ref, e2_ref, c1_ref):
    xn = _rms(x_ref[...], g_ref[...])
    xt_ref[...] = xn.T.astype(BF16)
    q = _dot(xn.astype(BF16), w_ref[...]).astype(BF16)
    half = PEER_KEYS
    for h in range(PEER_HEADS):
        s1 = _dot_nt(k1_ref[...], q[:, 2 * half * h:2 * half * h + half])
        s2 = _dot_nt(k2_ref[...], q[:, 2 * half * h + half:2 * half * (h + 1)])
        d1 = _top_distinct(s1, PEER_TOPK)
        d2 = _top_distinct(s2, PEER_TOPK)
        cand = jnp.concatenate([d1[a:a + 1] + d2[0:PEER_TOPK // (a + 1)] for a in range(PEER_TOPK)], axis=0)
        tau = _top_distinct(cand, PEER_TOPK)[PEER_TOPK - 1:PEER_TOPK]
        top = d1[0:1] + d2[0:1]
        z = jnp.sum(jnp.where(cand >= tau, jnp.exp(cand - top), 0.0), axis=0, keepdims=True)
        thr = jnp.full(s1.shape, HUGE, F32)
        for a in range(PEER_TOPK):
            ok = (d1[a:a + 1] + d2) >= tau
            thr_a = jnp.min(jnp.where(ok, d2, HUGE), axis=0, keepdims=True)
            thr = jnp.where(s1 == d1[a:a + 1], thr_a, thr)
        thr_ref[h] = thr
        s2_ref[h] = s2
        e2_ref[h] = jnp.exp(s2 - d2[0:1])
        c1_ref[h] = jnp.exp(s1 - d1[0:1]) / z


def _peer_route(x, g, w_pq, k1, k2, tm):
    m = x.shape[0]
    hk = pl.BlockSpec((PEER_HEADS, PEER_KEYS, tm), lambda i: (0, 0, i))
    hks = jax.ShapeDtypeStruct((PEER_HEADS, PEER_KEYS, m), F32)
    return pl.pallas_call(
        _peer_route_kernel, grid=(m // tm,),
        in_specs=[pl.BlockSpec((tm, D_MODEL), lambda i: (i, 0)), _full(g), _full(w_pq), _full(k1), _full(k2)],
        out_specs=[pl.BlockSpec((D_MODEL, tm), lambda i: (0, i)), hk, hk, hk, hk],
        out_shape=[jax.ShapeDtypeStruct((D_MODEL, m), BF16), hks, hks, hks, hks],
        compiler_params=_params(("parallel",)), name="peer_route")(x, g, w_pq, k1, k2)


def _peer_kernel(xt_ref, u_ref, vt_ref, thr_ref, s2_ref, e2_ref, c1_ref, o_ref, wa_s, wb_s, act_s):
    e = pl.program_id(1)
    et, tt = act_s.shape
    n_sub = et // PEER_KEYS
    group = 2
    oc = o_ref.shape[0] // n_sub

    @pl.when(e == 0)
    def _():
        o_ref[...] = jnp.zeros_like(o_ref)
        wb_s[...] = jnp.zeros_like(wb_s)

    def step(cur_s, prev_s):
        def score(c):
            rs = slice(c * SCORE_ROWS, (c + 1) * SCORE_ROWS)
            act_s[rs, :] = _dot(u_ref[rs, :], xt_ref[...])

        def emit(c):
            rs = slice(c * EMIT_ROWS, (c + 1) * EMIT_ROWS)
            o_ref[rs, :] += _dot(vt_ref[rs, :], prev_s[...])

        def gate_group(j0, lt):
            ls = slice(lt * 128, (lt + 1) * 128)
            for part in range(PEER_KEYS // GATE_ROWS):
                ks = slice(part * GATE_ROWS, (part + 1) * GATE_ROWS)
                gates = [None] * group
                for h in range(PEER_HEADS):
                    s2 = s2_ref[h, ks, ls]
                    e2 = e2_ref[h, ks, ls]
                    for k in range(group):
                        jj = j0 + k
                        g = jnp.where(s2 >= thr_ref[h, jj:jj + 1, ls], e2, 0.0) * c1_ref[h, jj:jj + 1, ls]
                        gates[k] = g if gates[k] is None else gates[k] + g
                for k in range(group):
                    r0 = (j0 + k) * PEER_KEYS + part * GATE_ROWS
                    rs = slice(r0, r0 + GATE_ROWS)
                    cur_s[rs, ls] = (gates[k] * jax.nn.gelu(act_s[rs, ls])).astype(BF16)

        scores = list(range(et // SCORE_ROWS))
        emits = list(range(o_ref.shape[0] // EMIT_ROWS))
        for j0 in range(0, n_sub, group):
            while scores and scores[0] * SCORE_ROWS < (j0 + group) * PEER_KEYS:
                score(scores.pop(0))
            for lt in range(tt // 128):
                gate_group(j0, lt)
                if scores:
                    score(scores.pop(0))
                elif emits:
                    emit(emits.pop(0))
        for c in emits:
            emit(c)

    @pl.when(e % 2 == 0)
    def _():
        step(wa_s, wb_s)

    @pl.when(e % 2 == 1)
    def _():
        step(wb_s, wa_s)


def _peer(xt, u, vt, thr, s2, e2, c1, tt, et):
    m = xt.shape[1]
    n_e = u.shape[0] // et
    tok3 = lambda r: pl.BlockSpec((PEER_HEADS, r, tt), lambda i, e: (0, 0, i))
    tile3 = pl.BlockSpec((PEER_HEADS, et // PEER_KEYS, tt), lambda i, e: (0, jnp.minimum(e, n_e - 1), i))
    return pl.pallas_call(
        _peer_kernel, grid=(m // tt, n_e + 1),
        in_specs=[pl.BlockSpec((D_MODEL, tt), lambda i, e: (0, i)),
                  pl.BlockSpec((et, D_MODEL), lambda i, e: (jnp.minimum(e, n_e - 1), 0)),
                  pl.BlockSpec((D_MODEL, et), lambda i, e: (0, jnp.maximum(e - 1, 0))),
                  tile3, tok3(PEER_KEYS), tok3(PEER_KEYS), tile3],
        out_specs=pl.BlockSpec((D_MODEL, tt), lambda i, e: (0, i)),
        out_shape=jax.ShapeDtypeStruct((D_MODEL, m), F32),
        scratch_shapes=[pltpu.VMEM((et, tt), BF16)] * 2 + [pltpu.VMEM((et, tt), F32)],
        compiler_params=_params(("parallel", "arbitrary")), name="peer_dense")(
            xt, u, vt, thr, s2, e2, c1)


def _final_kernel(h_ref, ot_ref, g_ref, y_ref):
    y_ref[...] = _rms(h_ref[...] + ot_ref[...].T, g_ref[...])


def _final_norm(h, out_t, g, tm):
    m = h.shape[0]
    row = pl.BlockSpec((tm, D_MODEL), lambda i: (i, 0))
    return pl.pallas_call(
        _final_kernel, grid=(m // tm,),
        in_specs=[row, pl.BlockSpec((D_MODEL, tm), lambda i: (0, i)), _full(g)], out_specs=row,
        out_shape=jax.ShapeDtypeStruct((m, D_MODEL), F32),
        compiler_params=_params(("parallel",)), name="final_norm")(h, out_t, g)


def _row(v):
    return v.reshape(1, -1).astype(F32)


def _cmp_rows(w):
    return jnp.repeat(w.T.astype(F32), HEAD_DIM, axis=1)


def kernel(x_prompt, x_sample, mem_prompt, cache_k_cmp, cache_v_cmp, cache_k_sel, cache_v_sel, cache_k_win, cache_v_win, state_s5_re, state_s5_im, cache_mem_k, cache_mem_v, page_table, g_mix, w_in, w_cmp_k, w_cmp_v, lam_re, lam_im, log_dt, b_re, b_im, c_re, c_im, d_skip, w_glu, g_attn_out, g_ssm_out, w_out, g_x, g_mem, w_xq, w_xk, w_xv, w_xo, g_ffn, w_pq, peer_k1, peer_k2, peer_u, peer_v, g_final):
    nb, seq, _ = x_prompt.shape
    db = x_sample.shape[0]
    depth = g_mix.shape[0]
    assert depth == 1 and x_sample.shape[1] == 1
    l = 0
    mp = nb * seq
    n_gate = 3 * N_HEADS
    kv_end = D_ATTN + 6 * D_KV

    w_attn = jnp.concatenate([w_in[l][:, :kv_end + n_gate], jnp.zeros((D_MODEL, GATE_PAD - n_gate), F32)], axis=1).astype(BF16)
    w_u = w_in[l][:, kv_end + n_gate:].astype(BF16)
    w_full = jnp.concatenate([w_attn, w_u], axis=1)
    w_glu_b = w_glu[l].astype(BF16)
    w_out_a = w_out[l][:D_ATTN].astype(BF16)
    w_out_s = w_out[l][D_ATTN:].astype(BF16)
    w_mem = jnp.concatenate([w_xk[l], w_xv[l]], axis=1).astype(BF16)
    w_xq_b = w_xq[l].astype(BF16)
    w_xo_b = w_xo[l].astype(BF16)
    w_pq_b = w_pq[l].astype(BF16)
    k1_b = peer_k1[l].astype(BF16)
    k2_b = peer_k2[l].astype(BF16)
    u_b = peer_u[l].astype(BF16)
    vt_b = peer_v[l].T.astype(BF16)
    wk_rows = _cmp_rows(w_cmp_k[l])
    wv_rows = _cmp_rows(w_cmp_v[l])
    gm, gx, gf, gfin = _row(g_mix[l]), _row(g_x[l]), _row(g_ffn[l]), _row(g_final)
    ga, gs, gme = _row(g_attn_out[l]), _row(g_ssm_out[l]), _row(g_mem[l])
    s5w = _s5_weights(lam_re[l], lam_im[l], log_dt[l], b_re[l], b_im[l], c_re[l], c_im[l])
    dsk = _row(d_skip[l])

    xp = x_prompt.reshape(mp, D_MODEL)
    q_hm, kv, k_hm, gates, v_t, cmp_ft = _proj_attn(xp, gm, w_attn, nb, tm=256)
    u_tm = _proj_u(xp, gm, w_u, nb, tm=256).reshape(seq * nb, D_SSM)
    ck, cv_t = _compress(kv[0], kv[1], wk_rows, wv_rows, nb)
    o_a = _nsa_prompt_t(q_hm, ck, cv_t, k_hm[0], v_t[0], k_hm[1], v_t[1], gates, nb)
    y_tm, p_sr, p_si = _s5_prompt(u_tm, s5w, dsk, nb, tc=256)
    o_s = _glu(y_tm.reshape(seq, nb * D_SSM), w_glu_b, nb, tm=256)
    h1 = _merge(o_a, o_s, ga, gs, w_out_a, w_out_s, xp, tm=256)
    mk, mv = _mem_kv(mem_prompt.reshape(nb * MEM_LEN, D_MODEL), gme, w_mem, tm=256)
    xq = _mm(h1, w_xq_b, 256, BF16, g=gx, name="xattn_q")
    xo = _xattn_prompt(xq, mk, mv, nb, tm=256)
    h2 = _mm(xo, w_xo_b, 256, F32, res=h1, name="xattn_o")
    routed = _peer_route(h2, gf, w_pq_b, k1_b, k2_b, tm=256)
    y_p = _final_norm(h2, _peer(routed[0], u_b, vt_b, *routed[1:], tt=512, et=1024), gfin, tm=256)

    xs = x_sample.reshape(db, D_MODEL)
    z = _mm(xs, w_full, db, F32, g=gm, gate_cols=(kv_end, kv_end + n_gate), name="proj_sample")
    new_rows = [z[:, D_ATTN + D_KV * i:D_ATTN + D_KV * (i + 1)] for i in range(6)]
    n_phys = cache_k_cmp.shape[1]
    pools = [c[l].transpose(0, 2, 3, 1).reshape(n_phys, D_KV, PAGE)
             for c in (cache_k_cmp, cache_v_cmp, cache_k_sel, cache_v_sel)]
    wl = cache_k_win.shape[2]
    buf_kw = cache_k_win[l].transpose(0, 2, 3, 1).reshape(db, D_KV, wl)
    buf_vw = cache_v_win[l].transpose(0, 2, 3, 1).reshape(db, D_KV, wl)
    q_s = (z[:, :D_ATTN] * (HEAD_DIM ** -0.5)).reshape(db, N_HEADS, HEAD_DIM)
    gates_t = z[:, kv_end:kv_end + n_gate].reshape(db, 3, N_HEADS).transpose(0, 2, 1)
    slopes = jnp.asarray(np.array([[_alibi_slope(i)] for i in range(N_HEADS)], np.float32))
    o_a_s = _nsa_sample(page_table, pools, z[:, D_ATTN:kv_end].reshape(db, 6, D_KV), buf_kw, buf_vw,
                        q_s, gates_t, slopes, wk_rows, wv_rows).reshape(db, D_ATTN)
    y_s, s_sr, s_si = _s5_sample(z[:, kv_end + GATE_PAD:], state_s5_re[l].reshape(db, -1), state_s5_im[l].reshape(db, -1), s5w, dsk)
    o_s_s = _glu(y_s, w_glu_b, 1, tm=db)
    h1s = _merge(o_a_s, o_s_s, ga, gs, w_out_a, w_out_s, xs, tm=db)
    xq_s = _mm(h1s, w_xq_b, db, F32, g=gx, name="xattn_q_sample")
    nt = X_HEAD_DIM // 128

    def mem_rows(c):
        return c.reshape(db, MEM_LEN, X_HEADS, nt, 128).transpose(0, 1, 3, 2, 4).reshape(db, MEM_LEN * nt * X_HEADS, 128)

    xo_s = _xattn_sample(xq_s.reshape(db, 1, D_MODEL), mem_rows(cache_mem_k[l]), mem_rows(cache_mem_v[l])).reshape(db, D_MODEL)
    h2s = _mm(xo_s, w_xo_b, db, F32, res=h1s, name="xattn_o_sample")
    routed_s = _peer_route(h2s, gf, w_pq_b, k1_b, k2_b, tm=db)
    y_s_out = _final_norm(h2s, _peer(routed_s[0], u_b, vt_b, *routed_s[1:], tt=db, et=1024), gfin, tm=db)

    wlp = min(WINDOW, seq)
    p_kv = [a.reshape(1, nb, N_KV, HEAD_DIM, seq).transpose(0, 1, 4, 2, 3) for a in list(cmp_ft) + list(kv[2:])]
    p_win = [a[:, :, seq - wlp:] for a in p_kv[4:6]]
    s_new = [r.reshape(1, db, 1, N_KV, HEAD_DIM) for r in new_rows]
    s_win = [jnp.concatenate([c[l], n[0]], axis=1)[None, :, -min(WINDOW, wl + 1):] for c, n in ((cache_k_win, s_new[4]), (cache_v_win, s_new[5]))]
    g64 = (1, -1, N_SSM_GROUPS, SSM_STATE)
    return (y_p.reshape(nb, seq, D_MODEL), y_s_out.reshape(db, 1, D_MODEL),
            p_kv[0], p_kv[1], p_kv[2], p_kv[3], p_win[0], p_win[1],
            p_sr.reshape(g64), p_si.reshape(g64),
            mk.reshape(1, nb, MEM_LEN, X_HEADS, X_HEAD_DIM), mv.reshape(1, nb, MEM_LEN, X_HEADS, X_HEAD_DIM),
            s_new[0], s_new[1], s_new[2], s_new[3], s_win[0], s_win[1],
            s_sr.reshape(g64), s_si.reshape(g64))
```

```python
import functools
import math

import jax
import jax.numpy as jnp
import numpy as np
from jax import lax
from jax.experimental import pallas as pl
from jax.experimental.pallas import tpu as pltpu

F32 = jnp.float32
BF16 = jnp.bfloat16

D_MODEL = 2048
N_HEADS = 16
HEAD_DIM = 64
N_KV = 4
GQ = 4
D_ATTN = 1024
D_SSM = 1024
D_KV = 256
CMP_BLOCK = 32
CMP_STRIDE = 16
SEL_BLOCK = 64
N_SEL = 8
WINDOW = 512
Q_BLOCK = 128
VT_ROWS = 80
HEADS_PER_LOOP = 4
FORCE = 1e4
N_SSM_GROUPS = 64
SSM_GROUP = 16
SSM_STATE = 64
SSM_BANDS = 4
MEM_LEN = 256
X_HEADS = 4
X_HEAD_DIM = 512
PEER_KEYS = 128
PEER_HEADS = 8
PEER_TOPK = 16
GATE_ROWS = 32
SCORE_ROWS = 128
EMIT_ROWS = 256
PAGE = 128
EPS = 1e-6
NEG = -1e30
TINY = 1e-30
LOWEST = -3.0e38
HUGE = 3.0e38
GATE_PAD = 128
VMEM_LIMIT = 56 * 2**20


def _params(sem, flags=None):
    return pltpu.CompilerParams(dimension_semantics=sem, vmem_limit_bytes=VMEM_LIMIT, flags=flags)


def _full(a):
    nd = a.ndim
    return pl.BlockSpec(a.shape, lambda *_: (0,) * nd)


def _rms(x, g):
    return x * lax.rsqrt(jnp.mean(x * x, axis=-1, keepdims=True) + EPS) * g


def _dot(a, b):
    return jnp.dot(a, b, preferred_element_type=F32)


def _dot_nt(a, b):
    return lax.dot_general(a, b, (((1,), (1,)), ((), ())), preferred_element_type=F32)


def _dot3(a, b_exact):
    hi = a.astype(BF16)
    r1 = a - hi.astype(F32)
    mid = r1.astype(BF16)
    lo = (r1 - mid.astype(F32)).astype(BF16)
    return _dot(hi, b_exact) + _dot(mid, b_exact) + _dot(lo, b_exact)


def _iota(shape, dim):
    return lax.broadcasted_iota(jnp.int32, shape, dim)


def _proj_attn_kernel(x_ref, g_ref, w_ref, q_ref, *rest):
    kv_refs, kh_refs, gate_ref, vt_refs, cmp_ft_refs = rest[:6], rest[6:8], rest[8], rest[9:11], rest[11:13]
    xn = _rms(x_ref[...], g_ref[...]).astype(BF16)
    z = _dot(xn, w_ref[...])
    for hd in range(N_HEADS):
        q_ref[hd] = (z[:, hd * HEAD_DIM:(hd + 1) * HEAD_DIM] * (HEAD_DIM ** -0.5)).astype(BF16)
    zts = {}
    for k in range(6):
        zk = z[:, D_ATTN + D_KV * k:D_ATTN + D_KV * (k + 1)]
        if k < 2:
            kv_refs[k][...] = zk
            cmp_ft_refs[k][0] = zk.T
        else:
            zts[k] = zk.T
            kv_refs[k][0] = zts[k]
    for kh_ref, k in zip(kh_refs, (2, 4)):
        for h in range(N_KV):
            c0 = D_ATTN + D_KV * k + h * HEAD_DIM
            kh_ref[h] = z[:, c0:c0 + HEAD_DIM].astype(BF16)
    for vt_ref, k in zip(vt_refs, (3, 5)):
        zt = zts[k].astype(BF16)
        for h in range(N_KV):
            vt_ref[h, 0, 0:HEAD_DIM, :] = zt[h * HEAD_DIM:(h + 1) * HEAD_DIM]
            vt_ref[h, 0, HEAD_DIM:, :] = jnp.ones((VT_ROWS - HEAD_DIM, zt.shape[1]), BF16)
    gate_ref[...] = jax.nn.sigmoid(z[:, D_ATTN + 6 * D_KV:])


def _proj_attn(x, g, w, nb, tm):
    m = x.shape[0]
    nt = m // nb // tm
    row = lambda n: pl.BlockSpec((tm, n), lambda i: (i, 0))
    hm = lambda n: pl.BlockSpec((n, tm, HEAD_DIM), lambda i: (0, i, 0))
    ft = pl.BlockSpec((1, D_KV, tm), lambda i: (i // nt, 0, i % nt))
    out_shape = ([jax.ShapeDtypeStruct((N_HEADS, m, HEAD_DIM), BF16)]
                 + [jax.ShapeDtypeStruct((m, D_KV), F32)] * 2
                 + [jax.ShapeDtypeStruct((nb, D_KV, m // nb), F32)] * 4
                 + [jax.ShapeDtypeStruct((N_KV, m, HEAD_DIM), BF16)] * 2
                 + [jax.ShapeDtypeStruct((m, GATE_PAD), F32)]
                 + [jax.ShapeDtypeStruct((N_KV, m // tm, VT_ROWS, tm), BF16)] * 2
                 + [jax.ShapeDtypeStruct((nb, D_KV, m // nb), F32)] * 2)
    vt = pl.BlockSpec((N_KV, 1, VT_ROWS, tm), lambda i: (0, i, 0, 0))
    out_specs = [hm(N_HEADS)] + [row(D_KV)] * 2 + [ft] * 4 + [hm(N_KV)] * 2 + [row(GATE_PAD)] + [vt] * 2 + [ft] * 2
    outs = pl.pallas_call(
        _proj_attn_kernel, grid=(m // tm,),
        in_specs=[row(D_MODEL), _full(g), _full(w)],
        out_specs=out_specs, out_shape=out_shape,
        compiler_params=_params(("parallel",)), name="proj_attn")(x, g, w)
    return outs[0], outs[1:7], outs[7:9], outs[9], outs[10:12], outs[12:14]


def _proj_u_kernel(x_ref, g_ref, w_ref, u_ref):
    xn = _rms(x_ref[...], g_ref[...]).astype(BF16)
    u_ref[...] = _dot(xn, w_ref[...])


def _proj_u(x, g, w, nb, tm):
    m = x.shape[0]
    t = m // nb
    nt = t // tm
    return pl.pallas_call(
        _proj_u_kernel, grid=(nb, nt),
        in_specs=[pl.BlockSpec((tm, D_MODEL), lambda b, i: (b * nt + i, 0)), _full(g), _full(w)],
        out_specs=pl.BlockSpec((tm, D_SSM), lambda b, i: (i, b)),
        out_shape=jax.ShapeDtypeStruct((t, nb * D_SSM), F32),
        compiler_params=_params(("parallel", "parallel")), name="proj_u")(x, g, w)


def _mm_kernel(*refs, norm, res, gate_cols):
    it = iter(refs)
    x_ref = next(it)
    g_ref = next(it) if norm else None
    w_ref = next(it)
    r_ref = next(it) if res else None
    o_ref = next(it)
    x = x_ref[...]
    if norm:
        x = _rms(x.astype(F32), g_ref[...])
    z = _dot(x.astype(BF16), w_ref[...])
    if res:
        z = z + r_ref[...]
    if gate_cols is not None:
        col = _iota(z.shape, 1)
        z = jnp.where((col >= gate_cols[0]) & (col < gate_cols[1]), jax.nn.sigmoid(z), z)
    o_ref[...] = z.astype(o_ref.dtype)


def _mm(x, w, tm, out_dtype, g=None, res=None, gate_cols=None, name="mm"):
    m, k = x.shape
    n = w.shape[1]
    row = lambda c: pl.BlockSpec((tm, c), lambda i: (i, 0))
    args, specs = [x], [row(k)]
    if g is not None:
        args.append(g)
        specs.append(_full(g))
    args.append(w)
    specs.append(_full(w))
    if res is not None:
        args.append(res)
        specs.append(row(n))
    return pl.pallas_call(
        functools.partial(_mm_kernel, norm=g is not None, res=res is not None, gate_cols=gate_cols),
        grid=(m // tm,), in_specs=specs, out_specs=row(n),
        out_shape=jax.ShapeDtypeStruct((m, n), out_dtype),
        compiler_params=_params(("parallel",)), name=name)(*args)


def _mm2_kernel(x_ref, g_ref, w_ref, o0_ref, o1_ref):
    xn = _rms(x_ref[...], g_ref[...]).astype(BF16)
    z = _dot(xn, w_ref[...])
    n = o0_ref.shape[1]
    o0_ref[...] = z[:, :n]
    o1_ref[...] = z[:, n:]


def _mem_kv(mem, g, w, tm):
    m = mem.shape[0]
    row = pl.BlockSpec((tm, D_MODEL), lambda i: (i, 0))
    return pl.pallas_call(
        _mm2_kernel, grid=(m // tm,), in_specs=[row, _full(g), _full(w)], out_specs=[row, row],
        out_shape=[jax.ShapeDtypeStruct((m, D_MODEL), F32)] * 2,
        compiler_params=_params(("parallel",)), name="mem_kv")(mem, g, w)


def _glu_kernel(y_ref, w_ref, o_ref):
    y = y_ref[...]
    o_ref[...] = y * jax.nn.sigmoid(_dot(y.astype(BF16), w_ref[...]))


def _glu(y_tm, w, nb, tm):
    t = y_tm.shape[0]
    nt = t // tm
    return pl.pallas_call(
        _glu_kernel, grid=(nb, nt),
        in_specs=[pl.BlockSpec((tm, D_SSM), lambda b, i: (i, b)), _full(w)],
        out_specs=pl.BlockSpec((tm, D_SSM), lambda b, i: (b * nt + i, 0)),
        out_shape=jax.ShapeDtypeStruct((nb * t, D_SSM), F32),
        compiler_params=_params(("parallel", "parallel")), name="glu")(y_tm, w)


def _merge_kernel(oa_ref, os_ref, ga_ref, gs_ref, wa_ref, ws_ref, x_ref, o_ref):
    a = _rms(oa_ref[...], ga_ref[...]).astype(BF16)
    s = _rms(os_ref[...], gs_ref[...]).astype(BF16)
    o_ref[...] = x_ref[...] + (_dot(a, wa_ref[...]) + _dot(s, ws_ref[...]))


def _merge(o_a, o_s, g_a, g_s, w_a, w_s, x, tm):
    m = x.shape[0]
    half = pl.BlockSpec((tm, D_ATTN), lambda i: (i, 0))
    row = pl.BlockSpec((tm, D_MODEL), lambda i: (i, 0))
    return pl.pallas_call(
        _merge_kernel, grid=(m // tm,),
        in_specs=[half, half, _full(g_a), _full(g_s), _full(w_a), _full(w_s), row],
        out_specs=row, out_shape=jax.ShapeDtypeStruct((m, D_MODEL), F32),
        compiler_params=_params(("parallel",)), name="merge_heads")(o_a, o_s, g_a, g_s, w_a, w_s, x)


def _pool16(x_refs, w_ref, tokens):
    halves = []
    for hf, x_ref in enumerate(x_refs):
        lanes = slice(hf * 128, (hf + 1) * 128)
        a = b = None
        for j in range(CMP_STRIDE):
            xj = x_ref[pl.ds(j, tokens // CMP_STRIDE, stride=CMP_STRIDE), :]
            ta = xj * w_ref[j:j + 1, lanes]
            tb = xj * w_ref[CMP_STRIDE + j:CMP_STRIDE + j + 1, lanes]
            a = ta if a is None else a + ta
            b = tb if b is None else b + tb
        halves.append((a, b))
    return (jnp.concatenate([halves[0][0], halves[1][0]], axis=1),
            jnp.concatenate([halves[0][1], halves[1][1]], axis=1))


def _shift_up(b, last_row):
    n = b.shape[0]
    rolled = pltpu.roll(b, n - 1, 0)
    return jnp.where(_iota(b.shape, 0) == n - 1, last_row, rolled)


def _compress_kernel(k0_ref, k1_ref, v0_ref, v1_ref, wk_ref, wv_ref, ck_ref, cv_ref):
    tokens = k0_ref.shape[0]
    a, b = _pool16((k0_ref, k1_ref), wk_ref, tokens)
    c = (a + _shift_up(b, 0.0)).astype(BF16)
    for h in range(N_KV):
        ck_ref[0, h] = c[:, h * HEAD_DIM:(h + 1) * HEAD_DIM]
    a, b = _pool16((v0_ref, v1_ref), wv_ref, tokens)
    ct = (a + _shift_up(b, 0.0)).T.astype(BF16)
    for h in range(N_KV):
        cv_ref[0, h] = ct[h * HEAD_DIM:(h + 1) * HEAD_DIM]


def _compress(kc, vc, wk, wv, nb):
    s = kc.shape[0] // nb
    nc = s // CMP_STRIDE
    lo = pl.BlockSpec((s, D_KV // 2), lambda b: (b, 0))
    hi = pl.BlockSpec((s, D_KV // 2), lambda b: (b, 1))
    out = pl.BlockSpec((1, N_KV, nc, HEAD_DIM), lambda b: (b, 0, 0, 0))
    out_t = pl.BlockSpec((1, N_KV, HEAD_DIM, nc), lambda b: (b, 0, 0, 0))
    return pl.pallas_call(
        _compress_kernel, grid=(nb,), in_specs=[lo, hi, lo, hi, _full(wk), _full(wv)], out_specs=[out, out_t],
        out_shape=[jax.ShapeDtypeStruct((nb, N_KV, nc, HEAD_DIM), BF16),
                   jax.ShapeDtypeStruct((nb, N_KV, HEAD_DIM, nc), BF16)],
        compiler_params=_params(("parallel",)), name="nsa_compress")(kc, kc, vc, vc, wk, wv)


def _overlap_matrix(nc, width, n_cmp, n_sel):
    i = _iota((nc, width), 0)
    j = _iota((nc, width), 1)
    lo = jnp.maximum(i * CMP_STRIDE, j * SEL_BLOCK)
    hi = jnp.minimum(i * CMP_STRIDE + CMP_BLOCK, (j + 1) * SEL_BLOCK)
    ov = jnp.maximum(hi - lo, 0).astype(F32) * (1.0 / CMP_BLOCK)
    return jnp.where((i < n_cmp) & (j < n_sel), ov, 0.0).astype(BF16)


def _select_blocks(imp, qpos, n_sel, axis=1):
    blk = _iota(imp.shape, axis)
    valid = blk * SEL_BLOCK <= qpos
    forced = (blk == qpos // SEL_BLOCK) | (blk == 0)
    x = jnp.where(valid, imp + jnp.where(forced, FORCE, 0.0), NEG)
    x = jnp.where(blk < n_sel, x, LOWEST)
    sel = jnp.zeros(imp.shape, F32)
    blk_f = blk.astype(F32)
    for _ in range(N_SEL):
        m = jnp.max(x, axis=axis, keepdims=True)
        first = jnp.min(jnp.where(x == m, blk_f, 4.0 * imp.shape[axis]), axis=axis, keepdims=True)
        pick = blk_f == first
        sel = jnp.where(pick & (m > 0.5 * NEG), 1.0, sel)
        x = jnp.where(pick, LOWEST, x)
    return sel


M_INIT = 0.1 * NEG


def _alibi_slope(head):
    return float(2.0 ** (-8.0 * (head + 1) / N_HEADS))


def _nsa_prompt_t_kernel(q_ref, ck_ref, cvt_ref, ks_ref, vst_ref, kw_ref, vwt_ref, gate_ref, o_ref, *, kt, nc):
    qi = pl.program_id(1)
    cols = GQ * Q_BLOCK
    q0 = qi * Q_BLOCK
    qpos = q0 + _iota((1, Q_BLOCK), 1)
    n_sel = ks_ref.shape[1] // SEL_BLOCK
    r_sel = -(-n_sel // 8) * 8
    bi = _iota((r_sel, nc), 0)
    ci = _iota((r_sel, nc), 1)
    ov = jnp.maximum(jnp.minimum(ci * CMP_STRIDE + CMP_BLOCK, (bi + 1) * SEL_BLOCK)
                     - jnp.maximum(ci * CMP_STRIDE, bi * SEL_BLOCK), 0).astype(F32) * (1.0 / CMP_BLOCK)
    ov = jnp.where((ci < nc - 1) & (bi < n_sel), ov, 0.0).astype(BF16)
    cend = _iota((nc, Q_BLOCK), 0) * CMP_STRIDE + (CMP_BLOCK - 1)
    cend_f = cend.astype(F32)
    mask_c = qpos >= cend
    n_hi = (q0 + Q_BLOCK + kt - 1) // kt
    w_lo = jnp.maximum(q0 - (WINDOW - 1), 0) // kt
    gates_t = gate_ref[...].T
    lanes = lambda g: slice(g * Q_BLOCK, (g + 1) * Q_BLOCK)

    def prepare(h):
        q = q_ref[h * GQ:(h + 1) * GQ].reshape(cols, HEAD_DIM)
        s = _dot_nt(ck_ref[0, h], q)
        ps, p_grp = [], None
        for g in range(GQ):
            sg = jnp.where(mask_c, s[:, lanes(g)] + _alibi_slope(h * GQ + g) * cend_f, NEG)
            p = jnp.where(mask_c, jnp.exp(sg - jnp.max(sg, axis=0, keepdims=True)), 0.0)
            p = p / jnp.maximum(jnp.sum(p, axis=0, keepdims=True), TINY)
            ps.append(p.astype(BF16))
            p_grp = p if p_grp is None else p_grp + p
        o_c = _dot(cvt_ref[0, h], jnp.concatenate(ps, axis=1))
        hi = p_grp.astype(BF16)
        r1 = p_grp - hi.astype(F32)
        mid = r1.astype(BF16)
        lo = (r1 - mid.astype(F32)).astype(BF16)
        imp = _dot(ov, hi) + _dot(ov, mid) + _dot(ov, lo)
        sel = _select_blocks(imp, qpos, n_sel, axis=0).astype(BF16)
        return q, sel, o_c

    def tile(t, carry, h, q, sel, k_ref, vt_ref):
        m, acc = carry
        k0 = pl.multiple_of(t * kt, kt)
        kpos = k0 + _iota((kt, Q_BLOCK), 0)
        kpos_f = kpos.astype(F32)
        s = _dot_nt(k_ref[h, pl.ds(k0, kt), :], q)
        ok = kpos <= qpos
        if sel is not None:
            expand = (k0 + _iota((kt, r_sel), 0)) // SEL_BLOCK == _iota((kt, r_sel), 1)
            ok = ok & (_dot(jnp.where(expand, 1.0, 0.0).astype(BF16), sel) > 0.5)
        else:
            ok = ok & (qpos - kpos < WINDOW)
        s = jnp.concatenate([jnp.where(ok, s[:, lanes(g)] + _alibi_slope(h * GQ + g) * kpos_f, NEG)
                             for g in range(GQ)], axis=1)
        m_new = jnp.maximum(m, jnp.max(s, axis=0, keepdims=True))
        p = jnp.exp(s - m_new).astype(BF16)
        acc = jnp.exp(m - m_new) * acc + _dot(vt_ref[h, t], p)
        return m_new, acc

    outs = [None] * N_HEADS
    for h0 in range(0, N_KV, HEADS_PER_LOOP):
        heads = range(h0, h0 + HEADS_PER_LOOP)
        prep = [prepare(h) for h in heads]

        def sel_tiles(t, carry):
            return [tile(t, c, h, q, sel, ks_ref, vst_ref) for c, h, (q, sel, _) in zip(carry, heads, prep)]

        def win_tiles(t, carry):
            return [tile(t, c, h, q, None, kw_ref, vwt_ref) for c, h, (q, _, _) in zip(carry, heads, prep)]

        init = [(jnp.full((1, cols), M_INIT, F32), jnp.zeros((VT_ROWS, cols), F32)) for _ in heads]
        sel_c = lax.fori_loop(0, w_lo, sel_tiles, init)
        sel_c, win_c = lax.fori_loop(w_lo, n_hi, lambda t, c: (sel_tiles(t, c[0]), win_tiles(t, c[1])), (sel_c, init))
        for h, (_, _, o_c), (_, a_s), (_, a_w) in zip(heads, prep, sel_c, win_c):
            o_s = a_s[0:HEAD_DIM] / jnp.maximum(a_s[HEAD_DIM:HEAD_DIM + 1], TINY)
            o_w = a_w[0:HEAD_DIM] / jnp.maximum(a_w[HEAD_DIM:HEAD_DIM + 1], TINY)
            for g in range(GQ):
                hd = h * GQ + g
                outs[hd] = (gates_t[hd:hd + 1] * o_c[:, lanes(g)] + gates_t[N_HEADS + hd:N_HEADS + hd + 1] * o_s[:, lanes(g)]
                            + gates_t[2 * N_HEADS + hd:2 * N_HEADS + hd + 1] * o_w[:, lanes(g)])
    o_ref[...] = jnp.concatenate(outs, axis=0).T


def _nsa_prompt_t(q_hm, ck, cvt, ks, vst, kw, vwt, gates, nb):
    m = gates.shape[0]
    s = m // nb
    nq = s // Q_BLOCK
    nc = ck.shape[2]
    kt = vst.shape[3]
    kv = pl.BlockSpec((N_KV, s, HEAD_DIM), lambda b, i: (0, b, 0))
    vt = pl.BlockSpec((N_KV, s // kt, VT_ROWS, kt), lambda b, i: (0, b, 0, 0))
    return pl.pallas_call(
        functools.partial(_nsa_prompt_t_kernel, kt=kt, nc=nc), grid=(nb, nq),
        in_specs=[pl.BlockSpec((N_HEADS, Q_BLOCK, HEAD_DIM), lambda b, i: (0, b * nq + i, 0)),
                  pl.BlockSpec((1, N_KV, nc, HEAD_DIM), lambda b, i: (b, 0, 0, 0)),
                  pl.BlockSpec((1, N_KV, HEAD_DIM, nc), lambda b, i: (b, 0, 0, 0)),
                  kv, vt, kv, vt,
                  pl.BlockSpec((Q_BLOCK, GATE_PAD), lambda b, i: (b * nq + i, 0))],
        out_specs=pl.BlockSpec((Q_BLOCK, D_ATTN), lambda b, i: (b * nq + i, 0)),
        out_shape=jax.ShapeDtypeStruct((m, D_ATTN), F32),
        compiler_params=_params(("parallel", "parallel")), name="nsa_prompt")(q_hm, ck, cvt, ks, vst, kw, vwt, gates)


def _nsa_sample_kernel(pt_ref, *refs, n_pages, past):
    del pt_ref
    it = iter(refs)
    pools = [[next(it) for _ in range(n_pages)] for _ in range(4)]
    new_ref = next(it)
    new = [new_ref.at[0, pl.ds(i, 1)] for i in range(6)]
    bkw_ref, bvw_ref, q_ref, gate_ref, slope_ref, wk_ref, wv_ref, o_ref = [next(it) for _ in range(8)]
    kbuf, vbuf = next(it), next(it)
    kwbuf, vwbuf = next(it), next(it)
    tok = next(it), next(it)
    nc = past // CMP_STRIDE
    n_sel = past // SEL_BLOCK + 1
    rows = N_HEADS
    slope = slope_ref[...]
    own = _iota((rows, D_KV), 1) // HEAD_DIM == _iota((rows, D_KV), 0) // GQ
    q = jnp.where(own, jnp.concatenate([q_ref[0]] * N_KV, axis=1), 0.0)
    qb = q.astype(BF16)

    def new_row(i):
        return new[i][...].astype(BF16).astype(F32)

    def own_heads(o):
        o = jnp.where(own, o, 0.0)
        return o[:, 0:64] + o[:, 64:128] + o[:, 128:192] + o[:, 192:256]

    def pooled(pages, w_ref, x_new):
        for i, p_ref in enumerate(pages):
            pt = p_ref[0].T
            tok[0][i * PAGE:(i + 1) * PAGE, :] = pt[:, 0:128]
            tok[1][i * PAGE:(i + 1) * PAGE, :] = pt[:, 128:256]
        a, b = _pool16(tok, w_ref, past)
        return (a + _shift_up(b, w_ref[CMP_STRIDE:CMP_STRIDE + 1, :] * x_new[...])).astype(BF16)

    def fill(pages, kb, vb):
        for i, (kp, vp) in enumerate(pages):
            kb[:, i * kp.shape[1]:(i + 1) * kp.shape[1]] = kp[...].astype(BF16)
            vb[:, i * vp.shape[1]:(i + 1) * vp.shape[1]] = vp[...].astype(BF16)

    def attend(kb, vb, k_new, v_new, mask, new_ok, dist):
        s = jnp.where(mask, _dot(qb, kb[...]) - slope * dist.astype(F32), NEG)
        s_new = jnp.where(new_ok, jnp.sum(qb.astype(F32) * k_new, axis=-1, keepdims=True), NEG)
        m = jnp.maximum(jnp.max(s, axis=-1, keepdims=True), s_new)
        p = jnp.where(mask, jnp.exp(s - m), 0.0)
        p_new = jnp.where(new_ok, jnp.exp(s_new - m), 0.0)
        l = jnp.sum(p, axis=-1, keepdims=True) + p_new
        o = _dot_nt(p.astype(BF16), vb[...]) + p_new.astype(BF16).astype(F32) * v_new
        return own_heads(o) / jnp.maximum(l, TINY)

    ck = pooled(pools[0], wk_ref, new[0])
    dist_c = past - (_iota((1, nc), 1) * CMP_STRIDE + (CMP_BLOCK - 1))
    mask_c = dist_c >= 0
    s = jnp.where(mask_c, _dot_nt(qb, ck) - slope * dist_c.astype(F32), NEG)
    p = jnp.where(mask_c, jnp.exp(s - jnp.max(s, axis=-1, keepdims=True)), 0.0)
    p = p / jnp.maximum(jnp.sum(p, axis=-1, keepdims=True), TINY)
    p_grp = jnp.concatenate(
        [jnp.broadcast_to(jnp.sum(p[h * GQ:(h + 1) * GQ], axis=0, keepdims=True), (GQ, nc)) for h in range(N_KV)], axis=0)
    imp = _dot3(p_grp, _overlap_matrix(nc, 128, nc, n_sel))
    sel = _select_blocks(imp, jnp.full((rows, 1), past, jnp.int32), n_sel)
    o_c = own_heads(_dot(p.astype(BF16), pooled(pools[1], wv_ref, new[1])))
    wl = bkw_ref.shape[2]
    dist_w = wl - _iota((1, wl), 1)
    fill([(bkw_ref.at[0], bvw_ref.at[0])], kwbuf, vwbuf)
    o_w = attend(kwbuf, vwbuf, new_row(4), new_row(5), (dist_w >= 0) & (dist_w < WINDOW), jnp.full((rows, 1), True), dist_w)
    fill([(k.at[0], v.at[0]) for k, v in zip(pools[2], pools[3])], kbuf, vbuf)
    dist_s = past - _iota((1, past), 1)
    expand = _iota((128, past), 0) == _iota((128, past), 1) // SEL_BLOCK
    hit = _dot(sel.astype(BF16), jnp.where(expand, 1.0, 0.0).astype(BF16))
    o_s = attend(kbuf, vbuf, new_row(2), new_row(3), (hit > 0.5) & (dist_s >= 0), sel[:, n_sel - 1:n_sel] > 0.5, dist_s)
    gates = gate_ref[0]
    o_ref[0] = gates[:, 0:1] * o_c + gates[:, 1:2] * o_s + gates[:, 2:3] * o_w


def _nsa_sample(page_table, pools, new_rows, buf_kw, buf_vw, q, gates_t, slopes, wk, wv):
    nb, n_pages = page_table.shape
    past = n_pages * PAGE
    page_specs = []
    for pool in pools:
        for p in range(n_pages):
            page_specs.append(pl.BlockSpec((1,) + pool.shape[1:], lambda b, pt, p=p: (pt[b, p], 0, 0)))
    per_b = lambda shape: pl.BlockSpec((1,) + shape, lambda b, pt: (b,) + (0,) * len(shape))
    in_specs = (page_specs + [per_b(new_rows.shape[1:])] + [per_b(buf_kw.shape[1:])] * 2
                + [per_b((N_HEADS, HEAD_DIM)), per_b((N_HEADS, 3)),
                   pl.BlockSpec(slopes.shape, lambda b, pt: (0, 0)),
                   pl.BlockSpec(wk.shape, lambda b, pt: (0, 0)), pl.BlockSpec(wv.shape, lambda b, pt: (0, 0))])
    args = [pool for pool in pools for _ in range(n_pages)] + [new_rows, buf_kw, buf_vw, q, gates_t, slopes, wk, wv]
    return pl.pallas_call(
        functools.partial(_nsa_sample_kernel, n_pages=n_pages, past=past),
        grid_spec=pltpu.PrefetchScalarGridSpec(
            num_scalar_prefetch=1, grid=(nb,), in_specs=in_specs,
            out_specs=per_b((N_HEADS, HEAD_DIM)),
            scratch_shapes=([pltpu.VMEM((D_KV, past), BF16)] * 2 + [pltpu.VMEM((D_KV, buf_kw.shape[2]), BF16)] * 2
                            + [pltpu.VMEM((past, D_KV // 2), F32)] * 2)),
        out_shape=jax.ShapeDtypeStruct((nb, N_HEADS, HEAD_DIM), F32),
        compiler_params=_params(("arbitrary",)), name="nsa_sample")(page_table, *args)


def _s5_disc_kernel(lr_ref, li_ref, ldt_ref, ar_ref, ai_ref, fr_ref, fi_ref):
    lr, li = lr_ref[...], li_ref[...]
    dt = jnp.exp(ldt_ref[...])
    mag = jnp.exp(lr * dt)
    ar = mag * jnp.cos(li * dt)
    ai = mag * jnp.sin(li * dt)
    den = lr * lr + li * li
    ar_ref[...] = ar
    ai_ref[...] = ai
    fr_ref[...] = ((ar - 1.0) * lr + ai * li) / den
    fi_ref[...] = (ai * lr - (ar - 1.0) * li) / den


def _s5_bbar_kernel(fr_ref, fi_ref, br_ref, bi_ref, or_ref, oi_ref):
    fr, fi, br, bi = fr_ref[...], fi_ref[...], br_ref[...], bi_ref[...]
    or_ref[...] = fr * br - fi * bi
    oi_ref[...] = fr * bi + fi * br


def _s5_weights(lam_re, lam_im, log_dt, b_re, b_im, c_re, c_im):
    g, n = lam_re.shape
    sd = jax.ShapeDtypeStruct((g, n), F32)
    ar, ai, fr, fi = pl.pallas_call(_s5_disc_kernel, out_shape=[sd] * 4, name="s5_discretise")(
        lam_re, lam_im, log_dt.reshape(g, 1))
    sb = jax.ShapeDtypeStruct((g * n, SSM_GROUP), F32)
    bbr, bbi = pl.pallas_call(_s5_bbar_kernel, out_shape=[sb] * 2, name="s5_bbar")(
        fr.reshape(g * n, 1), fi.reshape(g * n, 1), b_re.reshape(g * n, SSM_GROUP), b_im.reshape(g * n, SSM_GROUP))
    eye = jnp.eye(g // SSM_BANDS, dtype=F32)
    gl = g // SSM_BANDS

    def band_in(bb):
        x = bb.reshape(SSM_BANDS, gl, n, SSM_GROUP).transpose(0, 1, 3, 2)
        return jnp.einsum("jgpn,gh->jgphn", x, eye).reshape(SSM_BANDS, gl * SSM_GROUP, gl * n).astype(BF16)

    def band_out(c):
        x = c.reshape(SSM_BANDS, gl, SSM_GROUP, n).transpose(0, 1, 3, 2)
        return jnp.einsum("jgnp,gh->jgnhp", x, eye).reshape(SSM_BANDS, gl * n, gl * SSM_GROUP).astype(BF16)

    return (ar.reshape(1, g * n), ai.reshape(1, g * n), band_in(bbr), band_in(bbi), band_out(c_re), band_out(-c_im))


def _s5_prompt_kernel(u_ref, wbr_ref, wbi_ref, ar_ref, ai_ref, wcr_ref, wci_ref, d_ref,
                      y_ref, sr_ref, si_ref, hr_s, hi_s, cr_s, ci_s, *, nb):
    c = pl.program_id(1)
    rows, width = hr_s.shape
    rep = 8 // nb

    @pl.when(c == 0)
    def _():
        cr_s[...] = jnp.zeros_like(cr_s)
        ci_s[...] = jnp.zeros_like(ci_s)

    u = u_ref[...]
    ub = u.astype(BF16)
    hr_s[...] = _dot(ub, wbr_ref[0])
    hi_s[...] = _dot(ub, wbi_ref[0])
    ar = jnp.broadcast_to(ar_ref[...], (8, width))
    ai = jnp.broadcast_to(ai_ref[...], (8, width))
    sub = _iota((8, width), 0) // nb

    def step(i, carry):
        sr, si = carry
        base = pl.multiple_of(i * 8, 8)
        xr = hr_s[pl.ds(base, 8), :]
        xi = hi_s[pl.ds(base, 8), :]
        outr = outi = None
        for k in range(rep):
            yr = ar * sr - ai * si + xr
            yi = ar * si + ai * sr + xi
            outr = yr if k == 0 else jnp.where(sub == k, yr, outr)
            outi = yi if k == 0 else jnp.where(sub == k, yi, outi)
            sr, si = pltpu.roll(yr, nb, 0), pltpu.roll(yi, nb, 0)
        hr_s[pl.ds(base, 8), :] = outr
        hi_s[pl.ds(base, 8), :] = outi
        return sr, si

    sr, si = lax.fori_loop(0, rows // 8, step, (cr_s[...], ci_s[...]))
    cr_s[...] = sr
    ci_s[...] = si
    y = _dot(hr_s[...].astype(BF16), wcr_ref[0]) + _dot(hi_s[...].astype(BF16), wci_ref[0]) + d_ref[...] * u
    y_ref[...] = jax.nn.gelu(y)

    @pl.when(c == pl.num_programs(1) - 1)
    def _():
        sr_ref[...] = sr[0:nb]
        si_ref[...] = si[0:nb]


def _s5_prompt(u_tm, weights, d_skip, nb, tc):
    ar, ai, wbr, wbi, wcr, wci = weights
    rows = u_tm.shape[0]
    cw = D_SSM // SSM_BANDS
    sw = ar.shape[1] // SSM_BANDS
    blk = tc * nb
    tile = pl.BlockSpec((blk, cw), lambda j, c: (c, j))
    band = lambda a: pl.BlockSpec((1,) + a.shape[1:], lambda j, c: (j, 0, 0))
    vec = lambda w: pl.BlockSpec((1, w), lambda j, c: (0, j))
    state = pl.BlockSpec((nb, sw), lambda j, c: (0, j))
    return pl.pallas_call(
        functools.partial(_s5_prompt_kernel, nb=nb), grid=(SSM_BANDS, rows // blk),
        in_specs=[tile, band(wbr), band(wbi), vec(sw), vec(sw), band(wcr), band(wci), vec(cw)],
        out_specs=[tile, state, state],
        out_shape=[jax.ShapeDtypeStruct((rows, D_SSM), F32)] + [jax.ShapeDtypeStruct((nb, ar.shape[1]), F32)] * 2,
        scratch_shapes=[pltpu.VMEM((blk, sw), F32)] * 2 + [pltpu.VMEM((8, sw), F32)] * 2,
        compiler_params=_params(("parallel", "arbitrary")), name="s5_prompt")(
            u_tm, wbr, wbi, ar, ai, wcr, wci, d_skip)


def _s5_sample_kernel(u_ref, h0r_ref, h0i_ref, wbr_ref, wbi_ref, ar_ref, ai_ref, wcr_ref, wci_ref, d_ref,
                      y_ref, sr_ref, si_ref):
    cw = D_SSM // SSM_BANDS
    sw = ar_ref.shape[1] // SSM_BANDS
    for j in range(SSM_BANDS):
        cs = slice(j * cw, (j + 1) * cw)
        ss = slice(j * sw, (j + 1) * sw)
        u = u_ref[:, cs]
        ub = u.astype(BF16)
        ar, ai = ar_ref[:, ss], ai_ref[:, ss]
        h0r, h0i = h0r_ref[:, ss], h0i_ref[:, ss]
        hr = _dot(ub, wbr_ref[j]) + (ar * h0r - ai * h0i)
        hi = _dot(ub, wbi_ref[j]) + (ar * h0i + ai * h0r)
        sr_ref[:, ss] = hr
        si_ref[:, ss] = hi
        y = _dot(hr.astype(BF16), wcr_ref[j]) + _dot(hi.astype(BF16), wci_ref[j]) + d_ref[:, cs] * u
        y_ref[:, cs] = jax.nn.gelu(y)


def _s5_sample(u, h0r, h0i, weights, d_skip):
    ar, ai, wbr, wbi, wcr, wci = weights
    nb = u.shape[0]
    st = jax.ShapeDtypeStruct(h0r.shape, F32)
    return pl.pallas_call(
        _s5_sample_kernel, out_shape=[jax.ShapeDtypeStruct((nb, D_SSM), F32), st, st],
        compiler_params=pltpu.CompilerParams(vmem_limit_bytes=VMEM_LIMIT), name="s5_sample")(
            u, h0r, h0i, wbr, wbi, ar, ai, wcr, wci, d_skip)


def _xattn_prompt_kernel(q_ref, k_ref, v_ref, o_ref):
    scale = X_HEAD_DIM ** -0.5
    for h in range(X_HEADS):
        cs = slice(h * X_HEAD_DIM, (h + 1) * X_HEAD_DIM)
        s = _dot_nt(q_ref[:, cs], k_ref[:, cs].astype(BF16)) * scale
        p = jnp.exp(s - jnp.max(s, axis=-1, keepdims=True))
        p = p / jnp.sum(p, axis=-1, keepdims=True)
        o_ref[:, cs] = _dot(p.astype(BF16), v_ref[:, cs].astype(BF16)).astype(o_ref.dtype)


def _xattn_prompt(q, mk, mv, nb, tm):
    m = q.shape[0]
    nt = m // nb // tm
    mem = pl.BlockSpec((MEM_LEN, D_MODEL), lambda b, i: (b, 0))
    row = pl.BlockSpec((tm, D_MODEL), lambda b, i: (b * nt + i, 0))
    return pl.pallas_call(
        _xattn_prompt_kernel, grid=(nb, nt), in_specs=[row, mem, mem], out_specs=row,
        out_shape=jax.ShapeDtypeStruct((m, D_MODEL), BF16),
        compiler_params=_params(("parallel", "parallel")), name="xattn_prompt")(q, mk, mv)


def _xattn_sample_kernel(q_ref, k_ref, v_ref, o_ref):
    scale = X_HEAD_DIM ** -0.5
    nt = X_HEAD_DIM // 128
    rows = MEM_LEN * nt * X_HEADS

    def head(ref, h):
        parts = [ref[0, pl.ds(t * X_HEADS + h, MEM_LEN, stride=nt * X_HEADS), :] for t in range(nt)]
        return jnp.concatenate(parts, axis=1).astype(BF16)

    assert k_ref.shape[1] == rows
    for h in range(X_HEADS):
        cs = slice(h * X_HEAD_DIM, (h + 1) * X_HEAD_DIM)
        q = jnp.broadcast_to(q_ref[0, :, cs], (8, X_HEAD_DIM)).astype(BF16)
        s = _dot_nt(q, head(k_ref, h)) * scale
        p = jnp.exp(s - jnp.max(s, axis=-1, keepdims=True))
        p = p / jnp.sum(p, axis=-1, keepdims=True)
        o_ref[0, :, cs] = _dot(p.astype(BF16), head(v_ref, h))[0:1].astype(o_ref.dtype)


def _xattn_sample(q, mk, mv):
    nb = q.shape[0]
    mem = pl.BlockSpec((1,) + mk.shape[1:], lambda b: (b, 0, 0))
    row = pl.BlockSpec((1, 1, D_MODEL), lambda b: (b, 0, 0))
    return pl.pallas_call(
        _xattn_sample_kernel, grid=(nb,), in_specs=[row, mem, mem], out_specs=row,
        out_shape=jax.ShapeDtypeStruct((nb, 1, D_MODEL), BF16),
        compiler_params=_params(("parallel",)), name="xattn_sample")(q, mk, mv)


def _top_distinct(x, n):
    rows = []
    for _ in range(n):
        m = jnp.max(x, axis=0, keepdims=True)
        rows.append(m)
        x = jnp.where(x == m, LOWEST, x)
    return jnp.concatenate(rows, axis=0)


def _peer_route_kernel(x_ref, g_ref, w_ref, k1_ref, k2_ref, xt_ref, thr_ref, s2_ref, e2_ref, c1_ref):
    xn = _rms(x_ref[...], g_ref[...])
    xt_ref[...] = xn.T.astype(BF16)
    q = _dot(xn.astype(BF16), w_ref[...]).astype(BF16)
    half = PEER_KEYS
    for h in range(PEER_HEADS):
        s1 = _dot_nt(k1_ref[...], q[:, 2 * half * h:2 * half * h + half])
        s2 = _dot_nt(k2_ref[...], q[:, 2 * half * h + half:2 * half * (h + 1)])
        d1 = _top_distinct(s1, PEER_TOPK)
        d2 = _top_distinct(s2, PEER_TOPK)
        cand = jnp.concatenate([d1[a:a + 1] + d2[0:PEER_TOPK // (a + 1)] for a in range(PEER_TOPK)], axis=0)
        tau = _top_distinct(cand, PEER_TOPK)[PEER_TOPK - 1:PEER_TOPK]
        top = d1[0:1] + d2[0:1]
        z = jnp.sum(jnp.where(cand >= tau, jnp.exp(cand - top), 0.0), axis=0, keepdims=True)
        thr = jnp.full(s1.shape, HUGE, F32)
        for a in range(PEER_TOPK):
            ok = (d1[a:a + 1] + d2) >= tau
            thr_a = jnp.min(jnp.where(ok, d2, HUGE), axis=0, keepdims=True)
            thr = jnp.where(s1 == d1[a:a + 1], thr_a, thr)
        thr_ref[h] = thr
        s2_ref[h] = s2
        e2_ref[h] = jnp.exp(s2 - d2[0:1])
        c1_ref[h] = jnp.exp(s1 - d1[0:1]) / z


def _peer_route(x, g, w_pq, k1, k2, tm):
    m = x.shape[0]
    hk = pl.BlockSpec((PEER_HEADS, PEER_KEYS, tm), lambda i: (0, 0, i))
    hks = jax.ShapeDtypeStruct((PEER_HEADS, PEER_KEYS, m), F32)
    return pl.pallas_call(
        _peer_route_kernel, grid=(m // tm,),
        in_specs=[pl.BlockSpec((tm, D_MODEL), lambda i: (i, 0)), _full(g), _full(w_pq), _full(k1), _full(k2)],
        out_specs=[pl.BlockSpec((D_MODEL, tm), lambda i: (0, i)), hk, hk, hk, hk],
        out_shape=[jax.ShapeDtypeStruct((D_MODEL, m), BF16), hks, hks, hks, hks],
        compiler_params=_params(("parallel",)), name="peer_route")(x, g, w_pq, k1, k2)


def _peer_kernel(xt_ref, u_ref, vt_ref, thr_ref, s2_ref, e2_ref, c1_ref, o_ref, wa_s, wb_s, act_s):
    e = pl.program_id(1)
    et, tt = act_s.shape
    n_sub = et // PEER_KEYS
    group = 2
    oc = o_ref.shape[0] // n_sub

    @pl.when(e == 0)
    def _():
        o_ref[...] = jnp.zeros_like(o_ref)
        wb_s[...] = jnp.zeros_like(wb_s)

    def step(cur_s, prev_s):
        def score(c):
            rs = slice(c * SCORE_ROWS, (c + 1) * SCORE_ROWS)
            act_s[rs, :] = _dot(u_ref[rs, :], xt_ref[...])

        def emit(c):
            rs = slice(c * EMIT_ROWS, (c + 1) * EMIT_ROWS)
            o_ref[rs, :] += _dot(vt_ref[rs, :], prev_s[...])

        def gate_group(j0, lt):
            ls = slice(lt * 128, (lt + 1) * 128)
            for part in range(PEER_KEYS // GATE_ROWS):
                ks = slice(part * GATE_ROWS, (part + 1) * GATE_ROWS)
                gates = [None] * group
                for h in range(PEER_HEADS):
                    s2 = s2_ref[h, ks, ls]
                    e2 = e2_ref[h, ks, ls]
                    for k in range(group):
                        jj = j0 + k
                        g = jnp.where(s2 >= thr_ref[h, jj:jj + 1, ls], e2, 0.0) * c1_ref[h, jj:jj + 1, ls]
                        gates[k] = g if gates[k] is None else gates[k] + g
                for k in range(group):
                    r0 = (j0 + k) * PEER_KEYS + part * GATE_ROWS
                    rs = slice(r0, r0 + GATE_ROWS)
                    cur_s[rs, ls] = (gates[k] * jax.nn.gelu(act_s[rs, ls])).astype(BF16)

        scores = list(range(et // SCORE_ROWS))
        emits = list(range(o_ref.shape[0] // EMIT_ROWS))
        for j0 in range(0, n_sub, group):
            while scores and scores[0] * SCORE_ROWS < (j0 + group) * PEER_KEYS:
                score(scores.pop(0))
            for lt in range(tt // 128):
                gate_group(j0, lt)
                if scores:
                    score(scores.pop(0))
                elif emits:
                    emit(emits.pop(0))
        for c in emits:
            emit(c)

    @pl.when(e % 2 == 0)
    def _():
        step(wa_s, wb_s)

    @pl.when(e % 2 == 1)
    def _():
        step(wb_s, wa_s)


def _peer(xt, u, vt, thr, s2, e2, c1, tt, et):
    m = xt.shape[1]
    n_e = u.shape[0] // et
    tok3 = lambda r: pl.BlockSpec((PEER_HEADS, r, tt), lambda i, e: (0, 0, i))
    tile3 = pl.BlockSpec((PEER_HEADS, et // PEER_KEYS, tt), lambda i, e: (0, jnp.minimum(e, n_e - 1), i))
    return pl.pallas_call(
        _peer_kernel, grid=(m // tt, n_e + 1),
        in_specs=[pl.BlockSpec((D_MODEL, tt), lambda i, e: (0, i)),
                  pl.BlockSpec((et, D_MODEL), lambda i, e: (jnp.minimum(e, n_e - 1), 0)),
                  pl.BlockSpec((D_MODEL, et), lambda i, e: (0, jnp.maximum(e - 1, 0))),
                  tile3, tok3(PEER_KEYS), tok3(PEER_KEYS), tile3],
        out_specs=pl.BlockSpec((D_MODEL, tt), lambda i, e: (0, i)),
        out_shape=jax.ShapeDtypeStruct((D_MODEL, m), F32),
        scratch_shapes=[pltpu.VMEM((et, tt), BF16)] * 2 + [pltpu.VMEM((et, tt), F32)],
        compiler_params=_params(("parallel", "arbitrary")), name="peer_dense")(
            xt, u, vt, thr, s2, e2, c1)


def _final_kernel(h_ref, ot_ref, g_ref, y_ref):
    y_ref[...] = _rms(h_ref[...] + ot_ref[...].T, g_ref[...])


def _final_norm(h, out_t, g, tm):
    m = h.shape[0]
    row = pl.BlockSpec((tm, D_MODEL), lambda i: (i, 0))
    return pl.pallas_call(
        _final_kernel, grid=(m // tm,),
        in_specs=[row, pl.BlockSpec((D_MODEL, tm), lambda i: (0, i)), _full(g)], out_specs=row,
        out_shape=jax.ShapeDtypeStruct((m, D_MODEL), F32),
        compiler_params=_params(("parallel",)), name="final_norm")(h, out_t, g)


def _row(v):
    return v.reshape(1, -1).astype(F32)


def _cmp_rows(w):
    return jnp.repeat(w.T.astype(F32), HEAD_DIM, axis=1)


def kernel(x_prompt, x_sample, mem_prompt, cache_k_cmp, cache_v_cmp, cache_k_sel, cache_v_sel, cache_k_win, cache_v_win, state_s5_re, state_s5_im, cache_mem_k, cache_mem_v, page_table, g_mix, w_in, w_cmp_k, w_cmp_v, lam_re, lam_im, log_dt, b_re, b_im, c_re, c_im, d_skip, w_glu, g_attn_out, g_ssm_out, w_out, g_x, g_mem, w_xq, w_xk, w_xv, w_xo, g_ffn, w_pq, peer_k1, peer_k2, peer_u, peer_v, g_final):
    nb, seq, _ = x_prompt.shape
    db = x_sample.shape[0]
    depth = g_mix.shape[0]
    assert depth == 1 and x_sample.shape[1] == 1
    l = 0
    mp = nb * seq
    n_gate = 3 * N_HEADS
    kv_end = D_ATTN + 6 * D_KV

    w_attn = jnp.concatenate([w_in[l][:, :kv_end + n_gate], jnp.zeros((D_MODEL, GATE_PAD - n_gate), F32)], axis=1).astype(BF16)
    w_u = w_in[l][:, kv_end + n_gate:].astype(BF16)
    w_full = jnp.concatenate([w_attn, w_u], axis=1)
    w_glu_b = w_glu[l].astype(BF16)
    w_out_a = w_out[l][:D_ATTN].astype(BF16)
    w_out_s = w_out[l][D_ATTN:].astype(BF16)
    w_mem = jnp.concatenate([w_xk[l], w_xv[l]], axis=1).astype(BF16)
    w_xq_b = w_xq[l].astype(BF16)
    w_xo_b = w_xo[l].astype(BF16)
    w_pq_b = w_pq[l].astype(BF16)
    k1_b = peer_k1[l].astype(BF16)
    k2_b = peer_k2[l].astype(BF16)
    u_b = peer_u[l].astype(BF16)
    vt_b = peer_v[l].T.astype(BF16)
    wk_rows = _cmp_rows(w_cmp_k[l])
    wv_rows = _cmp_rows(w_cmp_v[l])
    gm, gx, gf, gfin = _row(g_mix[l]), _row(g_x[l]), _row(g_ffn[l]), _row(g_final)
    ga, gs, gme = _row(g_attn_out[l]), _row(g_ssm_out[l]), _row(g_mem[l])
    s5w = _s5_weights(lam_re[l], lam_im[l], log_dt[l], b_re[l], b_im[l], c_re[l], c_im[l])
    dsk = _row(d_skip[l])

    xp = x_prompt.reshape(mp, D_MODEL)
    q_hm, kv, k_hm, gates, v_t, cmp_ft = _proj_attn(xp, gm, w_attn, nb, tm=256)
    u_tm = _proj_u(xp, gm, w_u, nb, tm=256).reshape(seq * nb, D_SSM)
    ck, cv_t = _compress(kv[0], kv[1], wk_rows, wv_rows, nb)
    o_a = _nsa_prompt_t(q_hm, ck, cv_t, k_hm[0], v_t[0], k_hm[1], v_t[1], gates, nb)
    y_tm, p_sr, p_si = _s5_prompt(u_tm, s5w, dsk, nb, tc=256)
    o_s = _glu(y_tm.reshape(seq, nb * D_SSM), w_glu_b, nb, tm=256)
    h1 = _merge(o_a, o_s, ga, gs, w_out_a, w_out_s, xp, tm=512)
    mk, mv = _mem_kv(mem_prompt.reshape(nb * MEM_LEN, D_MODEL), gme, w_mem, tm=256)
    xq = _mm(h1, w_xq_b, 512, BF16, g=gx, name="xattn_q")
    xo = _xattn_prompt(xq, mk, mv, nb, tm=256)
    h2 = _mm(xo, w_xo_b, 512, F32, res=h1, name="xattn_o")
    routed = _peer_route(h2, gf, w_pq_b, k1_b, k2_b, tm=256)
    y_p = _final_norm(h2, _peer(routed[0], u_b, vt_b, *routed[1:], tt=512, et=1024), gfin, tm=256)

    xs = x_sample.reshape(db, D_MODEL)
    z = _mm(xs, w_full, db, F32, g=gm, gate_cols=(kv_end, kv_end + n_gate), name="proj_sample")
    new_rows = [z[:, D_ATTN + D_KV * i:D_ATTN + D_KV * (i + 1)] for i in range(6)]
    n_phys = cache_k_cmp.shape[1]
    pools = [c[l].transpose(0, 2, 3, 1).reshape(n_phys, D_KV, PAGE)
             for c in (cache_k_cmp, cache_v_cmp, cache_k_sel, cache_v_sel)]
    wl = cache_k_win.shape[2]
    buf_kw = cache_k_win[l].transpose(0, 2, 3, 1).reshape(db, D_KV, wl)
    buf_vw = cache_v_win[l].transpose(0, 2, 3, 1).reshape(db, D_KV, wl)
    q_s = (z[:, :D_ATTN] * (HEAD_DIM ** -0.5)).reshape(db, N_HEADS, HEAD_DIM)
    gates_t = z[:, kv_end:kv_end + n_gate].reshape(db, 3, N_HEADS).transpose(0, 2, 1)
    slopes = jnp.asarray(np.array([[_alibi_slope(i)] for i in range(N_HEADS)], np.float32))
    o_a_s = _nsa_sample(page_table, pools, z[:, D_ATTN:kv_end].reshape(db, 6, D_KV), buf_kw, buf_vw,
                        q_s, gates_t, slopes, wk_rows, wv_rows).reshape(db, D_ATTN)
    y_s, s_sr, s_si = _s5_sample(z[:, kv_end + GATE_PAD:], state_s5_re[l].reshape(db, -1), state_s5_im[l].reshape(db, -1), s5w, dsk)
    o_s_s = _glu(y_s, w_glu_b, 1, tm=db)
    h1s = _merge(o_a_s, o_s_s, ga, gs, w_out_a, w_out_s, xs, tm=db)
    xq_s = _mm(h1s, w_xq_b, db, F32, g=gx, name="xattn_q_sample")
    nt = X_HEAD_DIM // 128

    def mem_rows(c):
        return c.reshape(db, MEM_LEN, X_HEADS, nt, 128).transpose(0, 1, 3, 2, 4).reshape(db, MEM_LEN * nt * X_HEADS, 128)

    xo_s = _xattn_sample(xq_s.reshape(db, 1, D_MODEL), mem_rows(cache_mem_k[l]), mem_rows(cache_mem_v[l])).reshape(db, D_MODEL)
    h2s = _mm(xo_s, w_xo_b, db, F32, res=h1s, name="xattn_o_sample")
    routed_s = _peer_route(h2s, gf, w_pq_b, k1_b, k2_b, tm=db)
    y_s_out = _final_norm(h2s, _peer(routed_s[0], u_b, vt_b, *routed_s[1:], tt=db, et=1024), gfin, tm=db)

    wlp = min(WINDOW, seq)
    p_kv = [a.reshape(1, nb, N_KV, HEAD_DIM, seq).transpose(0, 1, 4, 2, 3) for a in list(cmp_ft) + list(kv[2:])]
    p_win = [a[:, :, seq - wlp:] for a in p_kv[4:6]]
    s_new = [r.reshape(1, db, 1, N_KV, HEAD_DIM) for r in new_rows]
    s_win = [jnp.concatenate([c[l], n[0]], axis=1)[None, :, -min(WINDOW, wl + 1):] for c, n in ((cache_k_win, s_new[4]), (cache_v_win, s_new[5]))]
    g64 = (1, -1, N_SSM_GROUPS, SSM_STATE)
    return (y_p.reshape(nb, seq, D_MODEL), y_s_out.reshape(db, 1, D_MODEL),
            p_kv[0], p_kv[1], p_kv[2], p_kv[3], p_win[0], p_win[1],
            p_sr.reshape(g64), p_si.reshape(g64),
            mk.reshape(1, nb, MEM_LEN, X_HEADS, X_HEAD_DIM), mv.reshape(1, nb, MEM_LEN, X_HEADS, X_HEAD_DIM),
            s_new[0], s_new[1], s_new[2], s_new[3], s_win[0], s_win[1],
            s_sr.reshape(g64), s_si.reshape(g64))
```
